```python
import math
import jax
import jax.numpy as jnp
from jax import lax
import numpy as np

D_MODEL = 1024
BATCH = 2
SEQ = 8192
DEPTH = 2
DEC_BATCH = 128
DEC_SEQ = 4
PAST_LEN = 2048
PAGE_SIZE = 128

HA = 4
DQK_A = 64
DV_A = 2 * DQK_A
KROW_A = 2 * DQK_A
ROT_DIM = DQK_A // 4
ROPE_THETA = 500000.0
Q_BLOCK = 128
HB = 4
DK_B = 64
DV_B = 64
GATE_RANK = 16
GATE_NORM = 16.0
GLA_CHUNK = 16
HC = 4
DC = 64
CMLP_CHUNK = 128
D_MIX = HA * DV_A + HB * DV_B + HC * DC
SIZES = (HA * 2 * DQK_A, HA * 2 * DQK_A, HA * DV_A,
         HB * DK_B, HB * DK_B, HB * DV_B, HB * DV_B, GATE_RANK,
         HC * DC, HC * DC)
D_IN = sum(SIZES)
N_EXPERTS = 32
TOP_K = 4
D_FF = D_MODEL
SWIGLU_LIMIT = 7.0
SWIGLU_ALPHA = 1.702
MOE_BLOCK = 128
LN_EPS = 1e-5
RMS_EPS = 1e-6
DEEPNORM_ALPHA = (2 * DEPTH) ** 0.25
DEEPNORM_BETA = (8 * DEPTH) ** -0.25
NEG_INF = -1e30

kernel_name = 'hymba_diffattn_gla_cmlp_moe_step'


def layer_norm(x, g, b):
    xf = x.astype(jnp.float32)
    mu = jnp.mean(xf, axis=-1, keepdims=True)
    var = jnp.mean(jnp.square(xf - mu), axis=-1, keepdims=True)
    y = (xf - mu) * lax.rsqrt(var + LN_EPS) * g.astype(jnp.float32) + b.astype(jnp.float32)
    return y.astype(x.dtype)


def rms_norm(x, g):
    xf = x.astype(jnp.float32)
    y = xf * lax.rsqrt(jnp.mean(jnp.square(xf), axis=-1, keepdims=True) + RMS_EPS) * g.astype(jnp.float32)
    return y.astype(x.dtype)


def rope(x, pos):
    half = ROT_DIM // 2
    inv_freq = ROPE_THETA ** (-jnp.arange(0, ROT_DIM, 2, dtype=jnp.float32) / ROT_DIM)
    ang = pos.astype(jnp.float32)[:, None] * inv_freq[None, :]
    cos = jnp.cos(ang)[:, None, None, :]
    sin = jnp.sin(ang)[:, None, None, :]
    xr = x[..., :ROT_DIM].astype(jnp.float32)
    x1, x2 = xr[..., :half], xr[..., half:]
    rot = jnp.concatenate([x1 * cos - x2 * sin, x2 * cos + x1 * sin], axis=-1).astype(x.dtype)
    return jnp.concatenate([rot, x[..., ROT_DIM:]], axis=-1)


def diff_lambda(lq1, lk1, lq2, lk2, lam_init):
    f = jnp.float32
    return (jnp.exp(jnp.sum(lq1.astype(f) * lk1.astype(f)))
            - jnp.exp(jnp.sum(lq2.astype(f) * lk2.astype(f))) + lam_init)


def diff_combine(s, lam):
    p = jax.nn.softmax(s, axis=-1)
    return p[:, :, 0] - lam * p[:, :, 1]


def diff_attn_prompt(q, k, v, lam):
    b_, s_len = q.shape[0], q.shape[1]
    nb = s_len // Q_BLOCK
    scale = DQK_A ** -0.5
    qb = q.reshape(b_, nb, Q_BLOCK, HA, 2, DQK_A).swapaxes(0, 1)
    kpos = jnp.arange(s_len)

    def block(args):
        i, qi = args
        s = jnp.einsum('bqhmd,bkhmd->bhmqk', qi, k, preferred_element_type=jnp.float32) * scale
        qpos = i * Q_BLOCK + jnp.arange(Q_BLOCK)
        s = jnp.where(kpos[None, :] <= qpos[:, None], s, NEG_INF)
        a = diff_combine(s, lam).astype(v.dtype)
        return jnp.einsum('bhqk,bkhd->bqhd', a, v)

    o = lax.map(block, (jnp.arange(nb), qb))
    return o.swapaxes(0, 1).reshape(b_, s_len, HA, DV_A)


def diff_attn_sample(q, k, v, k_past, v_past, lam):
    l_new = q.shape[1]
    p_len = k_past.shape[1]
    scale = DQK_A ** -0.5
    s_past = jnp.einsum('bqhmd,bkhmd->bhmqk', q, k_past, preferred_element_type=jnp.float32) * scale
    s_new = jnp.einsum('bqhmd,bkhmd->bhmqk', q, k, preferred_element_type=jnp.float32) * scale
    causal = jnp.tril(jnp.ones((l_new, l_new), dtype=bool))
    s = jnp.concatenate([s_past, jnp.where(causal, s_new, NEG_INF)], axis=-1)
    a = diff_combine(s, lam).astype(v.dtype)
    return (jnp.einsum('bhqk,bkhd->bqhd', a[..., :p_len], v_past)
            + jnp.einsum('bhqk,bkhd->bqhd', a[..., p_len:], v))


def gla_chunked(q, k, v, log_a, s0):
    b_, l_len, h = q.shape[0], q.shape[1], q.shape[2]
    c = GLA_CHUNK if l_len % GLA_CHUNK == 0 else l_len
    n = l_len // c

    def chunks(t):
        return t.astype(jnp.float32).reshape(b_, n, c, h, t.shape[-1]).transpose(1, 0, 3, 2, 4)

    qf = chunks(q) * (DK_B ** -0.5)
    kf, vf = chunks(k), chunks(v)
    bcum = jnp.cumsum(chunks(log_a), axis=3)
    b_last = bcum[:, :, :, -1:, :]
    q_in = qf * jnp.exp(bcum)
    k_in = kf * jnp.exp(-bcum)
    k_end = kf * jnp.exp(b_last - bcum)
    att = jnp.einsum('nbhcd,nbhsd->nbhcs', q_in, k_in)
    att = jnp.where(jnp.tril(jnp.ones((c, c), dtype=bool)), att, 0.0)
    o_intra = jnp.einsum('nbhcs,nbhse->nbhce', att, vf)

    def step(s, inp):
        qi, ke, vi, dl = inp
        o = jnp.einsum('bhcd,bhde->bhce', qi, s)
        s = dl[..., None] * s + jnp.einsum('bhcd,bhce->bhde', ke, vi)
        return s, o

    s_fin, o_inter = lax.scan(step, s0.astype(jnp.float32),
                              (q_in, k_end, vf, jnp.exp(b_last[:, :, :, 0, :])))
    o = (o_intra + o_inter).transpose(1, 0, 3, 2, 4).reshape(b_, l_len, h, v.shape[-1])
    return o, s_fin


def chunk_mlp(u, v, ln_g, ln_b, ws, bs):
    b_, l_len = u.shape[0], u.shape[1]
    vn = layer_norm(v, ln_g, ln_b)
    lc = min(l_len, CMLP_CHUNK)
    n = l_len // lc
    wm = jnp.tril(ws[:, :lc, :lc]).astype(vn.dtype)
    vc = vn.reshape(b_, n, lc, HC, DC)
    mixed = jnp.einsum('gts,bnsgc->bntgc', wm, vc) + bs[:, :lc].T[:, :, None].astype(vn.dtype)
    out = u * mixed.reshape(b_, l_len, HC, DC)
    return out.reshape(b_, l_len, HC * DC), vn.reshape(b_, l_len, HC * DC)


def mixing_sublayer(h, pos, gla_s0, attend, w_in_l, lam_init, attn_norm_g_l, gla_w_gate_l, gla_b_gate_l,
                    gla_norm_g_l, cmlp_ln_g_l, cmlp_ln_b_l, cmlp_ws_l, cmlp_bs_l, w_o_l):
    b_, l_len = h.shape[0], h.shape[1]
    z = h @ w_in_l
    split_points = [int(i) for i in np.cumsum(SIZES)[:-1]]
    aq, ak, av, bq, bk, bv, bg, br, cu, cv = jnp.split(z, split_points, axis=-1)
    aq = rope(aq.reshape(b_, l_len, HA, 2, DQK_A), pos)
    ak = rope(ak.reshape(b_, l_len, HA, 2, DQK_A), pos)
    av = av.reshape(b_, l_len, HA, DV_A)
    o_a = rms_norm(attend(aq, ak, av), attn_norm_g_l) * (1.0 - lam_init)
    log_a = jax.nn.log_sigmoid((br @ gla_w_gate_l + gla_b_gate_l).astype(jnp.float32)) / GATE_NORM
    o_b, s_b = gla_chunked(bq.reshape(b_, l_len, HB, DK_B), bk.reshape(b_, l_len, HB, DK_B),
                           bv.reshape(b_, l_len, HB, DV_B), log_a.reshape(b_, l_len, HB, DK_B), gla_s0)
    o_b = rms_norm(o_b, gla_norm_g_l) * jax.nn.silu(bg.reshape(b_, l_len, HB, DV_B).astype(jnp.float32))
    o_c, vn = chunk_mlp(cu.reshape(b_, l_len, HC, DC), cv.reshape(b_, l_len, HC, DC),
                        cmlp_ln_g_l, cmlp_ln_b_l, cmlp_ws_l, cmlp_bs_l)
    mix = jnp.concatenate([o_a.reshape(b_, l_len, HA * DV_A).astype(h.dtype),
                           o_b.reshape(b_, l_len, HB * DV_B).astype(h.dtype),
                           o_c.astype(h.dtype)], axis=-1)
    return mix @ w_o_l, ak.reshape(b_, l_len, HA, KROW_A), av, s_b, vn


def moe(h, router_w_l, router_b_l, w1, b1, w2, b2):
    b_, l_len, d = h.shape
    x = h.reshape(-1, d)
    t = x.shape[0]
    logits = x.astype(jnp.float32) @ router_w_l.astype(jnp.float32) + router_b_l.astype(jnp.float32)
    top_v, top_i = lax.top_k(logits, TOP_K)
    gate = jax.nn.softmax(top_v, axis=-1)
    m = t * TOP_K
    flat_e = top_i.reshape(m)
    flat_tok = jnp.repeat(jnp.arange(t), TOP_K)
    order = jnp.argsort(flat_e)
    sorted_e = flat_e[order]
    sorted_tok = flat_tok[order]
    sorted_g = gate.reshape(m)[order]
    counts = jnp.bincount(flat_e, length=N_EXPERTS)
    padded = ((counts + MOE_BLOCK - 1) // MOE_BLOCK) * MOE_BLOCK
    pad_end = jnp.cumsum(padded)
    pad_start = pad_end - padded
    srt_start = jnp.cumsum(counts) - counts
    dest = pad_start[sorted_e] + (jnp.arange(m) - srt_start[sorted_e])
    nb = -(-m // MOE_BLOCK) + N_EXPERTS
    block_e = jnp.clip(jnp.searchsorted(pad_end, jnp.arange(nb) * MOE_BLOCK, side='right'), 0, N_EXPERTS - 1)
    x_pad = jnp.zeros((nb * MOE_BLOCK, d), x.dtype).at[dest].set(x[sorted_tok])

    def expert_block(args):
        xb, e = args
        hc = xb @ w1[e] + b1[e]
        g_ = jnp.minimum(hc[:, :D_FF], SWIGLU_LIMIT)
        u_ = jnp.clip(hc[:, D_FF:], -SWIGLU_LIMIT, SWIGLU_LIMIT)
        act = (u_ + 1.0) * g_ * jax.nn.sigmoid(SWIGLU_ALPHA * g_)
        return act @ w2[e] + b2[e]

    y_pad = lax.map(expert_block, (x_pad.reshape(nb, MOE_BLOCK, d), block_e)).reshape(nb * MOE_BLOCK, d)
    y = y_pad[dest] * sorted_g[:, None].astype(x.dtype)
    out = jnp.zeros_like(x).at[sorted_tok].add(y)
    return out.reshape(b_, l_len, d)


def setup_inputs(seed: int = 0) -> dict:
    key = jax.random.key(seed)
    ks = iter(jax.random.split(key, 40))

    def nrm(shape, scale):
        return jax.random.normal(next(ks), shape, jnp.float32) * scale

    n_pages = PAST_LEN // PAGE_SIZE
    n_used = DEC_BATCH * n_pages
    n_phys = n_used + max(1, n_used // 4)
    x_prompt = nrm((BATCH, SEQ, D_MODEL), 1.0)
    x_sample = nrm((DEC_BATCH, DEC_SEQ, D_MODEL), 1.0)
    cache_k = nrm((DEPTH, n_phys, PAGE_SIZE, HA, KROW_A), 1.0)
    cache_v = nrm((DEPTH, n_phys, PAGE_SIZE, HA, DV_A), 1.0)
    page_table = jax.random.permutation(next(ks), n_phys)[:n_used].reshape(DEC_BATCH, n_pages).astype(jnp.int32)
    state_gla = nrm((DEPTH, DEC_BATCH, HB, DK_B, DV_B), 1.0)
    return {
        'x_prompt': x_prompt,
        'x_sample': x_sample,
        'cache_k': cache_k,
        'cache_v': cache_v,
        'page_table': page_table,
        'state_gla': state_gla,
        'w_in': nrm((DEPTH, D_MODEL, D_IN), D_MODEL ** -0.5),
        'lam_q1': nrm((DEPTH, DQK_A), 0.1),
        'lam_k1': nrm((DEPTH, DQK_A), 0.1),
        'lam_q2': nrm((DEPTH, DQK_A), 0.1),
        'lam_k2': nrm((DEPTH, DQK_A), 0.1),
        'attn_norm_g': 1.0 + nrm((DEPTH, DV_A), 0.02),
        'gla_w_gate': nrm((DEPTH, GATE_RANK, HB * DK_B), GATE_RANK ** -0.5),
        'gla_b_gate': nrm((DEPTH, HB * DK_B), 0.1),
        'gla_norm_g': 1.0 + nrm((DEPTH, DV_B), 0.02),
        'cmlp_ln_g': 1.0 + nrm((DEPTH, DC), 0.02),
        'cmlp_ln_b': nrm((DEPTH, DC), 0.02),
        'cmlp_ws': nrm((DEPTH, HC, CMLP_CHUNK, CMLP_CHUNK), CMLP_CHUNK ** -0.5),
        'cmlp_bs': 1.0 + nrm((DEPTH, HC, CMLP_CHUNK), 0.02),
        'w_o': nrm((DEPTH, D_MIX, D_MODEL), DEEPNORM_BETA * D_MIX ** -0.5),
        'ln1_g': 1.0 + nrm((DEPTH, D_MODEL), 0.02),
        'ln1_b': nrm((DEPTH, D_MODEL), 0.02),
        'router_w': nrm((DEPTH, D_MODEL, N_EXPERTS), D_MODEL ** -0.5),
        'router_b': nrm((DEPTH, N_EXPERTS), 0.01),
        'exp_w1': nrm((DEPTH, N_EXPERTS, D_MODEL, 2 * D_FF), D_MODEL ** -0.5),
        'exp_b1': nrm((DEPTH, N_EXPERTS, 2 * D_FF), 0.01),
        'exp_w2': nrm((DEPTH, N_EXPERTS, D_FF, D_MODEL), DEEPNORM_BETA * D_FF ** -0.5),
        'exp_b2': nrm((DEPTH, N_EXPERTS, D_MODEL), 0.01),
        'ln2_g': 1.0 + nrm((DEPTH, D_MODEL), 0.02),
        'ln2_b': nrm((DEPTH, D_MODEL), 0.02),
    }


def reference(x_prompt, x_sample, cache_k, cache_v, page_table, state_gla, w_in, lam_q1, lam_k1, lam_q2, lam_k2,
              attn_norm_g, gla_w_gate, gla_b_gate, gla_norm_g, cmlp_ln_g, cmlp_ln_b, cmlp_ws, cmlp_bs, w_o,
              ln1_g, ln1_b, router_w, router_b, exp_w1, exp_b1, exp_w2, exp_b2, ln2_g, ln2_b):
    bp, s_len = x_prompt.shape[0], x_prompt.shape[1]
    ds, l_new = x_sample.shape[0], x_sample.shape[1]
    past_len = page_table.shape[1] * cache_k.shape[2]
    pos_p = jnp.arange(s_len)
    pos_s = past_len + jnp.arange(l_new)
    hp, hs = x_prompt, x_sample
    kp_rows, vp_rows, gp_states, ks_rows, vs_rows, gs_states, cs_rows = [], [], [], [], [], [], []
    for l in range(DEPTH):
        lam_init = 0.8 - 0.6 * math.exp(-0.3 * l)
        lam = diff_lambda(lam_q1[l], lam_k1[l], lam_q2[l], lam_k2[l], lam_init)
        shared = (w_in[l], lam_init, attn_norm_g[l], gla_w_gate[l], gla_b_gate[l], gla_norm_g[l],
                  cmlp_ln_g[l], cmlp_ln_b[l], cmlp_ws[l], cmlp_bs[l], w_o[l])
        s0 = jnp.zeros((bp, HB, DK_B, DV_B), jnp.float32)
        mix_p, k_p, v_p, g_p, _ = mixing_sublayer(
            hp, pos_p, s0, lambda q, k, v: diff_attn_prompt(q, k, v, lam), *shared)
        hp = layer_norm(DEEPNORM_ALPHA * hp + mix_p, ln1_g[l], ln1_b[l])
        hp = layer_norm(DEEPNORM_ALPHA * hp + moe(hp, router_w[l], router_b[l], exp_w1[l], exp_b1[l],
                                                    exp_w2[l], exp_b2[l]), ln2_g[l], ln2_b[l])
        k_past = cache_k[l, page_table].reshape(ds, past_len, HA, 2, DQK_A)
        v_past = cache_v[l, page_table].reshape(ds, past_len, HA, DV_A)
        mix_s, k_s, v_s, g_s, c_s = mixing_sublayer(
            hs, pos_s, state_gla[l], lambda q, k, v: diff_attn_sample(q, k, v, k_past, v_past, lam), *shared)
        hs = layer_norm(DEEPNORM_ALPHA * hs + mix_s, ln1_g[l], ln1_b[l])
        hs = layer_norm(DEEPNORM_ALPHA * hs + moe(hs, router_w[l], router_b[l], exp_w1[l], exp_b1[l],
                                                    exp_w2[l], exp_b2[l]), ln2_g[l], ln2_b[l])
        kp_rows.append(k_p)
        vp_rows.append(v_p)
        gp_states.append(g_p)
        ks_rows.append(k_s)
        vs_rows.append(v_s)
        gs_states.append(g_s)
        cs_rows.append(c_s)
    y_prompt, y_sample = hp, hs
    k_prompt = jnp.stack(kp_rows)
    v_prompt = jnp.stack(vp_rows)
    gla_prompt = jnp.stack(gp_states)
    k_sample = jnp.stack(ks_rows)
    v_sample = jnp.stack(vs_rows)
    gla_sample = jnp.stack(gs_states)
    cmlp_v_sample = jnp.stack(cs_rows)
    return (y_prompt, y_sample, k_prompt, v_prompt, gla_prompt, k_sample, v_sample, gla_sample, cmlp_v_sample)
```

```python
import functools
import math

import numpy as np
import jax
import jax.numpy as jnp
from jax import lax
from jax.experimental import pallas as pl
from jax.experimental.pallas import tpu as pltpu

F32, BF16 = jnp.float32, jnp.bfloat16
LANES = 128
VMEM_LIMIT = 48 * 1024 * 1024

D_MODEL = 1024
HA, DQK_A, DV_A = 4, 64, 128
ROT_DIM = DQK_A // 4
ROPE_THETA = 500000.0
HB, DK_B, DV_B = 4, 64, 64
GATE_RANK = 16
GATE_NORM = 16.0
HC, DC = 4, 64
CMLP_CHUNK = 128
N_EXPERTS = 32
TOP_K = 4
D_FF = D_MODEL
SWIGLU_LIMIT = 7.0
SWIGLU_ALPHA = 1.702
LN_EPS = 1e-5
RMS_EPS = 1e-6
NEG_INF = -1e30

W_A = HA * 2 * DQK_A
W_B = HB * DK_B
W_C = HC * DC
COL_G = 3 * W_A
COL_C = COL_G + 4 * W_B
COL_R = COL_C + 2 * W_C
COL_END = COL_R + LANES
GLA_CHUNK_PROMPT = 32
MOE_BLOCK = 256


def _cparams(sem):
    return pltpu.CompilerParams(dimension_semantics=sem, vmem_limit_bytes=VMEM_LIMIT)


def _split3(x):
    hi = x.astype(BF16)
    r = x - hi.astype(F32)
    mid = r.astype(BF16)
    lo = (r - mid.astype(F32)).astype(BF16)
    return hi, mid, lo


def _dot_sel(sel_bf16, x):
    acc = None
    for p in _split3(x):
        d = jnp.dot(sel_bf16, p, preferred_element_type=F32)
        acc = d if acc is None else acc + d
    return acc


def _seg_sum(x, bd_bf16):
    acc = None
    for p in _split3(x):
        d = jnp.dot(p, bd_bf16, preferred_element_type=F32)
        acc = d if acc is None else acc + d
    return acc


def _iota2(shape, dim):
    return lax.broadcasted_iota(jnp.int32, shape, dim)


def _idiv(x, n):
    shift = n.bit_length() - 1
    assert n == 1 << shift
    return x >> shift


def _head_blockdiag(n):
    r, c = _iota2((n, n), 0), _iota2((n, n), 1)
    return _idiv(r, DK_B) == _idiv(c, DK_B)


def _chunk_causal(n, chunk):
    r, c = _iota2((n, n), 0), _iota2((n, n), 1)
    return (_idiv(r, chunk) == _idiv(c, chunk)) & (c <= r)


def _ln_rows(x, g, b):
    mu = jnp.mean(x, axis=-1, keepdims=True)
    xc = x - mu
    var = jnp.mean(xc * xc, axis=-1, keepdims=True)
    return xc * lax.rsqrt(var + LN_EPS) * g + b


def _inproj_kernel(x_ref, w_ref, cos_ref, sa_ref, sb_ref,
                   q_ref, k_ref, v_ref, kb_ref, vb_ref, g_ref, c_ref, br_ref):
    xb = x_ref[...].astype(BF16)

    def proj(a, b):
        return jnp.dot(xb, w_ref[:, a:b], preferred_element_type=F32)

    cos, sa, sb = cos_ref[...], sa_ref[...], sb_ref[...]

    def rope(z):
        outs = []
        for i in range(z.shape[1] // LANES):
            zi = z[:, i * LANES:(i + 1) * LANES]
            outs.append(zi * cos + pltpu.roll(zi, LANES - ROT_DIM // 2, 1) * sa
                        + pltpu.roll(zi, ROT_DIM // 2, 1) * sb)
        return jnp.concatenate(outs, axis=1)

    q_ref[...] = (rope(proj(0, W_A)) * (DQK_A ** -0.5)).astype(BF16)
    k = rope(proj(W_A, 2 * W_A))
    k_ref[...] = k
    kb_ref[...] = k.astype(BF16)
    v = proj(2 * W_A, 3 * W_A)
    v_ref[...] = v
    vb_ref[...] = v.astype(BF16)
    g_ref[...] = proj(COL_G, COL_C)
    c_ref[...] = proj(COL_C, COL_R)
    br_ref[...] = proj(COL_R, COL_END)


def _inproj(x, w, tabs):
    t = x.shape[0]
    tm = min(512, t)
    cos, sa, sb = tabs
    ntab = cos.shape[0] // tm
    row = lambda n: pl.BlockSpec((tm, n), lambda i: (i, 0))
    tab = pl.BlockSpec((tm, LANES), lambda i: (i % ntab, 0))
    shapes = [(W_A, BF16), (W_A, F32), (W_A, F32), (W_A, BF16), (W_A, BF16),
              (4 * W_B, F32), (2 * W_C, F32), (LANES, F32)]
    return pl.pallas_call(
        _inproj_kernel,
        grid=(t // tm,),
        in_specs=[row(D_MODEL), pl.BlockSpec((D_MODEL, COL_END), lambda i: (0, 0)), tab, tab, tab],
        out_specs=[row(n) for n, _ in shapes],
        out_shape=[jax.ShapeDtypeStruct((t, n), d) for n, d in shapes],
        compiler_params=_cparams(("parallel",)),
        name="inproj",
    )(x, w, cos, sa, sb)


def _diff_lambda(lamv, lam_init):
    a = jnp.sum(lamv[0:1] * lamv[1:2], axis=1, keepdims=True)
    b = jnp.sum(lamv[2:3] * lamv[3:4], axis=1, keepdims=True)
    return jnp.exp(a) - jnp.exp(b) + lam_init


def _diff_finish(o1, o2, lam, g, lam_init):
    o = o1 - lam * o2
    ms = jnp.mean(o * o, axis=-1, keepdims=True)
    return o * lax.rsqrt(ms + RMS_EPS) * g * (1.0 - lam_init)


def _split_maps(q):
    lane = _iota2(q.shape, 1)
    zero = jnp.zeros_like(q)
    return jnp.concatenate([jnp.where(lane < DQK_A, q, zero), jnp.where(lane >= DQK_A, q, zero)], axis=0)


def _attn_prompt_kernel(q_ref, k_ref, v_ref, lamv_ref, g_ref, o_ref, m_sc, l_sc, acc_sc, *, blk, lam_init):
    qi = pl.program_id(2)
    qq = _split_maps(q_ref[...])
    m_sc[...] = jnp.full(m_sc.shape, NEG_INF, F32)
    l_sc[...] = jnp.zeros(l_sc.shape, F32)
    acc_sc[...] = jnp.zeros(acc_sc.shape, F32)

    def step(j, masked):
        start = pl.multiple_of(j * blk, blk)
        k = k_ref[pl.ds(start, blk), :]
        v = v_ref[pl.ds(start, blk), :]
        s = lax.dot_general(qq, k, (((1,), (1,)), ((), ())), preferred_element_type=F32)
        if masked:
            r = _iota2(s.shape, 0) & (blk - 1)
            c = _iota2(s.shape, 1)
            s = jnp.where(c <= r, s, NEG_INF)
        m_prev = m_sc[...]
        m_new = jnp.maximum(m_prev, jnp.max(s, axis=1, keepdims=True))
        alpha = jnp.exp(m_prev - m_new)
        p = jnp.exp(s - jnp.tile(m_new, (1, blk // LANES)))
        l_sc[...] = alpha * l_sc[...] + jnp.sum(p, axis=1, keepdims=True)
        acc_sc[...] = alpha * acc_sc[...] + jnp.dot(p.astype(BF16), v, preferred_element_type=F32)
        m_sc[...] = m_new

    def body(j, carry):
        step(j, False)
        return carry

    lax.fori_loop(0, qi, body, 0)
    step(qi, True)

    o = acc_sc[...] / l_sc[...]
    lam = _diff_lambda(lamv_ref[...], lam_init)
    o_ref[...] = _diff_finish(o[:blk], o[blk:], lam, g_ref[...], lam_init).astype(o_ref.dtype)


def _attn_prompt(q, kb, vb, lamv, g, nbatch, lam_init, blk=256):
    t = q.shape[0]
    s = t // nbatch
    nq = s // blk
    kern = functools.partial(_attn_prompt_kernel, blk=blk, lam_init=lam_init)
    return pl.pallas_call(
        kern,
        grid=(nbatch, HA, nq),
        in_specs=[pl.BlockSpec((blk, LANES), lambda b, h, i: (b * nq + i, h)),
                  pl.BlockSpec((s, LANES), lambda b, h, i: (b, h)),
                  pl.BlockSpec((s, LANES), lambda b, h, i: (b, h)),
                  pl.BlockSpec((4, LANES), lambda b, h, i: (0, 0)),
                  pl.BlockSpec((1, LANES), lambda b, h, i: (0, 0))],
        out_specs=pl.BlockSpec((blk, LANES), lambda b, h, i: (b * nq + i, h)),
        out_shape=jax.ShapeDtypeStruct((t, W_A), BF16),
        scratch_shapes=[pltpu.VMEM((2 * blk, LANES), F32)] * 3,
        compiler_params=_cparams(("parallel", "parallel", "arbitrary")),
        name="attn_prompt",
    )(q, kb, vb, lamv, g)


def _attn_sample_kernel(pt_ref, q_ref, kn_ref, vn_ref, lamv_ref, g_ref, *rest, n_pages, l_new, lam_init):
    del pt_ref
    kp = rest[:n_pages]
    vp = rest[n_pages:2 * n_pages]
    o_ref = rest[2 * n_pages]
    page = kp[0].shape[0]
    rpad = q_ref.shape[0]
    lam = _diff_lambda(lamv_ref[...], lam_init)
    q = q_ref[...]
    r8 = _iota2((2 * rpad, page), 0) & (rpad - 1)
    c8 = _iota2((2 * rpad, page), 1)
    new_ok = (c8 <= r8) & (c8 < l_new)
    outs = []
    for h in range(HA):
        hs = slice(h * LANES, (h + 1) * LANES)
        qq = _split_maps(q[:, hs]).astype(BF16)
        nt = (((1,), (1,)), ((), ()))
        s_past = [lax.dot_general(qq, kp[j][:, hs].astype(BF16), nt, preferred_element_type=F32)
                  for j in range(n_pages)]
        zpad = jnp.zeros((page - rpad, LANES), F32)
        k_new = jnp.concatenate([kn_ref[:, hs], zpad], axis=0).astype(BF16)
        v_new = jnp.concatenate([vn_ref[:, hs], zpad], axis=0).astype(BF16)
        s_new = jnp.where(new_ok, lax.dot_general(qq, k_new, nt, preferred_element_type=F32), NEG_INF)
        m = jnp.max(s_new, axis=1, keepdims=True)
        for sj in s_past:
            m = jnp.maximum(m, jnp.max(sj, axis=1, keepdims=True))
        p_new = jnp.exp(s_new - m)
        l = jnp.sum(p_new, axis=1, keepdims=True)
        acc = jnp.dot(p_new.astype(BF16), v_new, preferred_element_type=F32)
        for j in range(n_pages):
            pj = jnp.exp(s_past[j] - m)
            l = l + jnp.sum(pj, axis=1, keepdims=True)
            acc = acc + jnp.dot(pj.astype(BF16), vp[j][:, hs].astype(BF16), preferred_element_type=F32)
        o = acc / l
        outs.append(_diff_finish(o[:rpad], o[rpad:], lam, g_ref[...], lam_init))
    o_ref[...] = jnp.concatenate(outs, axis=1)


def _attn_sample(q, k, v, cache_k, cache_v, layer, page_table, lamv, g, l_new, lam_init):
    t = q.shape[0]
    db = t // l_new
    n_pages = page_table.shape[1]
    page = cache_k.shape[2]
    rpad = 8

    def pad_rows(a):
        a = a.reshape(db, l_new, W_A).astype(F32)
        return jnp.concatenate([a, jnp.zeros((db, rpad - l_new, W_A), F32)], axis=1)

    new_spec = pl.BlockSpec((None, rpad, W_A), lambda b, pt: (b, 0, 0))
    page_specs = [pl.BlockSpec((None, None, page, W_A), functools.partial(lambda b, pt, j: (layer, pt[b, j], 0, 0), j=j))
                  for j in range(n_pages)]
    kern = functools.partial(_attn_sample_kernel, n_pages=n_pages, l_new=l_new, lam_init=lam_init)
    out = pl.pallas_call(
        kern,
        grid_spec=pltpu.PrefetchScalarGridSpec(
            num_scalar_prefetch=1,
            grid=(db,),
            in_specs=[new_spec, new_spec, new_spec,
                      pl.BlockSpec((4, LANES), lambda b, pt: (0, 0)),
                      pl.BlockSpec((1, LANES), lambda b, pt: (0, 0))] + page_specs + page_specs,
            out_specs=new_spec,
        ),
        out_shape=jax.ShapeDtypeStruct((db, rpad, W_A), F32),
        compiler_params=_cparams(("arbitrary",)),
        name="attn_sample",
    )(page_table, pad_rows(q), pad_rows(k), pad_rows(v), lamv, g,
      *([cache_k] * n_pages), *([cache_v] * n_pages))
    return out[:, :l_new].reshape(t, W_A).astype(BF16)


def _log_sigmoid(x):
    return jnp.minimum(x, 0.0) - jnp.log1p(jnp.exp(-jnp.abs(x)))


def _gla_gate(br, wg_ref, bgate_ref):
    x = jnp.dot(br.astype(BF16), wg_ref[...], preferred_element_type=F32) + bgate_ref[...]
    return _log_sigmoid(x) / GATE_NORM


def _gla_intra(q_att, k_in, v, chunk):
    keep = _chunk_causal(q_att.shape[0], chunk)
    lane = _idiv(_iota2(q_att.shape, 1), DK_B)
    kb = k_in.astype(BF16)
    atts, vs = [], []
    for h in range(HB):
        qh = jnp.where(lane == h, q_att, 0.0).astype(BF16)
        a = lax.dot_general(qh, kb, (((1,), (1,)), ((), ())), preferred_element_type=F32)
        atts.append(jnp.where(keep, a, 0.0).astype(BF16))
        vs.append(jnp.where(lane == h, v, 0.0).astype(BF16))
    return jnp.dot(jnp.concatenate(atts, axis=1), jnp.concatenate(vs, axis=0), preferred_element_type=F32)


def _gla_finish(o, gate_in, gng_ref, bd):
    ms = _seg_sum(o * o, bd) * (1.0 / DV_B)
    o = o * lax.rsqrt(ms + RMS_EPS) * gng_ref[...]
    return o * (gate_in * (1.0 / (1.0 + jnp.exp(-gate_in))))


def _chunk_mlp(c_in, lng_ref, lnb_ref, ws_ref, bst, chunk, bd):
    n = c_in.shape[0]
    cu, cv = c_in[:, :W_C], c_in[:, W_C:]
    mu = _seg_sum(cv, bd) * (1.0 / DC)
    xc = cv - mu
    var = _seg_sum(xc * xc, bd) * (1.0 / DC)
    vn = xc * lax.rsqrt(var + LN_EPS) * lng_ref[...] + lnb_ref[...]
    keep = _chunk_causal(n, chunk)
    lane = _idiv(_iota2(vn.shape, 1), DC)
    ws, vs = [], []
    for g in range(HC):
        ws.append(jnp.where(keep, ws_ref[g], 0.0).astype(BF16))
        vs.append(jnp.where(lane == g, vn, 0.0).astype(BF16))
    mixed = jnp.dot(jnp.concatenate(ws, axis=1), jnp.concatenate(vs, axis=0), preferred_element_type=F32) + bst
    return cu * mixed, vn


def _mixer_prompt_kernel(g_ref, br_ref, c_ref, wg_ref, bgate_ref, gng_ref, lng_ref, lnb_ref, ws_ref, bst_ref,
                         o_ref, st_ref, st_sc, *, ts, chunk):
    t = pl.program_id(1)

    @pl.when(t == 0)
    def _():
        st_sc[...] = jnp.zeros(st_sc.shape, F32)

    grp = CMLP_CHUNK
    bd = _head_blockdiag(W_B)
    bd_bf = jnp.where(bd, 1.0, 0.0).astype(BF16)
    csum_sel = jnp.where(_chunk_causal(grp, chunk), 1.0, 0.0).astype(BF16)
    rows = _iota2((grp, W_B), 0)
    half = chunk // 2
    for gi in range(ts // grp):
        rs = slice(gi * grp, (gi + 1) * grp)
        g = g_ref[rs, :]
        gq = g[:, 0:W_B] * (DK_B ** -0.5)
        gk, gv, gg = g[:, W_B:2 * W_B], g[:, 2 * W_B:3 * W_B], g[:, 3 * W_B:4 * W_B]
        la = _gla_gate(br_ref[rs, :], wg_ref, bgate_ref)
        bcum = _dot_sel(csum_sel, la)
        mids, lasts = [], []
        for ci in range(grp // chunk):
            mids.append(jnp.broadcast_to(bcum[ci * chunk + half - 1:ci * chunk + half, :], (chunk, W_B)))
            lasts.append(jnp.broadcast_to(bcum[(ci + 1) * chunk - 1:(ci + 1) * chunk, :], (chunk, W_B)))
        bmid = jnp.concatenate(mids, axis=0)
        blast = jnp.concatenate(lasts, axis=0)
        q_att = gq * jnp.exp(bcum - bmid)
        k_in = gk * jnp.exp(bmid - bcum)
        k_end = gk * jnp.exp(blast - bcum)
        q_dec = (gq * jnp.exp(bcum)).astype(BF16)
        o = _gla_intra(q_att, k_in, gv, chunk)
        v_t = gv.T.astype(BF16)
        o_inter = []
        for ci in range(grp // chunk):
            cs = slice(ci * chunk, (ci + 1) * chunk)
            st = st_sc[...]
            o_inter.append(lax.dot_general(q_dec[cs], st.astype(BF16), (((1,), (1,)), ((), ())),
                                           preferred_element_type=F32))
            kem = jnp.where(_idiv(rows, chunk) == ci, k_end, 0.0).astype(BF16)
            upd = jnp.dot(v_t, kem, preferred_element_type=F32)
            dl = jnp.exp(blast[ci * chunk:ci * chunk + 1, :])
            st_sc[...] = st * dl + jnp.where(bd, upd, 0.0)
        o = o + jnp.concatenate(o_inter, axis=0)
        o_b = _gla_finish(o, gg, gng_ref, bd_bf)
        o_c, _ = _chunk_mlp(c_ref[rs, :], lng_ref, lnb_ref, ws_ref, bst_ref[...], CMLP_CHUNK, bd_bf)
        o_ref[rs, :] = jnp.concatenate([o_b, o_c], axis=1).astype(o_ref.dtype)

    @pl.when(t == pl.num_programs(1) - 1)
    def _():
        st_ref[...] = st_sc[...].T


def _mixer_prompt(g_in, br, c_in, prm, nbatch, ts=256):
    t = g_in.shape[0]
    nt = t // nbatch // ts
    row = lambda n: pl.BlockSpec((ts, n), lambda b, i: (b * nt + i, 0))
    full = lambda a: pl.BlockSpec(a.shape, lambda b, i: (0,) * a.ndim)
    kern = functools.partial(_mixer_prompt_kernel, ts=ts, chunk=GLA_CHUNK_PROMPT)
    return pl.pallas_call(
        kern,
        grid=(nbatch, nt),
        in_specs=[row(4 * W_B), row(LANES), row(2 * W_C)] + [full(a) for a in prm],
        out_specs=[row(W_B + W_C), pl.BlockSpec((None, W_B, W_B), lambda b, i: (b, 0, 0))],
        out_shape=[jax.ShapeDtypeStruct((t, W_B + W_C), BF16),
                   jax.ShapeDtypeStruct((nbatch, W_B, W_B), F32)],
        scratch_shapes=[pltpu.VMEM((W_B, W_B), F32)],
        compiler_params=_cparams(("parallel", "arbitrary")),
        name="mixer_prompt",
    )(g_in, br, c_in, *prm)


def _mixer_sample_kernel(g_ref, br_ref, c_ref, s0_ref, wg_ref, bgate_ref, gng_ref, lng_ref, lnb_ref, ws_ref,
                         bst_ref, o_ref, vn_ref, st_ref, *, l_new):
    n = g_ref.shape[0]
    bd = _head_blockdiag(W_B)
    bd_bf = jnp.where(bd, 1.0, 0.0).astype(BF16)
    r, c = _iota2((n, n), 0), _iota2((n, n), 1)
    csum_sel = jnp.where(_chunk_causal(n, l_new), 1.0, 0.0).astype(BF16)
    last_sel = jnp.where(_idiv(r, l_new) == _idiv(c, l_new), 1.0, 0.0).astype(BF16)
    g = g_ref[...]
    gq = g[:, 0:W_B] * (DK_B ** -0.5)
    gk, gv, gg = g[:, W_B:2 * W_B], g[:, 2 * W_B:3 * W_B], g[:, 3 * W_B:4 * W_B]
    la = _gla_gate(br_ref[...], wg_ref, bgate_ref)
    bcum = _dot_sel(csum_sel, la)
    blast = _dot_sel(last_sel, la)
    q_in = gq * jnp.exp(bcum)
    k_in = gk * jnp.exp(-bcum)
    k_end = gk * jnp.exp(blast - bcum)
    o = _gla_intra(q_in, k_in, gv, l_new)
    zrows = jnp.zeros((LANES - n, W_B), F32)
    ke_t = jnp.concatenate([k_end, zrows], axis=0).T
    bl_t = jnp.concatenate([blast, zrows], axis=0).T
    v_pad = jnp.concatenate([gv, zrows], axis=0).astype(BF16)
    rows = _iota2((n, W_B), 0)
    cols = _iota2((W_B, LANES), 1)
    for s in range(n // l_new):
        s0 = s0_ref[s]
        qs = jnp.where(_idiv(rows, l_new) == s, q_in, 0.0).astype(BF16)
        o = o + jnp.dot(qs, s0.astype(BF16), preferred_element_type=F32)
        kes = jnp.where(_idiv(cols, l_new) == s, ke_t, 0.0).astype(BF16)
        upd = jnp.dot(kes, v_pad, preferred_element_type=F32)
        dl = jnp.exp(bl_t[:, s * l_new:s * l_new + 1])
        st_ref[s] = s0 * dl + jnp.where(bd, upd, 0.0)
    o_b = _gla_finish(o, gg, gng_ref, bd_bf)
    o_c, vn = _chunk_mlp(c_ref[...], lng_ref, lnb_ref, ws_ref, bst_ref[...], l_new, bd_bf)
    o_ref[...] = jnp.concatenate([o_b, o_c], axis=1).astype(o_ref.dtype)
    vn_ref[...] = vn


def _mixer_sample(g_in, br, c_in, s0bd, prm, l_new, ts=64):
    t = g_in.shape[0]
    ns = ts // l_new
    row = lambda n: pl.BlockSpec((ts, n), lambda i: (i, 0))
    full = lambda a: pl.BlockSpec(a.shape, lambda i: (0,) * a.ndim)
    st = pl.BlockSpec((ns, W_B, W_B), lambda i: (i, 0, 0))
    kern = functools.partial(_mixer_sample_kernel, l_new=l_new)
    return pl.pallas_call(
        kern,
        grid=(t // ts,),
        in_specs=[row(4 * W_B), row(LANES), row(2 * W_C), st] + [full(a) for a in prm],
        out_specs=[row(W_B + W_C), row(W_C), st],
        out_shape=[jax.ShapeDtypeStruct((t, W_B + W_C), BF16),
                   jax.ShapeDtypeStruct((t, W_C), F32),
                   jax.ShapeDtypeStruct(s0bd.shape, F32)],
        compiler_params=_cparams(("parallel",)),
        name="mixer_sample",
    )(g_in, br, c_in, s0bd, *prm)


def _outproj_kernel(oa_ref, obc_ref, x_ref, wo_ref, g_ref, b_ref, rw_ref, rb_ref, h_ref, hb_ref, lg_ref, *, alpha):
    y = jnp.dot(oa_ref[...], wo_ref[0:W_A, :], preferred_element_type=F32)
    y = y + jnp.dot(obc_ref[...], wo_ref[W_A:, :], preferred_element_type=F32)
    h = _ln_rows(alpha * x_ref[...] + y, g_ref[...], b_ref[...])
    h_ref[...] = h
    hb_ref[...] = h.astype(BF16)
    lg_ref[...] = jnp.dot(h, rw_ref[...], preferred_element_type=F32, precision=lax.Precision.HIGHEST) + rb_ref[...]


def _outproj(o_a, o_bc, x, wo, g, b, rw, rb, alpha):
    t = x.shape[0]
    tm = min(512, t)
    row = lambda n: pl.BlockSpec((tm, n), lambda i: (i, 0))
    full = lambda a: pl.BlockSpec(a.shape, lambda i: (0,) * a.ndim)
    return pl.pallas_call(
        functools.partial(_outproj_kernel, alpha=alpha),
        grid=(t // tm,),
        in_specs=[row(W_A), row(W_B + W_C), row(D_MODEL)] + [full(a) for a in (wo, g, b, rw, rb)],
        out_specs=[row(D_MODEL), row(D_MODEL), row(LANES)],
        out_shape=[jax.ShapeDtypeStruct((t, D_MODEL), F32), jax.ShapeDtypeStruct((t, D_MODEL), BF16),
                   jax.ShapeDtypeStruct((t, LANES), F32)],
        compiler_params=_cparams(("parallel",)),
        name="outproj",
    )(o_a, o_bc, x, wo, g, b, rw, rb)


def _moe_kernel(be_ref, nu_ref, x_ref, w1_ref, b1_ref, w2_ref, b2_ref, y_ref):
    del be_ref

    @pl.when(pl.program_id(0) >= nu_ref[0])
    def _():
        y_ref[...] = jnp.zeros(y_ref.shape, F32)

    @pl.when(pl.program_id(0) < nu_ref[0])
    def _():
        hc = jnp.dot(x_ref[...], w1_ref[...], preferred_element_type=F32) + b1_ref[...]
        g = jnp.minimum(hc[:, :D_FF], SWIGLU_LIMIT)
        u = jnp.clip(hc[:, D_FF:], -SWIGLU_LIMIT, SWIGLU_LIMIT)
        act = (u + 1.0) * g * (1.0 / (1.0 + jnp.exp(-SWIGLU_ALPHA * g)))
        y_ref[...] = jnp.dot(act.astype(BF16), w2_ref[...], preferred_element_type=F32) + b2_ref[...]


def _moe_experts(x_pad, block_e, n_used, w1, b1, w2, b2, bm):
    nb = x_pad.shape[0] // bm
    return pl.pallas_call(
        _moe_kernel,
        grid_spec=pltpu.PrefetchScalarGridSpec(
            num_scalar_prefetch=2,
            grid=(nb,),
            in_specs=[pl.BlockSpec((bm, D_MODEL), lambda i, be, nu: (i, 0)),
                      pl.BlockSpec((None, D_MODEL, 2 * D_FF), lambda i, be, nu: (be[i], 0, 0)),
                      pl.BlockSpec((None, 1, 2 * D_FF), lambda i, be, nu: (be[i], 0, 0)),
                      pl.BlockSpec((None, D_FF, D_MODEL), lambda i, be, nu: (be[i], 0, 0)),
                      pl.BlockSpec((None, 1, D_MODEL), lambda i, be, nu: (be[i], 0, 0))],
            out_specs=pl.BlockSpec((bm, D_MODEL), lambda i, be, nu: (i, 0)),
        ),
        out_shape=jax.ShapeDtypeStruct((nb * bm, D_MODEL), F32),
        compiler_params=_cparams(("arbitrary",)),
        name="moe_experts",
    )(block_e, n_used, x_pad, w1, b1, w2, b2)


def _route(logits, bm):
    t = logits.shape[0]
    m = t * TOP_K
    top_v, top_i = lax.top_k(logits[:, :N_EXPERTS], TOP_K)
    gate = jax.nn.softmax(top_v, axis=-1)
    flat_e = top_i.reshape(m)
    onehot = (flat_e[:, None] == jnp.arange(N_EXPERTS)[None, :]).astype(jnp.int32)
    csum = jnp.cumsum(onehot, axis=0)
    rank = jnp.take_along_axis(csum, flat_e[:, None], axis=1)[:, 0] - 1
    counts = csum[-1]
    padded = ((counts + bm - 1) // bm) * bm
    pad_end = jnp.cumsum(padded)
    pad_start = pad_end - padded
    dest = pad_start[flat_e] + rank
    nb = -(-m // bm) + N_EXPERTS
    block_e = jnp.clip(jnp.searchsorted(pad_end, jnp.arange(nb) * bm, side='right'), 0, N_EXPERTS - 1)
    n_used = (pad_end[-1] // bm).astype(jnp.int32).reshape(1)
    block_e = jnp.where(jnp.arange(nb) < n_used[0], block_e, block_e[jnp.maximum(n_used[0] - 1, 0)])
    src_tok = jnp.zeros((nb * bm,), jnp.int32).at[dest].set(jnp.arange(m, dtype=jnp.int32) // TOP_K)
    return gate, dest.reshape(t, TOP_K), src_tok, block_e.astype(jnp.int32), n_used


def _ln2_kernel(h_ref, m_ref, g_ref, b_ref, o_ref, *, alpha):
    o_ref[...] = _ln_rows(alpha * h_ref[...] + m_ref[...], g_ref[...], b_ref[...])


def _ln2(h, moe_out, row0, g, b, alpha):
    t = h.shape[0]
    tm = min(512, t)
    off = row0 // tm
    return pl.pallas_call(
        functools.partial(_ln2_kernel, alpha=alpha),
        grid=(t // tm,),
        in_specs=[pl.BlockSpec((tm, D_MODEL), lambda i: (i, 0)),
                  pl.BlockSpec((tm, D_MODEL), lambda i: (i + off, 0)),
                  pl.BlockSpec((1, D_MODEL), lambda i: (0, 0)),
                  pl.BlockSpec((1, D_MODEL), lambda i: (0, 0))],
        out_specs=pl.BlockSpec((tm, D_MODEL), lambda i: (i, 0)),
        out_shape=jax.ShapeDtypeStruct((t, D_MODEL), F32),
        compiler_params=_cparams(("parallel",)),
        name="ln2",
    )(h, moe_out, g, b)


def _rope_tables(pos):
    half = ROT_DIM // 2
    inv_freq = ROPE_THETA ** (-jnp.arange(0, ROT_DIM, 2, dtype=F32) / ROT_DIM)
    ang = pos.astype(F32)[:, None] * inv_freq[None, :]
    cos, sin = jnp.cos(ang), jnp.sin(ang)
    m = np.arange(LANES) % DQK_A
    idx = m % half
    cos_l = jnp.where(m < ROT_DIM, cos[:, idx], 1.0)
    sa = jnp.where(m < half, -sin[:, idx], 0.0)
    sb = jnp.where((m >= half) & (m < ROT_DIM), sin[:, idx], 0.0)
    return cos_l, sa, sb


def _prep_w_in(w):
    r0 = COL_C
    r1 = r0 + GATE_RANK
    pad = jnp.zeros((w.shape[0], LANES - GATE_RANK), w.dtype)
    return jnp.concatenate([w[:, :r0], w[:, r1:], w[:, r0:r1], pad], axis=1).astype(BF16)


def _tile_lanes(v, reps):
    return jnp.tile(v.reshape(1, -1), (1, reps)).astype(F32)


def _blockdiag_states(s):
    n = s.shape[0]
    eye = jnp.eye(HB, dtype=s.dtype)
    return jnp.einsum('nhde,hg->nhdge', s, eye).reshape(n, HB * DK_B, HB * DV_B)


def _diag_states(sbd):
    n = sbd.shape[0]
    s = sbd.reshape(n, HB, DK_B, HB, DV_B)
    return jnp.stack([s[:, h, :, h, :] for h in range(HB)], axis=1)


def kernel(x_prompt, x_sample, cache_k, cache_v, page_table, state_gla, w_in, lam_q1, lam_k1, lam_q2, lam_k2, attn_norm_g, gla_w_gate, gla_b_gate, gla_norm_g, cmlp_ln_g, cmlp_ln_b, cmlp_ws, cmlp_bs, w_o, ln1_g, ln1_b, router_w, router_b, exp_w1, exp_b1, exp_w2, exp_b2, ln2_g, ln2_b):
    depth = w_in.shape[0]
    bp, s_len, _ = x_prompt.shape
    db, l_new, _ = x_sample.shape
    n_phys, page = cache_k.shape[1], cache_k.shape[2]
    past_len = page_table.shape[1] * page
    alpha = (2 * depth) ** 0.25
    tp, ts = bp * s_len, db * l_new
    bm = MOE_BLOCK

    tabs_p = _rope_tables(jnp.arange(s_len))
    tabs_s = _rope_tables(past_len + (jnp.arange(ts) % l_new))
    ck = cache_k.reshape(depth, n_phys, page, W_A)
    cv = cache_v.reshape(depth, n_phys, page, W_A)
    page_table = page_table.astype(jnp.int32)

    hp = x_prompt.reshape(tp, D_MODEL)
    hs = x_sample.reshape(ts, D_MODEL)
    outs = {k: [] for k in ("kp", "vp", "gp", "ks", "vs", "gs", "cs")}
    for l in range(depth):
        lam_init = 0.8 - 0.6 * math.exp(-0.3 * l)
        w = _prep_w_in(w_in[l])
        lamv = jnp.pad(jnp.stack([lam_q1[l], lam_k1[l], lam_q2[l], lam_k2[l]]).astype(F32),
                       ((0, 0), (0, LANES - DQK_A)))
        g_attn = attn_norm_g[l].reshape(1, DV_A).astype(F32)
        wg = jnp.pad(gla_w_gate[l], ((0, LANES - GATE_RANK), (0, 0))).astype(BF16)
        wo = w_o[l].astype(BF16)
        rw = jnp.pad(router_w[l].astype(F32), ((0, 0), (0, LANES - N_EXPERTS)))
        rb = jnp.pad(router_b[l].astype(F32), (0, LANES - N_EXPERTS), constant_values=NEG_INF).reshape(1, LANES)
        ln1 = (ln1_g[l].reshape(1, D_MODEL), ln1_b[l].reshape(1, D_MODEL))

        def mixer_params(lc, n_rows):
            reps = n_rows // lc
            ws = jnp.tile(cmlp_ws[l][:, :lc, :lc], (1, reps, reps))
            bst = jnp.tile(jnp.repeat(cmlp_bs[l][:, :lc].T, DC, axis=1), (reps, 1))
            return (wg, gla_b_gate[l].reshape(1, W_B), _tile_lanes(gla_norm_g[l], HB),
                    _tile_lanes(cmlp_ln_g[l], HC), _tile_lanes(cmlp_ln_b[l], HC), ws, bst)

        q, k, v, kb, vb, g_in, c_in, br = _inproj(hp, w, tabs_p)
        o_a = _attn_prompt(q, kb, vb, lamv, g_attn, bp, lam_init)
        o_bc, st_p = _mixer_prompt(g_in, br, c_in, mixer_params(CMLP_CHUNK, CMLP_CHUNK), bp)
        hp1, hp1b, lg_p = _outproj(o_a, o_bc, hp, wo, *ln1, rw, rb, alpha)
        outs["kp"].append(k.reshape(bp, s_len, HA, 2 * DQK_A))
        outs["vp"].append(v.reshape(bp, s_len, HA, DV_A))
        outs["gp"].append(_diag_states(st_p))

        q, k, v, kb, vb, g_in, c_in, br = _inproj(hs, w, tabs_s)
        o_a = _attn_sample(q, k, v, ck, cv, l, page_table, lamv, g_attn, l_new, lam_init)
        rows_s = min(64, ts)
        lc = min(l_new, CMLP_CHUNK)
        o_bc, vn, st_s = _mixer_sample(g_in, br, c_in, _blockdiag_states(state_gla[l].astype(F32)),
                                       mixer_params(lc, rows_s), l_new, rows_s)
        hs1, hs1b, lg_s = _outproj(o_a, o_bc, hs, wo, *ln1, rw, rb, alpha)
        outs["ks"].append(k.reshape(db, l_new, HA, 2 * DQK_A))
        outs["vs"].append(v.reshape(db, l_new, HA, DV_A))
        outs["gs"].append(_diag_states(st_s))
        outs["cs"].append(vn.reshape(db, l_new, W_C))

        gate, dest, src_tok, block_e, n_used = _route(jnp.concatenate([lg_p, lg_s], axis=0), bm)
        x_pad = jnp.concatenate([hp1b, hs1b], axis=0)[src_tok]
        y_pad = _moe_experts(x_pad, block_e, n_used, exp_w1[l].astype(BF16), exp_b1[l].reshape(N_EXPERTS, 1, -1),
                             exp_w2[l].astype(BF16), exp_b2[l].reshape(N_EXPERTS, 1, -1), bm)
        moe_out = jnp.sum(y_pad[dest] * gate[:, :, None], axis=1)
        hp = _ln2(hp1, moe_out, 0, ln2_g[l].reshape(1, -1), ln2_b[l].reshape(1, -1), alpha)
        hs = _ln2(hs1, moe_out, tp, ln2_g[l].reshape(1, -1), ln2_b[l].reshape(1, -1), alpha)

    return (hp.reshape(bp, s_len, D_MODEL), hs.reshape(db, l_new, D_MODEL),
            jnp.stack(outs["kp"]), jnp.stack(outs["vp"]), jnp.stack(outs["gp"]),
            jnp.stack(outs["ks"]), jnp.stack(outs["vs"]), jnp.stack(outs["gs"]), jnp.stack(outs["cs"]))
```

```python
import functools
import math

import numpy as np
import jax
import jax.numpy as jnp
from jax import lax
from jax.experimental import pallas as pl
from jax.experimental.pallas import tpu as pltpu

F32, BF16 = jnp.float32, jnp.bfloat16
LANES = 128
VMEM_LIMIT = 48 * 1024 * 1024

D_MODEL = 1024
HA, DQK_A, DV_A = 4, 64, 128
ROT_DIM = DQK_A // 4
ROPE_THETA = 500000.0
HB, DK_B, DV_B = 4, 64, 64
GATE_RANK = 16
GATE_NORM = 16.0
HC, DC = 4, 64
CMLP_CHUNK = 128
N_EXPERTS = 32
TOP_K = 4
D_FF = D_MODEL
SWIGLU_LIMIT = 7.0
SWIGLU_ALPHA = 1.702
LN_EPS = 1e-5
RMS_EPS = 1e-6
NEG_INF = -1e30
LOG2E = math.log2(math.e)

W_A = HA * 2 * DQK_A
W_B = HB * DK_B
W_C = HC * DC
COL_G = 3 * W_A
COL_C = COL_G + 4 * W_B
COL_R = COL_C + 2 * W_C
COL_END = COL_R + LANES
GLA_CHUNK_PROMPT = 32
MOE_BLOCK = 256


def _cparams(sem):
    return pltpu.CompilerParams(dimension_semantics=sem, vmem_limit_bytes=VMEM_LIMIT)


def _split3(x):
    hi = x.astype(BF16)
    r = x - hi.astype(F32)
    mid = r.astype(BF16)
    lo = (r - mid.astype(F32)).astype(BF16)
    return hi, mid, lo


def _dot_sel(sel_bf16, x):
    acc = None
    for p in _split3(x):
        d = jnp.dot(sel_bf16, p, preferred_element_type=F32)
        acc = d if acc is None else acc + d
    return acc


def _seg_sum(x, bd_bf16):
    acc = None
    for p in _split3(x):
        d = jnp.dot(p, bd_bf16, preferred_element_type=F32)
        acc = d if acc is None else acc + d
    return acc


def _iota2(shape, dim):
    return lax.broadcasted_iota(jnp.int32, shape, dim)


def _idiv(x, n):
    shift = n.bit_length() - 1
    assert n == 1 << shift
    return x >> shift


def _head_blockdiag(n):
    r, c = _iota2((n, n), 0), _iota2((n, n), 1)
    return _idiv(r, DK_B) == _idiv(c, DK_B)


def _chunk_causal(n, chunk):
    r, c = _iota2((n, n), 0), _iota2((n, n), 1)
    return (_idiv(r, chunk) == _idiv(c, chunk)) & (c <= r)


def _ln_rows(x, g, b):
    mu = jnp.mean(x, axis=-1, keepdims=True)
    xc = x - mu
    var = jnp.mean(xc * xc, axis=-1, keepdims=True)
    return xc * lax.rsqrt(var + LN_EPS) * g + b


def _inproj_kernel(x_ref, w_ref, cos_ref, sa_ref, sb_ref,
                   q_ref, k_ref, v_ref, kb_ref, vb_ref, g_ref, c_ref, br_ref):
    xb = x_ref[...].astype(BF16)

    def proj(a, b):
        return jnp.dot(xb, w_ref[:, a:b], preferred_element_type=F32)

    cos, sa, sb = cos_ref[...], sa_ref[...], sb_ref[...]

    def rope(z):
        outs = []
        for i in range(z.shape[1] // LANES):
            zi = z[:, i * LANES:(i + 1) * LANES]
            outs.append(zi * cos + pltpu.roll(zi, LANES - ROT_DIM // 2, 1) * sa
                        + pltpu.roll(zi, ROT_DIM // 2, 1) * sb)
        return jnp.concatenate(outs, axis=1)

    q_ref[...] = (rope(proj(0, W_A)) * (DQK_A ** -0.5 * LOG2E)).astype(BF16)
    k = rope(proj(W_A, 2 * W_A))
    kb_ref[...] = k.astype(BF16)
    v = proj(2 * W_A, 3 * W_A)
    vb_ref[...] = v.astype(BF16)
    for h in range(HA):
        k_ref[:, h, :] = k[:, h * LANES:(h + 1) * LANES]
        v_ref[:, h, :] = v[:, h * LANES:(h + 1) * LANES]
    g_ref[...] = proj(COL_G, COL_C)
    c_ref[...] = proj(COL_C, COL_R)
    br_ref[...] = proj(COL_R, COL_END)


def _inproj(x, w, tabs):
    t = x.shape[0]
    tm = min(512, t)
    cos, sa, sb = tabs
    ntab = cos.shape[0] // tm
    row = lambda n: pl.BlockSpec((tm, n), lambda i: (i, 0))
    tab = pl.BlockSpec((tm, LANES), lambda i: (i % ntab, 0))
    heads = pl.BlockSpec((tm, HA, DV_A), lambda i: (i, 0, 0))
    shapes = [((W_A,), BF16), ((HA, DV_A), F32), ((HA, DV_A), F32), ((W_A,), BF16), ((W_A,), BF16),
              ((4 * W_B,), F32), ((2 * W_C,), F32), ((LANES,), F32)]
    return pl.pallas_call(
        _inproj_kernel,
        grid=(t // tm,),
        in_specs=[row(D_MODEL), pl.BlockSpec((D_MODEL, COL_END), lambda i: (0, 0)), tab, tab, tab],
        out_specs=[heads if len(n) == 2 else row(n[0]) for n, _ in shapes],
        out_shape=[jax.ShapeDtypeStruct((t,) + n, d) for n, d in shapes],
        compiler_params=_cparams(("parallel",)),
        name="inproj",
    )(x, w, cos, sa, sb)


def _diff_lambda(lamv, lam_init):
    a = jnp.sum(lamv[0:1] * lamv[1:2], axis=1, keepdims=True)
    b = jnp.sum(lamv[2:3] * lamv[3:4], axis=1, keepdims=True)
    return jnp.exp(a) - jnp.exp(b) + lam_init


def _diff_finish(o1, o2, lam, g, lam_init):
    o = o1 - lam * o2
    ms = jnp.mean(o * o, axis=-1, keepdims=True)
    return o * lax.rsqrt(ms + RMS_EPS) * g * (1.0 - lam_init)


def _split_maps(q):
    lane = _iota2(q.shape, 1)
    zero = jnp.zeros_like(q)
    return jnp.concatenate([jnp.where(lane < DQK_A, q, zero), jnp.where(lane >= DQK_A, q, zero)], axis=0)


def _attn_prompt_kernel(q_ref, k_ref, v_ref, lamv_ref, g_ref, o_ref, m_sc, l_sc, acc_sc, *, blk, lam_init):
    qi = pl.program_id(2)
    qq = _split_maps(q_ref[...])
    m_sc[...] = jnp.full(m_sc.shape, NEG_INF, F32)
    l_sc[...] = jnp.zeros(l_sc.shape, F32)
    acc_sc[...] = jnp.zeros(acc_sc.shape, F32)

    def step(j, masked):
        start = pl.multiple_of(j * blk, blk)
        k = k_ref[pl.ds(start, blk), :]
        v = v_ref[pl.ds(start, blk), :]
        s = lax.dot_general(qq, k, (((1,), (1,)), ((), ())), preferred_element_type=F32)
        if masked:
            r = _iota2(s.shape, 0) & (blk - 1)
            c = _iota2(s.shape, 1)
            s = jnp.where(c <= r, s, NEG_INF)
        m_prev = m_sc[...]
        m_new = jnp.maximum(m_prev, jnp.max(s, axis=1, keepdims=True))
        alpha = jnp.exp2(m_prev - m_new)
        p = jnp.exp2(s - jnp.tile(m_new, (1, blk // LANES)))
        l_sc[...] = alpha * l_sc[...] + jnp.sum(p, axis=1, keepdims=True)
        acc_sc[...] = alpha * acc_sc[...] + jnp.dot(p.astype(BF16), v, preferred_element_type=F32)
        m_sc[...] = m_new

    def body(j, carry):
        step(j, False)
        return carry

    lax.fori_loop(0, qi, body, 0)
    step(qi, True)

    o = acc_sc[...] / l_sc[...]
    lam = _diff_lambda(lamv_ref[...], lam_init)
    o_ref[...] = _diff_finish(o[:blk], o[blk:], lam, g_ref[...], lam_init).astype(o_ref.dtype)


def _attn_prompt(q, kb, vb, lamv, g, nbatch, lam_init):
    t = q.shape[0]
    s = t // nbatch
    blk = min(512, s)
    nq = s // blk
    kern = functools.partial(_attn_prompt_kernel, blk=blk, lam_init=lam_init)
    return pl.pallas_call(
        kern,
        grid=(nbatch, HA, nq),
        in_specs=[pl.BlockSpec((blk, LANES), lambda b, h, i: (b * nq + i, h)),
                  pl.BlockSpec((s, LANES), lambda b, h, i: (b, h)),
                  pl.BlockSpec((s, LANES), lambda b, h, i: (b, h)),
                  pl.BlockSpec((4, LANES), lambda b, h, i: (0, 0)),
                  pl.BlockSpec((1, LANES), lambda b, h, i: (0, 0))],
        out_specs=pl.BlockSpec((blk, LANES), lambda b, h, i: (b * nq + i, h)),
        out_shape=jax.ShapeDtypeStruct((t, W_A), BF16),
        scratch_shapes=[pltpu.VMEM((2 * blk, LANES), F32)] * 3,
        compiler_params=_cparams(("parallel", "parallel", "arbitrary")),
        name="attn_prompt",
    )(q, kb, vb, lamv, g)


def _attn_sample_kernel(pt_ref, q_ref, kn_ref, vn_ref, lamv_ref, g_ref, *rest, n_pages, l_new, lam_init):
    del pt_ref
    kp = rest[:n_pages]
    vp = rest[n_pages:2 * n_pages]
    o_ref = rest[2 * n_pages]
    rows_pg = kp[0].shape[0]
    rpad = q_ref.shape[0]
    nq = 2 * rpad
    lam = _diff_lambda(lamv_ref[...], lam_init)
    q = q_ref[...]
    qq = jnp.concatenate([_split_maps(q[:, h * LANES:(h + 1) * LANES]) for h in range(HA)],
                         axis=0).astype(BF16)
    r = _iota2((HA * nq, rows_pg), 0)
    c = _iota2((HA * nq, rows_pg), 1)
    head_ok = (c & (HA - 1)) == _idiv(r, nq)
    new_ok = head_ok & (_idiv(c, HA) <= (r & (rpad - 1))) & (c < l_new * HA)
    nt = (((1,), (1,)), ((), ()))
    zpad = jnp.zeros((rows_pg - kn_ref.shape[0], LANES), F32)
    k_new = jnp.concatenate([kn_ref[...], zpad], axis=0).astype(BF16)
    v_new = jnp.concatenate([vn_ref[...], zpad], axis=0).astype(BF16)
    s_new = jnp.where(new_ok, lax.dot_general(qq, k_new, nt, preferred_element_type=F32), NEG_INF)
    s_past = [jnp.where(head_ok, lax.dot_general(qq, kp[j][...].astype(BF16), nt, preferred_element_type=F32),
                        NEG_INF) for j in range(n_pages)]
    m = jnp.max(s_new, axis=1, keepdims=True)
    for sj in s_past:
        m = jnp.maximum(m, jnp.max(sj, axis=1, keepdims=True))
    p_new = jnp.exp2(s_new - m)
    l = jnp.sum(p_new, axis=1, keepdims=True)
    acc = jnp.dot(p_new.astype(BF16), v_new, preferred_element_type=F32)
    for j in range(n_pages):
        pj = jnp.exp2(s_past[j] - m)
        l = l + jnp.sum(pj, axis=1, keepdims=True)
        acc = acc + jnp.dot(pj.astype(BF16), vp[j][...].astype(BF16), preferred_element_type=F32)
    o = acc / l
    outs = [_diff_finish(o[h * nq:h * nq + rpad], o[h * nq + rpad:(h + 1) * nq], lam, g_ref[...], lam_init)
            for h in range(HA)]
    o_ref[...] = jnp.concatenate(outs, axis=1)


def _attn_sample(q, k, v, cache_k, cache_v, layer, page_table, lamv, g, l_new, lam_init):
    t = q.shape[0]
    db = t // l_new
    n_pages = page_table.shape[1]
    page = cache_k.shape[2]
    rpad = 8

    def pad_rows(a):
        a = a.reshape(db, l_new, W_A).astype(F32)
        return jnp.concatenate([a, jnp.zeros((db, rpad - l_new, W_A), F32)], axis=1)

    def new_rows(a):
        return a.reshape(db, l_new * HA, DV_A).astype(F32)

    depth, n_phys = cache_k.shape[:2]
    cache_k = cache_k.reshape(depth, n_phys, page * HA, DV_A)
    cache_v = cache_v.reshape(depth, n_phys, page * HA, DV_A)
    q_spec = pl.BlockSpec((None, rpad, W_A), lambda b, pt: (b, 0, 0))
    new_spec = pl.BlockSpec((None, l_new * HA, DV_A), lambda b, pt: (b, 0, 0))
    page_specs = [pl.BlockSpec((None, None, page * HA, DV_A),
                               functools.partial(lambda b, pt, j: (layer, pt[b, j], 0, 0), j=j))
                  for j in range(n_pages)]
    kern = functools.partial(_attn_sample_kernel, n_pages=n_pages, l_new=l_new, lam_init=lam_init)
    out = pl.pallas_call(
        kern,
        grid_spec=pltpu.PrefetchScalarGridSpec(
            num_scalar_prefetch=1,
            grid=(db,),
            in_specs=[q_spec, new_spec, new_spec,
                      pl.BlockSpec((4, LANES), lambda b, pt: (0, 0)),
                      pl.BlockSpec((1, LANES), lambda b, pt: (0, 0))] + page_specs + page_specs,
            out_specs=q_spec,
        ),
        out_shape=jax.ShapeDtypeStruct((db, rpad, W_A), F32),
        compiler_params=_cparams(("arbitrary",)),
        name="attn_sample",
    )(page_table, pad_rows(q), new_rows(k), new_rows(v), lamv, g,
      *([cache_k] * n_pages), *([cache_v] * n_pages))
    return out[:, :l_new].reshape(t, W_A).astype(BF16)


def _log_sigmoid(x):
    return jnp.minimum(x, 0.0) - jnp.log1p(jnp.exp(-jnp.abs(x)))


def _gla_gate(br, wg_ref, bgate_ref):
    x = jnp.dot(br.astype(BF16), wg_ref[...], preferred_element_type=F32) + bgate_ref[...]
    return _log_sigmoid(x) / GATE_NORM


def _gla_intra(q_att, k_in, v, chunk):
    keep = _chunk_causal(q_att.shape[0], chunk)
    lane = _idiv(_iota2(q_att.shape, 1), DK_B)
    kb = k_in.astype(BF16)
    atts, vs = [], []
    for h in range(HB):
        qh = jnp.where(lane == h, q_att, 0.0).astype(BF16)
        a = lax.dot_general(qh, kb, (((1,), (1,)), ((), ())), preferred_element_type=F32)
        atts.append(jnp.where(keep, a, 0.0).astype(BF16))
        vs.append(jnp.where(lane == h, v, 0.0).astype(BF16))
    return jnp.dot(jnp.concatenate(atts, axis=1), jnp.concatenate(vs, axis=0), preferred_element_type=F32)


def _gla_finish(o, gate_in, gng_ref, bd):
    ms = _seg_sum(o * o, bd) * (1.0 / DV_B)
    o = o * lax.rsqrt(ms + RMS_EPS) * gng_ref[...]
    return o * (gate_in * (1.0 / (1.0 + jnp.exp(-gate_in))))


def _chunk_mlp(c_in, lng_ref, lnb_ref, ws_ref, bst, chunk, bd):
    n = c_in.shape[0]
    cu, cv = c_in[:, :W_C], c_in[:, W_C:]
    mu = _seg_sum(cv, bd) * (1.0 / DC)
    xc = cv - mu
    var = _seg_sum(xc * xc, bd) * (1.0 / DC)
    vn = xc * lax.rsqrt(var + LN_EPS) * lng_ref[...] + lnb_ref[...]
    keep = _chunk_causal(n, chunk)
    lane = _idiv(_iota2(vn.shape, 1), DC)
    ws, vs = [], []
    for g in range(HC):
        ws.append(jnp.where(keep, ws_ref[g], 0.0).astype(BF16))
        vs.append(jnp.where(lane == g, vn, 0.0).astype(BF16))
    mixed = jnp.dot(jnp.concatenate(ws, axis=1), jnp.concatenate(vs, axis=0), preferred_element_type=F32) + bst
    return cu * mixed, vn


def _mixer_prompt_kernel(g_ref, br_ref, c_ref, wg_ref, bgate_ref, gng_ref, lng_ref, lnb_ref, ws_ref, bst_ref,
                         o_ref, st_ref, st_sc, *, ts, chunk):
    t = pl.program_id(1)

    @pl.when(t == 0)
    def _():
        st_sc[...] = jnp.zeros(st_sc.shape, F32)

    grp = CMLP_CHUNK
    bd = _head_blockdiag(W_B)
    bd_bf = jnp.where(bd, 1.0, 0.0).astype(BF16)
    csum_sel = jnp.where(_chunk_causal(grp, chunk), 1.0, 0.0).astype(BF16)
    rows = _iota2((grp, W_B), 0)
    half = chunk // 2
    for gi in range(ts // grp):
        rs = slice(gi * grp, (gi + 1) * grp)
        g = g_ref[rs, :]
        gq = g[:, 0:W_B] * (DK_B ** -0.5)
        gk, gv, gg = g[:, W_B:2 * W_B], g[:, 2 * W_B:3 * W_B], g[:, 3 * W_B:4 * W_B]
        la = _gla_gate(br_ref[rs, :], wg_ref, bgate_ref)
        bcum = _dot_sel(csum_sel, la)
        mids, lasts = [], []
        for ci in range(grp // chunk):
            mids.append(jnp.broadcast_to(bcum[ci * chunk + half - 1:ci * chunk + half, :], (chunk, W_B)))
            lasts.append(jnp.broadcast_to(bcum[(ci + 1) * chunk - 1:(ci + 1) * chunk, :], (chunk, W_B)))
        bmid = jnp.concatenate(mids, axis=0)
        blast = jnp.concatenate(lasts, axis=0)
        q_att = gq * jnp.exp(bcum - bmid)
        k_in = gk * jnp.exp(bmid - bcum)
        k_end = gk * jnp.exp(blast - bcum)
        q_dec = (gq * jnp.exp(bcum)).astype(BF16)
        o = _gla_intra(q_att, k_in, gv, chunk)
        v_t = gv.T.astype(BF16)
        o_inter = []
        for ci in range(grp // chunk):
            cs = slice(ci * chunk, (ci + 1) * chunk)
            st = st_sc[...]
            o_inter.append(lax.dot_general(q_dec[cs], st.astype(BF16), (((1,), (1,)), ((), ())),
                                           preferred_element_type=F32))
            kem = jnp.where(_idiv(rows, chunk) == ci, k_end, 0.0).astype(BF16)
            upd = jnp.dot(v_t, kem, preferred_element_type=F32)
            dl = jnp.exp(blast[ci * chunk:ci * chunk + 1, :])
            st_sc[...] = st * dl + jnp.where(bd, upd, 0.0)
        o = o + jnp.concatenate(o_inter, axis=0)
        o_b = _gla_finish(o, gg, gng_ref, bd_bf)
        o_c, _ = _chunk_mlp(c_ref[rs, :], lng_ref, lnb_ref, ws_ref, bst_ref[...], CMLP_CHUNK, bd_bf)
        o_ref[rs, :] = jnp.concatenate([o_b, o_c], axis=1).astype(o_ref.dtype)

    @pl.when(t == pl.num_programs(1) - 1)
    def _():
        st_ref[...] = st_sc[...].T


def _mixer_prompt(g_in, br, c_in, prm, nbatch, ts=256):
    t = g_in.shape[0]
    nt = t // nbatch // ts
    row = lambda n: pl.BlockSpec((ts, n), lambda b, i: (b * nt + i, 0))
    full = lambda a: pl.BlockSpec(a.shape, lambda b, i: (0,) * a.ndim)
    kern = functools.partial(_mixer_prompt_kernel, ts=ts, chunk=GLA_CHUNK_PROMPT)
    return pl.pallas_call(
        kern,
        grid=(nbatch, nt),
        in_specs=[row(4 * W_B), row(LANES), row(2 * W_C)] + [full(a) for a in prm],
        out_specs=[row(W_B + W_C), pl.BlockSpec((None, W_B, W_B), lambda b, i: (b, 0, 0))],
        out_shape=[jax.ShapeDtypeStruct((t, W_B + W_C), BF16),
                   jax.ShapeDtypeStruct((nbatch, W_B, W_B), F32)],
        scratch_shapes=[pltpu.VMEM((W_B, W_B), F32)],
        compiler_params=_cparams(("parallel", "arbitrary")),
        name="mixer_prompt",
    )(g_in, br, c_in, *prm)


def _mixer_sample_kernel(g_ref, br_ref, c_ref, s0_ref, wg_ref, bgate_ref, gng_ref, lng_ref, lnb_ref, ws_ref,
                         bst_ref, o_ref, vn_ref, st_ref, *, l_new):
    n = g_ref.shape[0]
    bd = _head_blockdiag(W_B)
    bd_bf = jnp.where(bd, 1.0, 0.0).astype(BF16)
    r, c = _iota2((n, n), 0), _iota2((n, n), 1)
    csum_sel = jnp.where(_chunk_causal(n, l_new), 1.0, 0.0).astype(BF16)
    last_sel = jnp.where(_idiv(r, l_new) == _idiv(c, l_new), 1.0, 0.0).astype(BF16)
    g = g_ref[...]
    gq = g[:, 0:W_B] * (DK_B ** -0.5)
    gk, gv, gg = g[:, W_B:2 * W_B], g[:, 2 * W_B:3 * W_B], g[:, 3 * W_B:4 * W_B]
    la = _gla_gate(br_ref[...], wg_ref, bgate_ref)
    bcum = _dot_sel(csum_sel, la)
    blast = _dot_sel(last_sel, la)
    q_in = gq * jnp.exp(bcum)
    k_in = gk * jnp.exp(-bcum)
    k_end = gk * jnp.exp(blast - bcum)
    o = _gla_intra(q_in, k_in, gv, l_new)
    zrows = jnp.zeros((LANES - n, W_B), F32)
    ke_t = jnp.concatenate([k_end, zrows], axis=0).T
    bl_t = jnp.concatenate([blast, zrows], axis=0).T
    v_pad = jnp.concatenate([gv, zrows], axis=0).astype(BF16)
    rows = _iota2((n, W_B), 0)
    cols = _iota2((W_B, LANES), 1)
    for s in range(n // l_new):
        s0 = s0_ref[s]
        qs = jnp.where(_idiv(rows, l_new) == s, q_in, 0.0).astype(BF16)
        o = o + jnp.dot(qs, s0.astype(BF16), preferred_element_type=F32)
        kes = jnp.where(_idiv(cols, l_new) == s, ke_t, 0.0).astype(BF16)
        upd = jnp.dot(kes, v_pad, preferred_element_type=F32)
        dl = jnp.exp(bl_t[:, s * l_new:s * l_new + 1])
        st_ref[s] = s0 * dl + jnp.where(bd, upd, 0.0)
    o_b = _gla_finish(o, gg, gng_ref, bd_bf)
    o_c, vn = _chunk_mlp(c_ref[...], lng_ref, lnb_ref, ws_ref, bst_ref[...], l_new, bd_bf)
    o_ref[...] = jnp.concatenate([o_b, o_c], axis=1).astype(o_ref.dtype)
    vn_ref[...] = vn


def _mixer_sample(g_in, br, c_in, s0bd, prm, l_new, ts=64):
    t = g_in.shape[0]
    ns = ts // l_new
    row = lambda n: pl.BlockSpec((ts, n), lambda i: (i, 0))
    full = lambda a: pl.BlockSpec(a.shape, lambda i: (0,) * a.ndim)
    st = pl.BlockSpec((ns, W_B, W_B), lambda i: (i, 0, 0))
    kern = functools.partial(_mixer_sample_kernel, l_new=l_new)
    return pl.pallas_call(
        kern,
        grid=(t // ts,),
        in_specs=[row(4 * W_B), row(LANES), row(2 * W_C), st] + [full(a) for a in prm],
        out_specs=[row(W_B + W_C), row(W_C), st],
        out_shape=[jax.ShapeDtypeStruct((t, W_B + W_C), BF16),
                   jax.ShapeDtypeStruct((t, W_C), F32),
                   jax.ShapeDtypeStruct(s0bd.shape, F32)],
        compiler_params=_cparams(("parallel",)),
        name="mixer_sample",
    )(g_in, br, c_in, s0bd, *prm)


def _outproj_kernel(oa_ref, obc_ref, x_ref, wo_ref, g_ref, b_ref, rw_ref, rb_ref, h_ref, hb_ref, lg_ref, *, alpha):
    y = jnp.dot(oa_ref[...], wo_ref[0:W_A, :], preferred_element_type=F32)
    y = y + jnp.dot(obc_ref[...], wo_ref[W_A:, :], preferred_element_type=F32)
    h = _ln_rows(alpha * x_ref[...] + y, g_ref[...], b_ref[...])
    h_ref[...] = h
    hb_ref[...] = h.astype(BF16)
    lg_ref[...] = jnp.dot(h, rw_ref[...], preferred_element_type=F32, precision=lax.Precision.HIGHEST) + rb_ref[...]


def _outproj(o_a, o_bc, x, wo, g, b, rw, rb, alpha):
    t = x.shape[0]
    tm = min(512, t)
    row = lambda n: pl.BlockSpec((tm, n), lambda i: (i, 0))
    full = lambda a: pl.BlockSpec(a.shape, lambda i: (0,) * a.ndim)
    return pl.pallas_call(
        functools.partial(_outproj_kernel, alpha=alpha),
        grid=(t // tm,),
        in_specs=[row(W_A), row(W_B + W_C), row(D_MODEL)] + [full(a) for a in (wo, g, b, rw, rb)],
        out_specs=[row(D_MODEL), row(D_MODEL), row(LANES)],
        out_shape=[jax.ShapeDtypeStruct((t, D_MODEL), F32), jax.ShapeDtypeStruct((t, D_MODEL), BF16),
                   jax.ShapeDtypeStruct((t, LANES), F32)],
        compiler_params=_cparams(("parallel",)),
        name="outproj",
    )(o_a, o_bc, x, wo, g, b, rw, rb)


def _moe_kernel(be_ref, nu_ref, x_ref, w1_ref, b1_ref, w2_ref, b2_ref, y_ref, w1b_sc, w2b_sc):
    i = pl.program_id(0)

    @pl.when((i == 0) | (be_ref[i] != be_ref[jnp.maximum(i - 1, 0)]))
    def _():
        w1b_sc[...] = w1_ref[...].astype(BF16)
        w2b_sc[...] = w2_ref[...].astype(BF16)

    @pl.when(i >= nu_ref[0])
    def _():
        y_ref[...] = jnp.zeros(y_ref.shape, F32)

    @pl.when(i < nu_ref[0])
    def _():
        hc = jnp.dot(x_ref[...], w1b_sc[...], preferred_element_type=F32) + b1_ref[...]
        g = jnp.minimum(hc[:, :D_FF], SWIGLU_LIMIT)
        u = jnp.clip(hc[:, D_FF:], -SWIGLU_LIMIT, SWIGLU_LIMIT)
        act = (u + 1.0) * g * (1.0 / (1.0 + jnp.exp(-SWIGLU_ALPHA * g)))
        y_ref[...] = jnp.dot(act.astype(BF16), w2b_sc[...], preferred_element_type=F32) + b2_ref[...]


def _moe_experts(x_pad, block_e, n_used, w1, b1, w2, b2, bm):
    nb = x_pad.shape[0] // bm
    return pl.pallas_call(
        _moe_kernel,
        grid_spec=pltpu.PrefetchScalarGridSpec(
            num_scalar_prefetch=2,
            grid=(nb,),
            in_specs=[pl.BlockSpec((bm, D_MODEL), lambda i, be, nu: (i, 0)),
                      pl.BlockSpec((None, D_MODEL, 2 * D_FF), lambda i, be, nu: (be[i], 0, 0)),
                      pl.BlockSpec((None, 1, 2 * D_FF), lambda i, be, nu: (be[i], 0, 0)),
                      pl.BlockSpec((None, D_FF, D_MODEL), lambda i, be, nu: (be[i], 0, 0)),
                      pl.BlockSpec((None, 1, D_MODEL), lambda i, be, nu: (be[i], 0, 0))],
            out_specs=pl.BlockSpec((bm, D_MODEL), lambda i, be, nu: (i, 0)),
            scratch_shapes=[pltpu.VMEM((D_MODEL, 2 * D_FF), BF16), pltpu.VMEM((D_FF, D_MODEL), BF16)],
        ),
        out_shape=jax.ShapeDtypeStruct((nb * bm, D_MODEL), F32),
        compiler_params=_cparams(("arbitrary",)),
        name="moe_experts",
    )(block_e, n_used, x_pad, w1, b1, w2, b2)


def _route(logits, bm):
    t = logits.shape[0]
    m = t * TOP_K
    top_v, top_i = lax.top_k(logits[:, :N_EXPERTS], TOP_K)
    gate = jax.nn.softmax(top_v, axis=-1)
    flat_e = top_i.reshape(m)
    onehot = (flat_e[:, None] == jnp.arange(N_EXPERTS)[None, :]).astype(jnp.int32)
    csum = jnp.cumsum(onehot, axis=0)
    rank = jnp.take_along_axis(csum, flat_e[:, None], axis=1)[:, 0] - 1
    counts = csum[-1]
    padded = ((counts + bm - 1) // bm) * bm
    pad_end = jnp.cumsum(padded)
    pad_start = pad_end - padded
    dest = pad_start[flat_e] + rank
    nb = -(-m // bm) + N_EXPERTS
    block_e = jnp.clip(jnp.searchsorted(pad_end, jnp.arange(nb) * bm, side='right'), 0, N_EXPERTS - 1)
    n_used = (pad_end[-1] // bm).astype(jnp.int32).reshape(1)
    block_e = jnp.where(jnp.arange(nb) < n_used[0], block_e, block_e[jnp.maximum(n_used[0] - 1, 0)])
    src_tok = jnp.zeros((nb * bm,), jnp.int32).at[dest].set(jnp.arange(m, dtype=jnp.int32) // TOP_K)
    return gate, dest.reshape(t, TOP_K), src_tok, block_e.astype(jnp.int32), n_used


def _ln2_kernel(h_ref, m_ref, g_ref, b_ref, o_ref, *, alpha):
    o_ref[...] = _ln_rows(alpha * h_ref[...] + m_ref[...], g_ref[...], b_ref[...])


def _ln2(h, moe_out, row0, g, b, alpha):
    t = h.shape[0]
    tm = min(512, t)
    off = row0 // tm
    return pl.pallas_call(
        functools.partial(_ln2_kernel, alpha=alpha),
        grid=(t // tm,),
        in_specs=[pl.BlockSpec((tm, D_MODEL), lambda i: (i, 0)),
                  pl.BlockSpec((tm, D_MODEL), lambda i: (i + off, 0)),
                  pl.BlockSpec((1, D_MODEL), lambda i: (0, 0)),
                  pl.BlockSpec((1, D_MODEL), lambda i: (0, 0))],
        out_specs=pl.BlockSpec((tm, D_MODEL), lambda i: (i, 0)),
        out_shape=jax.ShapeDtypeStruct((t, D_MODEL), F32),
        compiler_params=_cparams(("parallel",)),
        name="ln2",
    )(h, moe_out, g, b)


def _rope_tables(pos):
    half = ROT_DIM // 2
    inv_freq = ROPE_THETA ** (-jnp.arange(0, ROT_DIM, 2, dtype=F32) / ROT_DIM)
    ang = pos.astype(F32)[:, None] * inv_freq[None, :]
    cos, sin = jnp.cos(ang), jnp.sin(ang)
    m = np.arange(LANES) % DQK_A
    idx = m % half
    cos_l = jnp.where(m < ROT_DIM, cos[:, idx], 1.0)
    sa = jnp.where(m < half, -sin[:, idx], 0.0)
    sb = jnp.where((m >= half) & (m < ROT_DIM), sin[:, idx], 0.0)
    return cos_l, sa, sb


def _prep_w_in(w):
    r0 = COL_C
    r1 = r0 + GATE_RANK
    pad = jnp.zeros((w.shape[0], LANES - GATE_RANK), w.dtype)
    return jnp.concatenate([w[:, :r0], w[:, r1:], w[:, r0:r1], pad], axis=1).astype(BF16)


def _tile_lanes(v, reps):
    return jnp.tile(v.reshape(1, -1), (1, reps)).astype(F32)


def _blockdiag_states(s):
    n = s.shape[0]
    eye = jnp.eye(HB, dtype=s.dtype)
    return jnp.einsum('nhde,hg->nhdge', s, eye).reshape(n, HB * DK_B, HB * DV_B)


def _diag_states(sbd):
    n = sbd.shape[0]
    s = sbd.reshape(n, HB, DK_B, HB, DV_B)
    return jnp.stack([s[:, h, :, h, :] for h in range(HB)], axis=1)


def kernel(x_prompt, x_sample, cache_k, cache_v, page_table, state_gla, w_in, lam_q1, lam_k1, lam_q2, lam_k2, attn_norm_g, gla_w_gate, gla_b_gate, gla_norm_g, cmlp_ln_g, cmlp_ln_b, cmlp_ws, cmlp_bs, w_o, ln1_g, ln1_b, router_w, router_b, exp_w1, exp_b1, exp_w2, exp_b2, ln2_g, ln2_b):
    depth = w_in.shape[0]
    bp, s_len, _ = x_prompt.shape
    db, l_new, _ = x_sample.shape
    n_phys, page = cache_k.shape[1], cache_k.shape[2]
    past_len = page_table.shape[1] * page
    alpha = (2 * depth) ** 0.25
    tp, ts = bp * s_len, db * l_new
    bm = MOE_BLOCK

    tabs_p = _rope_tables(jnp.arange(s_len))
    tabs_s = _rope_tables(past_len + (jnp.arange(ts) % l_new))
    page_table = page_table.astype(jnp.int32)

    hp = x_prompt.reshape(tp, D_MODEL)
    hs = x_sample.reshape(ts, D_MODEL)
    outs = {k: [] for k in ("kp", "vp", "gp", "ks", "vs", "gs", "cs")}
    for l in range(depth):
        lam_init = 0.8 - 0.6 * math.exp(-0.3 * l)
        w = _prep_w_in(w_in[l])
        lamv = jnp.pad(jnp.stack([lam_q1[l], lam_k1[l], lam_q2[l], lam_k2[l]]).astype(F32),
                       ((0, 0), (0, LANES - DQK_A)))
        g_attn = attn_norm_g[l].reshape(1, DV_A).astype(F32)
        wg = jnp.pad(gla_w_gate[l], ((0, LANES - GATE_RANK), (0, 0))).astype(BF16)
        wo = w_o[l].astype(BF16)
        rw = jnp.pad(router_w[l].astype(F32), ((0, 0), (0, LANES - N_EXPERTS)))
        rb = jnp.pad(router_b[l].astype(F32), (0, LANES - N_EXPERTS), constant_values=NEG_INF).reshape(1, LANES)
        ln1 = (ln1_g[l].reshape(1, D_MODEL), ln1_b[l].reshape(1, D_MODEL))

        def mixer_params(lc, n_rows):
            reps = n_rows // lc
            ws = jnp.tile(cmlp_ws[l][:, :lc, :lc], (1, reps, reps))
            bst = jnp.tile(jnp.repeat(cmlp_bs[l][:, :lc].T, DC, axis=1), (reps, 1))
            return (wg, gla_b_gate[l].reshape(1, W_B), _tile_lanes(gla_norm_g[l], HB),
                    _tile_lanes(cmlp_ln_g[l], HC), _tile_lanes(cmlp_ln_b[l], HC), ws, bst)

        q, k, v, kb, vb, g_in, c_in, br = _inproj(hp, w, tabs_p)
        o_a = _attn_prompt(q, kb, vb, lamv, g_attn, bp, lam_init)
        o_bc, st_p = _mixer_prompt(g_in, br, c_in, mixer_params(CMLP_CHUNK, CMLP_CHUNK), bp)
        hp1, hp1b, lg_p = _outproj(o_a, o_bc, hp, wo, *ln1, rw, rb, alpha)
        outs["kp"].append(k.reshape(bp, s_len, HA, 2 * DQK_A))
        outs["vp"].append(v.reshape(bp, s_len, HA, DV_A))
        outs["gp"].append(_diag_states(st_p))

        q, k, v, kb, vb, g_in, c_in, br = _inproj(hs, w, tabs_s)
        o_a = _attn_sample(q, kb, vb, cache_k, cache_v, l, page_table, lamv, g_attn, l_new, lam_init)
        rows_s = min(64, ts)
        lc = min(l_new, CMLP_CHUNK)
        o_bc, vn, st_s = _mixer_sample(g_in, br, c_in, _blockdiag_states(state_gla[l].astype(F32)),
                                       mixer_params(lc, rows_s), l_new, rows_s)
        hs1, hs1b, lg_s = _outproj(o_a, o_bc, hs, wo, *ln1, rw, rb, alpha)
        outs["ks"].append(k.reshape(db, l_new, HA, 2 * DQK_A))
        outs["vs"].append(v.reshape(db, l_new, HA, DV_A))
        outs["gs"].append(_diag_states(st_s))
        outs["cs"].append(vn.reshape(db, l_new, W_C))

        gate, dest, src_tok, block_e, n_used = _route(jnp.concatenate([lg_p, lg_s], axis=0), bm)
        x_pad = jnp.concatenate([hp1b, hs1b], axis=0)[src_tok]
        y_pad = _moe_experts(x_pad, block_e, n_used, exp_w1[l], exp_b1[l].reshape(N_EXPERTS, 1, -1),
                             exp_w2[l], exp_b2[l].reshape(N_EXPERTS, 1, -1), bm)
        moe_out = jnp.sum(y_pad[dest] * gate[:, :, None], axis=1)
        hp = _ln2(hp1, moe_out, 0, ln2_g[l].reshape(1, -1), ln2_b[l].reshape(1, -1), alpha)
        hs = _ln2(hs1, moe_out, tp, ln2_g[l].reshape(1, -1), ln2_b[l].reshape(1, -1), alpha)

    return (hp.reshape(bp, s_len, D_MODEL), hs.reshape(db, l_new, D_MODEL),
            jnp.stack(outs["kp"]), jnp.stack(outs["vp"]), jnp.stack(outs["gp"]),
            jnp.stack(outs["ks"]), jnp.stack(outs["vs"]), jnp.stack(outs["gs"]), jnp.stack(outs["cs"]))
```

```python
import functools
import math

import numpy as np
import jax
import jax.numpy as jnp
from jax import lax
from jax.experimental import pallas as pl
from jax.experimental.pallas import tpu as pltpu
from jax.experimental.pallas import tpu_sc as plsc

F32, BF16 = jnp.float32, jnp.bfloat16
LANES = 128
VMEM_LIMIT = 48 * 1024 * 1024

D_MODEL = 1024
HA, DQK_A, DV_A = 4, 64, 128
ROT_DIM = DQK_A // 4
ROPE_THETA = 500000.0
HB, DK_B, DV_B = 4, 64, 64
GATE_RANK = 16
GATE_NORM = 16.0
HC, DC = 4, 64
CMLP_CHUNK = 128
N_EXPERTS = 32
TOP_K = 4
D_FF = D_MODEL
SWIGLU_LIMIT = 7.0
SWIGLU_ALPHA = 1.702
LN_EPS = 1e-5
RMS_EPS = 1e-6
NEG_INF = -1e30
LOG2E = math.log2(math.e)

W_A = HA * 2 * DQK_A
W_B = HB * DK_B
W_C = HC * DC
COL_G = 3 * W_A
COL_C = COL_G + 4 * W_B
COL_R = COL_C + 2 * W_C
COL_END = COL_R + LANES
GLA_CHUNK_PROMPT = 32
MOE_BLOCK = 256


def _cparams(sem):
    return pltpu.CompilerParams(dimension_semantics=sem, vmem_limit_bytes=VMEM_LIMIT)


def _split3(x):
    hi = x.astype(BF16)
    r = x - hi.astype(F32)
    mid = r.astype(BF16)
    lo = (r - mid.astype(F32)).astype(BF16)
    return hi, mid, lo


def _dot_sel(sel_bf16, x):
    acc = None
    for p in _split3(x):
        d = jnp.dot(sel_bf16, p, preferred_element_type=F32)
        acc = d if acc is None else acc + d
    return acc


def _seg_sum(x, bd_bf16):
    acc = None
    for p in _split3(x):
        d = jnp.dot(p, bd_bf16, preferred_element_type=F32)
        acc = d if acc is None else acc + d
    return acc


def _iota2(shape, dim):
    return lax.broadcasted_iota(jnp.int32, shape, dim)


def _idiv(x, n):
    shift = n.bit_length() - 1
    assert n == 1 << shift
    return x >> shift


def _head_blockdiag(n):
    r, c = _iota2((n, n), 0), _iota2((n, n), 1)
    return _idiv(r, DK_B) == _idiv(c, DK_B)


def _chunk_causal(n, chunk):
    r, c = _iota2((n, n), 0), _iota2((n, n), 1)
    return (_idiv(r, chunk) == _idiv(c, chunk)) & (c <= r)


def _ln_rows(x, g, b):
    mu = jnp.mean(x, axis=-1, keepdims=True)
    xc = x - mu
    var = jnp.mean(xc * xc, axis=-1, keepdims=True)
    return xc * lax.rsqrt(var + LN_EPS) * g + b


def _inproj_kernel(x_ref, w_ref, cos_ref, sa_ref, sb_ref,
                   q_ref, k_ref, v_ref, kb_ref, vb_ref, g_ref, c_ref, br_ref):
    xb = x_ref[...].astype(BF16)

    def proj(a, b):
        return jnp.dot(xb, w_ref[:, a:b], preferred_element_type=F32)

    cos, sa, sb = cos_ref[...], sa_ref[...], sb_ref[...]

    def rope(z):
        outs = []
        for i in range(z.shape[1] // LANES):
            zi = z[:, i * LANES:(i + 1) * LANES]
            outs.append(zi * cos + pltpu.roll(zi, LANES - ROT_DIM // 2, 1) * sa
                        + pltpu.roll(zi, ROT_DIM // 2, 1) * sb)
        return jnp.concatenate(outs, axis=1)

    q_ref[...] = (rope(proj(0, W_A)) * (DQK_A ** -0.5 * LOG2E)).astype(BF16)
    k = rope(proj(W_A, 2 * W_A))
    kb_ref[...] = k.astype(BF16)
    v = proj(2 * W_A, 3 * W_A)
    vb_ref[...] = v.astype(BF16)
    for h in range(HA):
        k_ref[:, h, :] = k[:, h * LANES:(h + 1) * LANES]
        v_ref[:, h, :] = v[:, h * LANES:(h + 1) * LANES]
    g_ref[...] = proj(COL_G, COL_C)
    c_ref[...] = proj(COL_C, COL_R)
    br_ref[...] = proj(COL_R, COL_END)


def _inproj(x, w, tabs):
    t = x.shape[0]
    tm = min(512, t)
    cos, sa, sb = tabs
    ntab = cos.shape[0] // tm
    row = lambda n: pl.BlockSpec((tm, n), lambda i: (i, 0))
    tab = pl.BlockSpec((tm, LANES), lambda i: (i % ntab, 0))
    heads = pl.BlockSpec((tm, HA, DV_A), lambda i: (i, 0, 0))
    shapes = [((W_A,), BF16), ((HA, DV_A), F32), ((HA, DV_A), F32), ((W_A,), BF16), ((W_A,), BF16),
              ((4 * W_B,), F32), ((2 * W_C,), F32), ((LANES,), F32)]
    return pl.pallas_call(
        _inproj_kernel,
        grid=(t // tm,),
        in_specs=[row(D_MODEL), pl.BlockSpec((D_MODEL, COL_END), lambda i: (0, 0)), tab, tab, tab],
        out_specs=[heads if len(n) == 2 else row(n[0]) for n, _ in shapes],
        out_shape=[jax.ShapeDtypeStruct((t,) + n, d) for n, d in shapes],
        compiler_params=_cparams(("parallel",)),
        name="inproj",
    )(x, w, cos, sa, sb)


def _diff_lambda(lamv, lam_init):
    a = jnp.sum(lamv[0:1] * lamv[1:2], axis=1, keepdims=True)
    b = jnp.sum(lamv[2:3] * lamv[3:4], axis=1, keepdims=True)
    return jnp.exp(a) - jnp.exp(b) + lam_init


def _diff_finish(o1, o2, lam, g, lam_init):
    o = o1 - lam * o2
    ms = jnp.mean(o * o, axis=-1, keepdims=True)
    return o * lax.rsqrt(ms + RMS_EPS) * g * (1.0 - lam_init)


def _split_maps(q):
    lane = _iota2(q.shape, 1)
    zero = jnp.zeros_like(q)
    return jnp.concatenate([jnp.where(lane < DQK_A, q, zero), jnp.where(lane >= DQK_A, q, zero)], axis=0)


def _attn_prompt_kernel(q_ref, k_ref, v_ref, lamv_ref, g_ref, o_ref, m_sc, l_sc, acc_sc, *, blk, lam_init):
    qi = pl.program_id(2)
    qq = _split_maps(q_ref[...])
    m_sc[...] = jnp.full(m_sc.shape, NEG_INF, F32)
    l_sc[...] = jnp.zeros(l_sc.shape, F32)
    acc_sc[...] = jnp.zeros(acc_sc.shape, F32)

    def step(j, masked):
        start = pl.multiple_of(j * blk, blk)
        k = k_ref[pl.ds(start, blk), :]
        v = v_ref[pl.ds(start, blk), :]
        s = lax.dot_general(qq, k, (((1,), (1,)), ((), ())), preferred_element_type=F32)
        if masked:
            r = _iota2(s.shape, 0) & (blk - 1)
            c = _iota2(s.shape, 1)
            s = jnp.where(c <= r, s, NEG_INF)
        m_prev = m_sc[...]
        m_new = jnp.maximum(m_prev, jnp.max(s, axis=1, keepdims=True))
        alpha = jnp.exp2(m_prev - m_new)
        p = jnp.exp2(s - jnp.tile(m_new, (1, blk // LANES)))
        l_sc[...] = alpha * l_sc[...] + jnp.sum(p, axis=1, keepdims=True)
        acc_sc[...] = alpha * acc_sc[...] + jnp.dot(p.astype(BF16), v, preferred_element_type=F32)
        m_sc[...] = m_new

    def body(j, carry):
        step(j, False)
        return carry

    lax.fori_loop(0, qi, body, 0)
    step(qi, True)

    o = acc_sc[...] / l_sc[...]
    lam = _diff_lambda(lamv_ref[...], lam_init)
    o_ref[...] = _diff_finish(o[:blk], o[blk:], lam, g_ref[...], lam_init).astype(o_ref.dtype)


def _attn_prompt(q, kb, vb, lamv, g, nbatch, lam_init):
    t = q.shape[0]
    s = t // nbatch
    blk = min(512, s)
    nq = s // blk
    kern = functools.partial(_attn_prompt_kernel, blk=blk, lam_init=lam_init)
    return pl.pallas_call(
        kern,
        grid=(nbatch, HA, nq),
        in_specs=[pl.BlockSpec((blk, LANES), lambda b, h, i: (b * nq + i, h)),
                  pl.BlockSpec((s, LANES), lambda b, h, i: (b, h)),
                  pl.BlockSpec((s, LANES), lambda b, h, i: (b, h)),
                  pl.BlockSpec((4, LANES), lambda b, h, i: (0, 0)),
                  pl.BlockSpec((1, LANES), lambda b, h, i: (0, 0))],
        out_specs=pl.BlockSpec((blk, LANES), lambda b, h, i: (b * nq + i, h)),
        out_shape=jax.ShapeDtypeStruct((t, W_A), BF16),
        scratch_shapes=[pltpu.VMEM((2 * blk, LANES), F32)] * 3,
        compiler_params=_cparams(("parallel", "parallel", "arbitrary")),
        name="attn_prompt",
    )(q, kb, vb, lamv, g)


def _attn_sample_kernel(pt_ref, q_ref, kn_ref, vn_ref, lamv_ref, g_ref, *rest, n_pages, l_new, lam_init):
    del pt_ref
    kp = rest[:n_pages]
    vp = rest[n_pages:2 * n_pages]
    o_ref = rest[2 * n_pages]
    rows_pg = kp[0].shape[0]
    rpad = q_ref.shape[0]
    nq = 2 * rpad
    lam = _diff_lambda(lamv_ref[...], lam_init)
    q = q_ref[...]
    qq = jnp.concatenate([_split_maps(q[:, h * LANES:(h + 1) * LANES]) for h in range(HA)],
                         axis=0).astype(BF16)
    r = _iota2((HA * nq, rows_pg), 0)
    c = _iota2((HA * nq, rows_pg), 1)
    head_ok = (c & (HA - 1)) == _idiv(r, nq)
    new_ok = head_ok & (_idiv(c, HA) <= (r & (rpad - 1))) & (c < l_new * HA)
    nt = (((1,), (1,)), ((), ()))
    zpad = jnp.zeros((rows_pg - kn_ref.shape[0], LANES), F32)
    k_new = jnp.concatenate([kn_ref[...], zpad], axis=0).astype(BF16)
    v_new = jnp.concatenate([vn_ref[...], zpad], axis=0).astype(BF16)
    s_new = jnp.where(new_ok, lax.dot_general(qq, k_new, nt, preferred_element_type=F32), NEG_INF)
    s_past = [jnp.where(head_ok, lax.dot_general(qq, kp[j][...].astype(BF16), nt, preferred_element_type=F32),
                        NEG_INF) for j in range(n_pages)]
    m = jnp.max(s_new, axis=1, keepdims=True)
    for sj in s_past:
        m = jnp.maximum(m, jnp.max(sj, axis=1, keepdims=True))
    p_new = jnp.exp2(s_new - m)
    l = jnp.sum(p_new, axis=1, keepdims=True)
    acc = jnp.dot(p_new.astype(BF16), v_new, preferred_element_type=F32)
    for j in range(n_pages):
        pj = jnp.exp2(s_past[j] - m)
        l = l + jnp.sum(pj, axis=1, keepdims=True)
        acc = acc + jnp.dot(pj.astype(BF16), vp[j][...].astype(BF16), preferred_element_type=F32)
    o = acc / l
    outs = [_diff_finish(o[h * nq:h * nq + rpad], o[h * nq + rpad:(h + 1) * nq], lam, g_ref[...], lam_init)
            for h in range(HA)]
    o_ref[...] = jnp.concatenate(outs, axis=1)


def _attn_sample(q, k, v, cache_k, cache_v, layer, page_table, lamv, g, l_new, lam_init):
    t = q.shape[0]
    db = t // l_new
    n_pages = page_table.shape[1]
    page = cache_k.shape[2]
    rpad = 8

    def pad_rows(a):
        a = a.reshape(db, l_new, W_A).astype(F32)
        return jnp.concatenate([a, jnp.zeros((db, rpad - l_new, W_A), F32)], axis=1)

    def new_rows(a):
        return a.reshape(db, l_new * HA, DV_A).astype(F32)

    depth, n_phys = cache_k.shape[:2]
    cache_k = cache_k.reshape(depth, n_phys, page * HA, DV_A)
    cache_v = cache_v.reshape(depth, n_phys, page * HA, DV_A)
    q_spec = pl.BlockSpec((None, rpad, W_A), lambda b, pt: (b, 0, 0))
    new_spec = pl.BlockSpec((None, l_new * HA, DV_A), lambda b, pt: (b, 0, 0))
    page_specs = [pl.BlockSpec((None, None, page * HA, DV_A),
                               functools.partial(lambda b, pt, j: (layer, pt[b, j], 0, 0), j=j))
                  for j in range(n_pages)]
    kern = functools.partial(_attn_sample_kernel, n_pages=n_pages, l_new=l_new, lam_init=lam_init)
    out = pl.pallas_call(
        kern,
        grid_spec=pltpu.PrefetchScalarGridSpec(
            num_scalar_prefetch=1,
            grid=(db,),
            in_specs=[q_spec, new_spec, new_spec,
                      pl.BlockSpec((4, LANES), lambda b, pt: (0, 0)),
                      pl.BlockSpec((1, LANES), lambda b, pt: (0, 0))] + page_specs + page_specs,
            out_specs=q_spec,
        ),
        out_shape=jax.ShapeDtypeStruct((db, rpad, W_A), F32),
        compiler_params=_cparams(("arbitrary",)),
        name="attn_sample",
    )(page_table, pad_rows(q), new_rows(k), new_rows(v), lamv, g,
      *([cache_k] * n_pages), *([cache_v] * n_pages))
    return out[:, :l_new].reshape(t, W_A).astype(BF16)


def _log_sigmoid(x):
    return jnp.minimum(x, 0.0) - jnp.log1p(jnp.exp(-jnp.abs(x)))


def _gla_gate(br, wg_ref, bgate_ref):
    x = jnp.dot(br.astype(BF16), wg_ref[...], preferred_element_type=F32) + bgate_ref[...]
    return _log_sigmoid(x) / GATE_NORM


def _gla_intra(q_att, k_in, v, chunk):
    keep = _chunk_causal(q_att.shape[0], chunk)
    lane = _idiv(_iota2(q_att.shape, 1), DK_B)
    kb = k_in.astype(BF16)
    atts, vs = [], []
    for h in range(HB):
        qh = jnp.where(lane == h, q_att, 0.0).astype(BF16)
        a = lax.dot_general(qh, kb, (((1,), (1,)), ((), ())), preferred_element_type=F32)
        atts.append(jnp.where(keep, a, 0.0).astype(BF16))
        vs.append(jnp.where(lane == h, v, 0.0).astype(BF16))
    return jnp.dot(jnp.concatenate(atts, axis=1), jnp.concatenate(vs, axis=0), preferred_element_type=F32)


def _gla_finish(o, gate_in, gng_ref, bd):
    ms = _seg_sum(o * o, bd) * (1.0 / DV_B)
    o = o * lax.rsqrt(ms + RMS_EPS) * gng_ref[...]
    return o * (gate_in * (1.0 / (1.0 + jnp.exp(-gate_in))))


def _chunk_mlp(c_in, lng_ref, lnb_ref, ws_ref, bst, chunk, bd):
    n = c_in.shape[0]
    cu, cv = c_in[:, :W_C], c_in[:, W_C:]
    mu = _seg_sum(cv, bd) * (1.0 / DC)
    xc = cv - mu
    var = _seg_sum(xc * xc, bd) * (1.0 / DC)
    vn = xc * lax.rsqrt(var + LN_EPS) * lng_ref[...] + lnb_ref[...]
    keep = _chunk_causal(n, chunk)
    lane = _idiv(_iota2(vn.shape, 1), DC)
    ws, vs = [], []
    for g in range(HC):
        ws.append(jnp.where(keep, ws_ref[g], 0.0).astype(BF16))
        vs.append(jnp.where(lane == g, vn, 0.0).astype(BF16))
    mixed = jnp.dot(jnp.concatenate(ws, axis=1), jnp.concatenate(vs, axis=0), preferred_element_type=F32) + bst
    return cu * mixed, vn


def _mixer_prompt_kernel(g_ref, br_ref, c_ref, wg_ref, bgate_ref, gng_ref, lng_ref, lnb_ref, ws_ref, bst_ref,
                         o_ref, st_ref, st_sc, *, ts, chunk):
    t = pl.program_id(1)

    @pl.when(t == 0)
    def _():
        st_sc[...] = jnp.zeros(st_sc.shape, F32)

    grp = CMLP_CHUNK
    bd = _head_blockdiag(W_B)
    bd_bf = jnp.where(bd, 1.0, 0.0).astype(BF16)
    csum_sel = jnp.where(_chunk_causal(grp, chunk), 1.0, 0.0).astype(BF16)
    rows = _iota2((grp, W_B), 0)
    half = chunk // 2
    for gi in range(ts // grp):
        rs = slice(gi * grp, (gi + 1) * grp)
        g = g_ref[rs, :]
        gq = g[:, 0:W_B] * (DK_B ** -0.5)
        gk, gv, gg = g[:, W_B:2 * W_B], g[:, 2 * W_B:3 * W_B], g[:, 3 * W_B:4 * W_B]
        la = _gla_gate(br_ref[rs, :], wg_ref, bgate_ref)
        bcum = _dot_sel(csum_sel, la)
        mids, lasts = [], []
        for ci in range(grp // chunk):
            mids.append(jnp.broadcast_to(bcum[ci * chunk + half - 1:ci * chunk + half, :], (chunk, W_B)))
            lasts.append(jnp.broadcast_to(bcum[(ci + 1) * chunk - 1:(ci + 1) * chunk, :], (chunk, W_B)))
        bmid = jnp.concatenate(mids, axis=0)
        blast = jnp.concatenate(lasts, axis=0)
        q_att = gq * jnp.exp(bcum - bmid)
        k_in = gk * jnp.exp(bmid - bcum)
        k_end = gk * jnp.exp(blast - bcum)
        q_dec = (gq * jnp.exp(bcum)).astype(BF16)
        o = _gla_intra(q_att, k_in, gv, chunk)
        v_t = gv.T.astype(BF16)
        o_inter = []
        for ci in range(grp // chunk):
            cs = slice(ci * chunk, (ci + 1) * chunk)
            st = st_sc[...]
            o_inter.append(lax.dot_general(q_dec[cs], st.astype(BF16), (((1,), (1,)), ((), ())),
                                           preferred_element_type=F32))
            kem = jnp.where(_idiv(rows, chunk) == ci, k_end, 0.0).astype(BF16)
            upd = jnp.dot(v_t, kem, preferred_element_type=F32)
            dl = jnp.exp(blast[ci * chunk:ci * chunk + 1, :])
            st_sc[...] = st * dl + jnp.where(bd, upd, 0.0)
        o = o + jnp.concatenate(o_inter, axis=0)
        o_b = _gla_finish(o, gg, gng_ref, bd_bf)
        o_c, _ = _chunk_mlp(c_ref[rs, :], lng_ref, lnb_ref, ws_ref, bst_ref[...], CMLP_CHUNK, bd_bf)
        o_ref[rs, :] = jnp.concatenate([o_b, o_c], axis=1).astype(o_ref.dtype)

    @pl.when(t == pl.num_programs(1) - 1)
    def _():
        st_ref[...] = st_sc[...].T


def _mixer_prompt(g_in, br, c_in, prm, nbatch, ts=256):
    t = g_in.shape[0]
    nt = t // nbatch // ts
    row = lambda n: pl.BlockSpec((ts, n), lambda b, i: (b * nt + i, 0))
    full = lambda a: pl.BlockSpec(a.shape, lambda b, i: (0,) * a.ndim)
    kern = functools.partial(_mixer_prompt_kernel, ts=ts, chunk=GLA_CHUNK_PROMPT)
    return pl.pallas_call(
        kern,
        grid=(nbatch, nt),
        in_specs=[row(4 * W_B), row(LANES), row(2 * W_C)] + [full(a) for a in prm],
        out_specs=[row(W_B + W_C), pl.BlockSpec((None, W_B, W_B), lambda b, i: (b, 0, 0))],
        out_shape=[jax.ShapeDtypeStruct((t, W_B + W_C), BF16),
                   jax.ShapeDtypeStruct((nbatch, W_B, W_B), F32)],
        scratch_shapes=[pltpu.VMEM((W_B, W_B), F32)],
        compiler_params=_cparams(("parallel", "arbitrary")),
        name="mixer_prompt",
    )(g_in, br, c_in, *prm)


def _mixer_sample_kernel(g_ref, br_ref, c_ref, s0_ref, wg_ref, bgate_ref, gng_ref, lng_ref, lnb_ref, ws_ref,
                         bst_ref, o_ref, vn_ref, st_ref, *, l_new):
    n = g_ref.shape[0]
    bd = _head_blockdiag(W_B)
    bd_bf = jnp.where(bd, 1.0, 0.0).astype(BF16)
    r, c = _iota2((n, n), 0), _iota2((n, n), 1)
    csum_sel = jnp.where(_chunk_causal(n, l_new), 1.0, 0.0).astype(BF16)
    last_sel = jnp.where(_idiv(r, l_new) == _idiv(c, l_new), 1.0, 0.0).astype(BF16)
    g = g_ref[...]
    gq = g[:, 0:W_B] * (DK_B ** -0.5)
    gk, gv, gg = g[:, W_B:2 * W_B], g[:, 2 * W_B:3 * W_B], g[:, 3 * W_B:4 * W_B]
    la = _gla_gate(br_ref[...], wg_ref, bgate_ref)
    bcum = _dot_sel(csum_sel, la)
    blast = _dot_sel(last_sel, la)
    q_in = gq * jnp.exp(bcum)
    k_in = gk * jnp.exp(-bcum)
    k_end = gk * jnp.exp(blast - bcum)
    o = _gla_intra(q_in, k_in, gv, l_new)
    zrows = jnp.zeros((LANES - n, W_B), F32)
    ke_t = jnp.concatenate([k_end, zrows], axis=0).T
    bl_t = jnp.concatenate([blast, zrows], axis=0).T
    v_pad = jnp.concatenate([gv, zrows], axis=0).astype(BF16)
    rows = _iota2((n, W_B), 0)
    cols = _iota2((W_B, LANES), 1)
    zblk = jnp.zeros((DK_B, DV_B), F32)
    for s in range(n // l_new):
        s0 = jnp.concatenate(
            [jnp.concatenate([s0_ref[s, h] if g == h else zblk for g in range(HB)], axis=1) for h in range(HB)],
            axis=0)
        qs = jnp.where(_idiv(rows, l_new) == s, q_in, 0.0).astype(BF16)
        o = o + jnp.dot(qs, s0.astype(BF16), preferred_element_type=F32)
        kes = jnp.where(_idiv(cols, l_new) == s, ke_t, 0.0).astype(BF16)
        upd = jnp.dot(kes, v_pad, preferred_element_type=F32)
        dl = jnp.exp(bl_t[:, s * l_new:s * l_new + 1])
        fin = s0 * dl + upd
        for h in range(HB):
            st_ref[s, h] = fin[h * DK_B:(h + 1) * DK_B, h * DV_B:(h + 1) * DV_B]
    o_b = _gla_finish(o, gg, gng_ref, bd_bf)
    o_c, vn = _chunk_mlp(c_ref[...], lng_ref, lnb_ref, ws_ref, bst_ref[...], l_new, bd_bf)
    o_ref[...] = jnp.concatenate([o_b, o_c], axis=1).astype(o_ref.dtype)
    vn_ref[...] = vn


def _mixer_sample(g_in, br, c_in, s0bd, prm, l_new, ts=64):
    t = g_in.shape[0]
    ns = ts // l_new
    row = lambda n: pl.BlockSpec((ts, n), lambda i: (i, 0))
    full = lambda a: pl.BlockSpec(a.shape, lambda i: (0,) * a.ndim)
    st = pl.BlockSpec((ns, HB, DK_B, DV_B), lambda i: (i, 0, 0, 0))
    kern = functools.partial(_mixer_sample_kernel, l_new=l_new)
    return pl.pallas_call(
        kern,
        grid=(t // ts,),
        in_specs=[row(4 * W_B), row(LANES), row(2 * W_C), st] + [full(a) for a in prm],
        out_specs=[row(W_B + W_C), row(W_C), st],
        out_shape=[jax.ShapeDtypeStruct((t, W_B + W_C), BF16),
                   jax.ShapeDtypeStruct((t, W_C), F32),
                   jax.ShapeDtypeStruct(s0bd.shape, F32)],
        compiler_params=_cparams(("parallel",)),
        name="mixer_sample",
    )(g_in, br, c_in, s0bd, *prm)


def _outproj_kernel(oa_ref, obc_ref, x_ref, wo_ref, g_ref, b_ref, rw_ref, rb_ref, h_ref, lg_ref, *, alpha):
    y = jnp.dot(oa_ref[...], wo_ref[0:W_A, :], preferred_element_type=F32)
    y = y + jnp.dot(obc_ref[...], wo_ref[W_A:, :], preferred_element_type=F32)
    h = _ln_rows(alpha * x_ref[...] + y, g_ref[...], b_ref[...])
    h_ref[...] = h
    lg_ref[...] = jnp.dot(h, rw_ref[...], preferred_element_type=F32, precision=lax.Precision.HIGHEST) + rb_ref[...]


def _outproj(o_a, o_bc, x, wo, g, b, rw, rb, alpha):
    t = x.shape[0]
    tm = min(512, t)
    row = lambda n: pl.BlockSpec((tm, n), lambda i: (i, 0))
    full = lambda a: pl.BlockSpec(a.shape, lambda i: (0,) * a.ndim)
    return pl.pallas_call(
        functools.partial(_outproj_kernel, alpha=alpha),
        grid=(t // tm,),
        in_specs=[row(W_A), row(W_B + W_C), row(D_MODEL)] + [full(a) for a in (wo, g, b, rw, rb)],
        out_specs=[row(D_MODEL), row(LANES)],
        out_shape=[jax.ShapeDtypeStruct((t, D_MODEL), F32), jax.ShapeDtypeStruct((t, LANES), F32)],
        compiler_params=_cparams(("parallel",)),
        name="outproj",
    )(o_a, o_bc, x, wo, g, b, rw, rb)


def _moe_kernel(be_ref, nu_ref, x_ref, w1_ref, b1_ref, w2_ref, b2_ref, y_ref, w1b_sc, w2b_sc):
    i = pl.program_id(0)

    @pl.when((i == 0) | (be_ref[i] != be_ref[jnp.maximum(i - 1, 0)]))
    def _():
        w1b_sc[...] = w1_ref[...].astype(BF16)
        w2b_sc[...] = w2_ref[...].astype(BF16)

    @pl.when(i >= nu_ref[0])
    def _():
        y_ref[...] = jnp.zeros(y_ref.shape, F32)

    @pl.when(i < nu_ref[0])
    def _():
        hc = jnp.dot(x_ref[...].astype(BF16), w1b_sc[...], preferred_element_type=F32) + b1_ref[...]
        g = jnp.minimum(hc[:, :D_FF], SWIGLU_LIMIT)
        u = jnp.clip(hc[:, D_FF:], -SWIGLU_LIMIT, SWIGLU_LIMIT)
        act = (u + 1.0) * g * (1.0 / (1.0 + jnp.exp(-SWIGLU_ALPHA * g)))
        y_ref[...] = jnp.dot(act.astype(BF16), w2b_sc[...], preferred_element_type=F32) + b2_ref[...]


def _moe_experts(x_pad, block_e, n_used, w1, b1, w2, b2, bm):
    nb = x_pad.shape[0] // bm
    return pl.pallas_call(
        _moe_kernel,
        grid_spec=pltpu.PrefetchScalarGridSpec(
            num_scalar_prefetch=2,
            grid=(nb,),
            in_specs=[pl.BlockSpec((bm, D_MODEL), lambda i, be, nu: (i, 0)),
                      pl.BlockSpec((None, D_MODEL, 2 * D_FF), lambda i, be, nu: (be[i], 0, 0)),
                      pl.BlockSpec((None, 1, 2 * D_FF), lambda i, be, nu: (be[i], 0, 0)),
                      pl.BlockSpec((None, D_FF, D_MODEL), lambda i, be, nu: (be[i], 0, 0)),
                      pl.BlockSpec((None, 1, D_MODEL), lambda i, be, nu: (be[i], 0, 0))],
            out_specs=pl.BlockSpec((bm, D_MODEL), lambda i, be, nu: (i, 0)),
            scratch_shapes=[pltpu.VMEM((D_MODEL, 2 * D_FF), BF16), pltpu.VMEM((D_FF, D_MODEL), BF16)],
        ),
        out_shape=jax.ShapeDtypeStruct((nb * bm, D_MODEL), F32),
        compiler_params=_cparams(("arbitrary",)),
        name="moe_experts",
    )(block_e, n_used, x_pad, w1, b1, w2, b2)


def _route(logits, bm):
    t = logits.shape[0]
    m = t * TOP_K
    top_v, top_i = lax.top_k(logits[:, :N_EXPERTS], TOP_K)
    gate = jax.nn.softmax(top_v, axis=-1)
    flat_e = top_i.reshape(m)
    onehot = (flat_e[:, None] == jnp.arange(N_EXPERTS)[None, :]).astype(jnp.int32)
    csum = jnp.cumsum(onehot, axis=0)
    rank = jnp.take_along_axis(csum, flat_e[:, None], axis=1)[:, 0] - 1
    counts = csum[-1]
    padded = ((counts + bm - 1) // bm) * bm
    pad_end = jnp.cumsum(padded)
    pad_start = pad_end - padded
    dest = pad_start[flat_e] + rank
    nb = -(-m // bm) + N_EXPERTS
    block_e = jnp.clip(jnp.searchsorted(pad_end, jnp.arange(nb) * bm, side='right'), 0, N_EXPERTS - 1)
    n_used = (pad_end[-1] // bm).astype(jnp.int32).reshape(1)
    block_e = jnp.where(jnp.arange(nb) < n_used[0], block_e, block_e[jnp.maximum(n_used[0] - 1, 0)])
    src_tok = jnp.zeros((nb * bm,), jnp.int32).at[dest].set(jnp.arange(m, dtype=jnp.int32) // TOP_K)
    return gate, dest.reshape(t, TOP_K), src_tok, block_e.astype(jnp.int32), n_used


def _ln2_kernel(h_ref, gate_ref, *rest, alpha):
    y_refs, (g_ref, b_ref, o_ref) = rest[:TOP_K], rest[TOP_K:]
    gate = gate_ref[...]
    x = alpha * h_ref[...]
    for k in range(TOP_K):
        x = x + gate[:, k:k + 1] * y_refs[k][...]
    o_ref[...] = _ln_rows(x, g_ref[...], b_ref[...])


def _ln2(h, gate, yg, row0, g, b, alpha):
    t = h.shape[0]
    t_all = yg.shape[0] // TOP_K
    tm = min(512, t)
    assert row0 % tm == 0 and t_all % tm == 0
    off = row0 // tm
    nt_all = t_all // tm
    y_specs = [pl.BlockSpec((tm, D_MODEL), functools.partial(lambda i, k: (k * nt_all + off + i, 0), k=k))
               for k in range(TOP_K)]
    return pl.pallas_call(
        functools.partial(_ln2_kernel, alpha=alpha),
        grid=(t // tm,),
        in_specs=[pl.BlockSpec((tm, D_MODEL), lambda i: (i, 0)),
                  pl.BlockSpec((tm, LANES), lambda i: (i + off, 0))] + y_specs +
                 [pl.BlockSpec((1, D_MODEL), lambda i: (0, 0)),
                  pl.BlockSpec((1, D_MODEL), lambda i: (0, 0))],
        out_specs=pl.BlockSpec((tm, D_MODEL), lambda i: (i, 0)),
        out_shape=jax.ShapeDtypeStruct((t, D_MODEL), F32),
        compiler_params=_cparams(("parallel",)),
        name="ln2",
    )(h, gate, *([yg] * TOP_K), g, b)


SC_CORES, SC_SUBCORES = 2, 16
SC_CHUNK = 64


def _sc_gather(table, idx):
    b, d = idx.shape[0], table.shape[1]
    workers = SC_CORES * SC_SUBCORES
    per_w = b // workers
    assert per_w * workers == b and per_w % SC_CHUNK == 0
    mesh = plsc.VectorSubcoreMesh(core_axis_name="c", subcore_axis_name="s")

    @functools.partial(
        pl.kernel, mesh=mesh, out_type=jax.ShapeDtypeStruct((b, d), table.dtype),
        scratch_types=[pltpu.VMEM((SC_CHUNK,), jnp.int32), pltpu.VMEM((SC_CHUNK, d), table.dtype),
                       pltpu.SemaphoreType.DMA],
        name="sc_gather")
    def gather(table_hbm, idx_hbm, out_hbm, idx_v, rows_v, sem):
        wid = lax.axis_index("s") * SC_CORES + lax.axis_index("c")

        @pl.loop(0, per_w // SC_CHUNK)
        def _(i):
            base = wid * per_w + i * SC_CHUNK
            pltpu.sync_copy(idx_hbm.at[pl.ds(base, SC_CHUNK)], idx_v)
            pltpu.async_copy(table_hbm.at[idx_v], rows_v, sem).wait()
            pltpu.sync_copy(rows_v, out_hbm.at[pl.ds(base, SC_CHUNK)])

    return gather(table, idx)


def _rope_tables(pos):
    half = ROT_DIM // 2
    inv_freq = ROPE_THETA ** (-jnp.arange(0, ROT_DIM, 2, dtype=F32) / ROT_DIM)
    ang = pos.astype(F32)[:, None] * inv_freq[None, :]
    cos, sin = jnp.cos(ang), jnp.sin(ang)
    m = np.arange(LANES) % DQK_A
    idx = m % half
    cos_l = jnp.where(m < ROT_DIM, cos[:, idx], 1.0)
    sa = jnp.where(m < half, -sin[:, idx], 0.0)
    sb = jnp.where((m >= half) & (m < ROT_DIM), sin[:, idx], 0.0)
    return cos_l, sa, sb


def _prep_w_in(w):
    r0 = COL_C
    r1 = r0 + GATE_RANK
    pad = jnp.zeros((w.shape[0], LANES - GATE_RANK), w.dtype)
    return jnp.concatenate([w[:, :r0], w[:, r1:], w[:, r0:r1], pad], axis=1).astype(BF16)


def _tile_lanes(v, reps):
    return jnp.tile(v.reshape(1, -1), (1, reps)).astype(F32)


def _blockdiag_states(s):
    n = s.shape[0]
    eye = jnp.eye(HB, dtype=s.dtype)
    return jnp.einsum('nhde,hg->nhdge', s, eye).reshape(n, HB * DK_B, HB * DV_B)


def _diag_states(sbd):
    n = sbd.shape[0]
    s = sbd.reshape(n, HB, DK_B, HB, DV_B)
    return jnp.stack([s[:, h, :, h, :] for h in range(HB)], axis=1)


def kernel(x_prompt, x_sample, cache_k, cache_v, page_table, state_gla, w_in, lam_q1, lam_k1, lam_q2, lam_k2, attn_norm_g, gla_w_gate, gla_b_gate, gla_norm_g, cmlp_ln_g, cmlp_ln_b, cmlp_ws, cmlp_bs, w_o, ln1_g, ln1_b, router_w, router_b, exp_w1, exp_b1, exp_w2, exp_b2, ln2_g, ln2_b):
    depth = w_in.shape[0]
    bp, s_len, _ = x_prompt.shape
    db, l_new, _ = x_sample.shape
    n_phys, page = cache_k.shape[1], cache_k.shape[2]
    past_len = page_table.shape[1] * page
    alpha = (2 * depth) ** 0.25
    tp, ts = bp * s_len, db * l_new
    bm = MOE_BLOCK

    tabs_p = _rope_tables(jnp.arange(s_len))
    tabs_s = _rope_tables(past_len + (jnp.arange(ts) % l_new))
    page_table = page_table.astype(jnp.int32)

    hp = x_prompt.reshape(tp, D_MODEL)
    hs = x_sample.reshape(ts, D_MODEL)
    outs = {k: [] for k in ("kp", "vp", "gp", "ks", "vs", "gs", "cs")}
    for l in range(depth):
        lam_init = 0.8 - 0.6 * math.exp(-0.3 * l)
        w = _prep_w_in(w_in[l])
        lamv = jnp.pad(jnp.stack([lam_q1[l], lam_k1[l], lam_q2[l], lam_k2[l]]).astype(F32),
                       ((0, 0), (0, LANES - DQK_A)))
        g_attn = attn_norm_g[l].reshape(1, DV_A).astype(F32)
        wg = jnp.pad(gla_w_gate[l], ((0, LANES - GATE_RANK), (0, 0))).astype(BF16)
        wo = w_o[l].astype(BF16)
        rw = jnp.pad(router_w[l].astype(F32), ((0, 0), (0, LANES - N_EXPERTS)))
        rb = jnp.pad(router_b[l].astype(F32), (0, LANES - N_EXPERTS), constant_values=NEG_INF).reshape(1, LANES)
        ln1 = (ln1_g[l].reshape(1, D_MODEL), ln1_b[l].reshape(1, D_MODEL))

        def mixer_params(lc, n_rows):
            reps = n_rows // lc
            ws = jnp.tile(cmlp_ws[l][:, :lc, :lc], (1, reps, reps))
            bst = jnp.tile(jnp.repeat(cmlp_bs[l][:, :lc].T, DC, axis=1), (reps, 1))
            return (wg, gla_b_gate[l].reshape(1, W_B), _tile_lanes(gla_norm_g[l], HB),
                    _tile_lanes(cmlp_ln_g[l], HC), _tile_lanes(cmlp_ln_b[l], HC), ws, bst)

        q, k, v, kb, vb, g_in, c_in, br = _inproj(hp, w, tabs_p)
        o_a = _attn_prompt(q, kb, vb, lamv, g_attn, bp, lam_init)
        o_bc, st_p = _mixer_prompt(g_in, br, c_in, mixer_params(CMLP_CHUNK, CMLP_CHUNK), bp)
        hp1, lg_p = _outproj(o_a, o_bc, hp, wo, *ln1, rw, rb, alpha)
        outs["kp"].append(k.reshape(bp, s_len, HA, 2 * DQK_A))
        outs["vp"].append(v.reshape(bp, s_len, HA, DV_A))
        outs["gp"].append(_diag_states(st_p))

        q, k, v, kb, vb, g_in, c_in, br = _inproj(hs, w, tabs_s)
        o_a = _attn_sample(q, kb, vb, cache_k, cache_v, l, page_table, lamv, g_attn, l_new, lam_init)
        rows_s = min(64, ts)
        lc = min(l_new, CMLP_CHUNK)
        o_bc, vn, st_s = _mixer_sample(g_in, br, c_in, state_gla[l].astype(F32),
                                       mixer_params(lc, rows_s), l_new, rows_s)
        hs1, lg_s = _outproj(o_a, o_bc, hs, wo, *ln1, rw, rb, alpha)
        outs["ks"].append(k.reshape(db, l_new, HA, 2 * DQK_A))
        outs["vs"].append(v.reshape(db, l_new, HA, DV_A))
        outs["gs"].append(st_s)
        outs["cs"].append(vn.reshape(db, l_new, W_C))

        gate, dest, src_tok, block_e, n_used = _route(jnp.concatenate([lg_p, lg_s], axis=0), bm)
        x_pad = _sc_gather(jnp.concatenate([hp1, hs1], axis=0), src_tok)
        y_pad = _moe_experts(x_pad, block_e, n_used, exp_w1[l], exp_b1[l].reshape(N_EXPERTS, 1, -1),
                             exp_w2[l], exp_b2[l].reshape(N_EXPERTS, 1, -1), bm)
        yg = _sc_gather(y_pad, dest.T.reshape(-1))
        gate = jnp.pad(gate, ((0, 0), (0, LANES - TOP_K)))
        hp = _ln2(hp1, gate, yg, 0, ln2_g[l].reshape(1, -1), ln2_b[l].reshape(1, -1), alpha)
        hs = _ln2(hs1, gate, yg, tp, ln2_g[l].reshape(1, -1), ln2_b[l].reshape(1, -1), alpha)

    return (hp.reshape(bp, s_len, D_MODEL), hs.reshape(db, l_new, D_MODEL),
            jnp.stack(outs["kp"]), jnp.stack(outs["vp"]), jnp.stack(outs["gp"]),
            jnp.stack(outs["ks"]), jnp.stack(outs["vs"]), jnp.stack(outs["gs"]), jnp.stack(outs["cs"]))
```

```python
import functools
import math

import numpy as np
import jax
import jax.numpy as jnp
from jax import lax
from jax.experimental import pallas as pl
from jax.experimental.pallas import tpu as pltpu
from jax.experimental.pallas import tpu_sc as plsc

F32, BF16 = jnp.float32, jnp.bfloat16
LANES = 128
VMEM_LIMIT = 48 * 1024 * 1024

D_MODEL = 1024
HA, DQK_A, DV_A = 4, 64, 128
ROT_DIM = DQK_A // 4
ROPE_THETA = 500000.0
HB, DK_B, DV_B = 4, 64, 64
GATE_RANK = 16
GATE_NORM = 16.0
HC, DC = 4, 64
CMLP_CHUNK = 128
N_EXPERTS = 32
TOP_K = 4
D_FF = D_MODEL
SWIGLU_LIMIT = 7.0
SWIGLU_ALPHA = 1.702
LN_EPS = 1e-5
RMS_EPS = 1e-6
NEG_INF = -1e30
LOG2E = math.log2(math.e)

W_A = HA * 2 * DQK_A
W_B = HB * DK_B
W_C = HC * DC
COL_G = 3 * W_A
COL_C = COL_G + 4 * W_B
COL_R = COL_C + 2 * W_C
COL_END = COL_R + LANES
GLA_CHUNK_PROMPT = 32
MOE_BLOCK = 256
MOE_FF_CHUNK = 512


def _cparams(sem):
    return pltpu.CompilerParams(dimension_semantics=sem, vmem_limit_bytes=VMEM_LIMIT)


def _split3(x):
    hi = x.astype(BF16)
    r = x - hi.astype(F32)
    mid = r.astype(BF16)
    lo = (r - mid.astype(F32)).astype(BF16)
    return hi, mid, lo


def _dot_sel(sel_bf16, x):
    acc = None
    for p in _split3(x):
        d = jnp.dot(sel_bf16, p, preferred_element_type=F32)
        acc = d if acc is None else acc + d
    return acc


def _seg_sum(x, bd_bf16):
    acc = None
    for p in _split3(x):
        d = jnp.dot(p, bd_bf16, preferred_element_type=F32)
        acc = d if acc is None else acc + d
    return acc


def _iota2(shape, dim):
    return lax.broadcasted_iota(jnp.int32, shape, dim)


def _idiv(x, n):
    shift = n.bit_length() - 1
    assert n == 1 << shift
    return x >> shift


def _head_blockdiag(n):
    r, c = _iota2((n, n), 0), _iota2((n, n), 1)
    return _idiv(r, DK_B) == _idiv(c, DK_B)


def _chunk_causal(n, chunk):
    r, c = _iota2((n, n), 0), _iota2((n, n), 1)
    return (_idiv(r, chunk) == _idiv(c, chunk)) & (c <= r)


def _ln_rows(x, g, b):
    mu = jnp.mean(x, axis=-1, keepdims=True)
    xc = x - mu
    var = jnp.mean(xc * xc, axis=-1, keepdims=True)
    return xc * lax.rsqrt(var + LN_EPS) * g + b


def _inproj_kernel(x_ref, w_ref, cos_ref, sa_ref, sb_ref, k_all_ref, v_all_ref,
                   q_ref, k_ref, v_ref, kb_ref, vb_ref, g_ref, c_ref, br_ref):
    del k_all_ref, v_all_ref
    xb = x_ref[...].astype(BF16)

    def proj(a, b):
        return jnp.dot(xb, w_ref[:, a:b], preferred_element_type=F32)

    cos, sa, sb = cos_ref[...], sa_ref[...], sb_ref[...]

    def rope(z):
        outs = []
        for i in range(z.shape[1] // LANES):
            zi = z[:, i * LANES:(i + 1) * LANES]
            outs.append(zi * cos + pltpu.roll(zi, LANES - ROT_DIM // 2, 1) * sa
                        + pltpu.roll(zi, ROT_DIM // 2, 1) * sb)
        return jnp.concatenate(outs, axis=1)

    q_ref[...] = (rope(proj(0, W_A)) * (DQK_A ** -0.5 * LOG2E)).astype(BF16)
    k = rope(proj(W_A, 2 * W_A))
    kb_ref[...] = k.astype(BF16)
    v = proj(2 * W_A, 3 * W_A)
    vb_ref[...] = v.astype(BF16)
    for h in range(HA):
        k_ref[:, h, :] = k[:, h * LANES:(h + 1) * LANES]
        v_ref[:, h, :] = v[:, h * LANES:(h + 1) * LANES]
    g_ref[...] = proj(COL_G, COL_C)
    c_ref[...] = proj(COL_C, COL_R)
    br_ref[...] = proj(COL_R, COL_END)


def _inproj(x, w, tabs, layer, k_all, v_all):
    t = x.shape[0]
    tm = min(512, t)
    nt = t // tm
    cos, sa, sb = tabs
    ntab = cos.shape[0] // tm
    row = lambda n: pl.BlockSpec((tm, n), lambda i: (i, 0))
    tab = pl.BlockSpec((tm, LANES), lambda i: (i % ntab, 0))
    heads = pl.BlockSpec((tm, HA, DV_A), lambda i: (layer * nt + i, 0, 0))
    anywhere = pl.BlockSpec(memory_space=pl.ANY)
    shapes = [((W_A,), BF16), None, None, ((W_A,), BF16), ((W_A,), BF16),
              ((4 * W_B,), F32), ((2 * W_C,), F32), ((LANES,), F32)]
    return pl.pallas_call(
        _inproj_kernel,
        grid=(nt,),
        in_specs=[row(D_MODEL), pl.BlockSpec((D_MODEL, COL_END), lambda i: (0, 0)), tab, tab, tab,
                  anywhere, anywhere],
        out_specs=[heads if s is None else row(s[0][0]) for s in shapes],
        out_shape=[jax.ShapeDtypeStruct(k_all.shape, F32) if s is None else jax.ShapeDtypeStruct((t,) + s[0], s[1])
                   for s in shapes],
        input_output_aliases={5: 1, 6: 2},
        compiler_params=_cparams(("parallel",)),
        name="inproj",
    )(x, w, cos, sa, sb, k_all, v_all)


def _diff_lambda(lamv, lam_init):
    a = jnp.sum(lamv[0:1] * lamv[1:2], axis=1, keepdims=True)
    b = jnp.sum(lamv[2:3] * lamv[3:4], axis=1, keepdims=True)
    return jnp.exp(a) - jnp.exp(b) + lam_init


def _diff_finish(o1, o2, lam, g, lam_init):
    o = o1 - lam * o2
    ms = jnp.mean(o * o, axis=-1, keepdims=True)
    return o * lax.rsqrt(ms + RMS_EPS) * g * (1.0 - lam_init)


def _split_maps(q):
    lane = _iota2(q.shape, 1)
    zero = jnp.zeros_like(q)
    return jnp.concatenate([jnp.where(lane < DQK_A, q, zero), jnp.where(lane >= DQK_A, q, zero)], axis=0)


def _attn_prompt_kernel(q_ref, k_ref, v_ref, lamv_ref, g_ref, o_ref, m_sc, l_sc, acc_sc, *, blk, lam_init):
    qi = pl.program_id(2)
    qq = _split_maps(q_ref[...])
    m_sc[...] = jnp.full(m_sc.shape, NEG_INF, F32)
    l_sc[...] = jnp.zeros(l_sc.shape, F32)
    acc_sc[...] = jnp.zeros(acc_sc.shape, F32)

    def step(j, masked):
        start = pl.multiple_of(j * blk, blk)
        k = k_ref[pl.ds(start, blk), :]
        v = v_ref[pl.ds(start, blk), :]
        s = lax.dot_general(qq, k, (((1,), (1,)), ((), ())), preferred_element_type=F32)
        if masked:
            r = _iota2(s.shape, 0) & (blk - 1)
            c = _iota2(s.shape, 1)
            s = jnp.where(c <= r, s, NEG_INF)
        m_prev = m_sc[...]
        m_new = jnp.maximum(m_prev, jnp.max(s, axis=1, keepdims=True))
        alpha = jnp.exp2(m_prev - m_new)
        p = jnp.exp2(s - jnp.tile(m_new, (1, blk // LANES)))
        l_sc[...] = alpha * l_sc[...] + jnp.sum(p, axis=1, keepdims=True)
        acc_sc[...] = alpha * acc_sc[...] + jnp.dot(p.astype(BF16), v, preferred_element_type=F32)
        m_sc[...] = m_new

    def body(j, carry):
        step(j, False)
        return carry

    lax.fori_loop(0, qi, body, 0)
    step(qi, True)

    o = acc_sc[...] / l_sc[...]
    lam = _diff_lambda(lamv_ref[...], lam_init)
    o_ref[...] = _diff_finish(o[:blk], o[blk:], lam, g_ref[...], lam_init).astype(o_ref.dtype)


def _attn_prompt(q, kb, vb, lamv, g, nbatch, lam_init):
    t = q.shape[0]
    s = t // nbatch
    blk = min(512, s)
    nq = s // blk
    kern = functools.partial(_attn_prompt_kernel, blk=blk, lam_init=lam_init)
    return pl.pallas_call(
        kern,
        grid=(nbatch, HA, nq),
        in_specs=[pl.BlockSpec((blk, LANES), lambda b, h, i: (b * nq + i, h)),
                  pl.BlockSpec((s, LANES), lambda b, h, i: (b, h)),
                  pl.BlockSpec((s, LANES), lambda b, h, i: (b, h)),
                  pl.BlockSpec((4, LANES), lambda b, h, i: (0, 0)),
                  pl.BlockSpec((1, LANES), lambda b, h, i: (0, 0))],
        out_specs=pl.BlockSpec((blk, LANES), lambda b, h, i: (b * nq + i, h)),
        out_shape=jax.ShapeDtypeStruct((t, W_A), BF16),
        scratch_shapes=[pltpu.VMEM((2 * blk, LANES), F32)] * 3,
        compiler_params=_cparams(("parallel", "parallel", "arbitrary")),
        name="attn_prompt",
    )(q, kb, vb, lamv, g)


def _attn_sample_kernel(pt_ref, q_ref, kn_ref, vn_ref, lamv_ref, g_ref, *rest, n_pages, l_new, lam_init):
    del pt_ref
    kp = rest[:n_pages]
    vp = rest[n_pages:2 * n_pages]
    o_ref = rest[2 * n_pages]
    rows_pg = kp[0].shape[0]
    rpad = q_ref.shape[0]
    nq = 2 * rpad
    lam = _diff_lambda(lamv_ref[...], lam_init)
    q = q_ref[...]
    qq = jnp.concatenate([_split_maps(q[:, h * LANES:(h + 1) * LANES]) for h in range(HA)],
                         axis=0).astype(BF16)
    r = _iota2((HA * nq, rows_pg), 0)
    c = _iota2((HA * nq, rows_pg), 1)
    head_ok = (c & (HA - 1)) == _idiv(r, nq)
    new_ok = head_ok & (_idiv(c, HA) <= (r & (rpad - 1))) & (c < l_new * HA)
    nt = (((1,), (1,)), ((), ()))
    zpad = jnp.zeros((rows_pg - kn_ref.shape[0], LANES), F32)
    k_new = jnp.concatenate([kn_ref[...], zpad], axis=0).astype(BF16)
    v_new = jnp.concatenate([vn_ref[...], zpad], axis=0).astype(BF16)
    s_new = jnp.where(new_ok, lax.dot_general(qq, k_new, nt, preferred_element_type=F32), NEG_INF)
    s_past = [jnp.where(head_ok, lax.dot_general(qq, kp[j][...].astype(BF16), nt, preferred_element_type=F32),
                        NEG_INF) for j in range(n_pages)]
    m = jnp.max(s_new, axis=1, keepdims=True)
    for sj in s_past:
        m = jnp.maximum(m, jnp.max(sj, axis=1, keepdims=True))
    p_new = jnp.exp2(s_new - m)
    l = jnp.sum(p_new, axis=1, keepdims=True)
    acc = jnp.dot(p_new.astype(BF16), v_new, preferred_element_type=F32)
    for j in range(n_pages):
        pj = jnp.exp2(s_past[j] - m)
        l = l + jnp.sum(pj, axis=1, keepdims=True)
        acc = acc + jnp.dot(pj.astype(BF16), vp[j][...].astype(BF16), preferred_element_type=F32)
    o = acc / l
    outs = [_diff_finish(o[h * nq:h * nq + rpad], o[h * nq + rpad:(h + 1) * nq], lam, g_ref[...], lam_init)
            for h in range(HA)]
    o_ref[...] = jnp.concatenate(outs, axis=1)


def _attn_sample(q, k, v, cache_k, cache_v, layer, page_table, lamv, g, l_new, lam_init):
    t = q.shape[0]
    db = t // l_new
    n_pages = page_table.shape[1]
    page = cache_k.shape[2]
    rpad = 8

    def pad_rows(a):
        a = a.reshape(db, l_new, W_A).astype(F32)
        return jnp.concatenate([a, jnp.zeros((db, rpad - l_new, W_A), F32)], axis=1)

    def new_rows(a):
        return a.reshape(db, l_new * HA, DV_A).astype(F32)

    depth, n_phys = cache_k.shape[:2]
    cache_k = cache_k.reshape(depth, n_phys, page * HA, DV_A)
    cache_v = cache_v.reshape(depth, n_phys, page * HA, DV_A)
    q_spec = pl.BlockSpec((None, rpad, W_A), lambda b, pt: (b, 0, 0))
    new_spec = pl.BlockSpec((None, l_new * HA, DV_A), lambda b, pt: (b, 0, 0))
    page_specs = [pl.BlockSpec((None, None, page * HA, DV_A),
                               functools.partial(lambda b, pt, j: (layer, pt[b, j], 0, 0), j=j))
                  for j in range(n_pages)]
    kern = functools.partial(_attn_sample_kernel, n_pages=n_pages, l_new=l_new, lam_init=lam_init)
    out = pl.pallas_call(
        kern,
        grid_spec=pltpu.PrefetchScalarGridSpec(
            num_scalar_prefetch=1,
            grid=(db,),
            in_specs=[q_spec, new_spec, new_spec,
                      pl.BlockSpec((4, LANES), lambda b, pt: (0, 0)),
                      pl.BlockSpec((1, LANES), lambda b, pt: (0, 0))] + page_specs + page_specs,
            out_specs=q_spec,
        ),
        out_shape=jax.ShapeDtypeStruct((db, rpad, W_A), F32),
        compiler_params=_cparams(("arbitrary",)),
        name="attn_sample",
    )(page_table, pad_rows(q), new_rows(k), new_rows(v), lamv, g,
      *([cache_k] * n_pages), *([cache_v] * n_pages))
    return out[:, :l_new].reshape(t, W_A).astype(BF16)


def _log_sigmoid(x):
    return jnp.minimum(x, 0.0) - jnp.log1p(jnp.exp(-jnp.abs(x)))


def _gla_gate(br, wg_ref, bgate_ref):
    x = jnp.dot(br.astype(BF16), wg_ref[...], preferred_element_type=F32) + bgate_ref[...]
    return _log_sigmoid(x) / GATE_NORM


def _gla_intra(q_att, k_in, v, chunk):
    keep = _chunk_causal(q_att.shape[0], chunk)
    lane = _idiv(_iota2(q_att.shape, 1), DK_B)
    kb = k_in.astype(BF16)
    atts, vs = [], []
    for h in range(HB):
        qh = jnp.where(lane == h, q_att, 0.0).astype(BF16)
        a = lax.dot_general(qh, kb, (((1,), (1,)), ((), ())), preferred_element_type=F32)
        atts.append(jnp.where(keep, a, 0.0).astype(BF16))
        vs.append(jnp.where(lane == h, v, 0.0).astype(BF16))
    return jnp.dot(jnp.concatenate(atts, axis=1), jnp.concatenate(vs, axis=0), preferred_element_type=F32)


def _gla_finish(o, gate_in, gng_ref, bd):
    ms = _seg_sum(o * o, bd) * (1.0 / DV_B)
    o = o * lax.rsqrt(ms + RMS_EPS) * gng_ref[...]
    return o * (gate_in * (1.0 / (1.0 + jnp.exp(-gate_in))))


def _chunk_mlp(c_in, lng_ref, lnb_ref, ws_ref, bst, chunk, bd):
    n = c_in.shape[0]
    cu, cv = c_in[:, :W_C], c_in[:, W_C:]
    mu = _seg_sum(cv, bd) * (1.0 / DC)
    xc = cv - mu
    var = _seg_sum(xc * xc, bd) * (1.0 / DC)
    vn = xc * lax.rsqrt(var + LN_EPS) * lng_ref[...] + lnb_ref[...]
    keep = _chunk_causal(n, chunk)
    lane = _idiv(_iota2(vn.shape, 1), DC)
    ws, vs = [], []
    for g in range(HC):
        ws.append(jnp.where(keep, ws_ref[g], 0.0).astype(BF16))
        vs.append(jnp.where(lane == g, vn, 0.0).astype(BF16))
    mixed = jnp.dot(jnp.concatenate(ws, axis=1), jnp.concatenate(vs, axis=0), preferred_element_type=F32) + bst
    return cu * mixed, vn


def _mixer_prompt_kernel(g_ref, br_ref, c_ref, wg_ref, bgate_ref, gng_ref, lng_ref, lnb_ref, ws_ref, bst_ref,
                         o_ref, st_ref, st_sc, *, ts, chunk):
    t = pl.program_id(1)

    @pl.when(t == 0)
    def _():
        st_sc[...] = jnp.zeros(st_sc.shape, F32)

    grp = CMLP_CHUNK
    bd = _head_blockdiag(W_B)
    bd_bf = jnp.where(bd, 1.0, 0.0).astype(BF16)
    csum_sel = jnp.where(_chunk_causal(grp, chunk), 1.0, 0.0).astype(BF16)
    rows = _iota2((grp, W_B), 0)
    half = chunk // 2
    for gi in range(ts // grp):
        rs = slice(gi * grp, (gi + 1) * grp)
        g = g_ref[rs, :]
        gq = g[:, 0:W_B] * (DK_B ** -0.5)
        gk, gv, gg = g[:, W_B:2 * W_B], g[:, 2 * W_B:3 * W_B], g[:, 3 * W_B:4 * W_B]
        la = _gla_gate(br_ref[rs, :], wg_ref, bgate_ref)
        bcum = _dot_sel(csum_sel, la)
        mids, lasts = [], []
        for ci in range(grp // chunk):
            mids.append(jnp.broadcast_to(bcum[ci * chunk + half - 1:ci * chunk + half, :], (chunk, W_B)))
            lasts.append(jnp.broadcast_to(bcum[(ci + 1) * chunk - 1:(ci + 1) * chunk, :], (chunk, W_B)))
        bmid = jnp.concatenate(mids, axis=0)
        blast = jnp.concatenate(lasts, axis=0)
        q_att = gq * jnp.exp(bcum - bmid)
        k_in = gk * jnp.exp(bmid - bcum)
        k_end = gk * jnp.exp(blast - bcum)
        q_dec = (gq * jnp.exp(bcum)).astype(BF16)
        o = _gla_intra(q_att, k_in, gv, chunk)
        v_t = gv.T.astype(BF16)
        o_inter = []
        for ci in range(grp // chunk):
            cs = slice(ci * chunk, (ci + 1) * chunk)
            st = st_sc[...]
            o_inter.append(lax.dot_general(q_dec[cs], st.astype(BF16), (((1,), (1,)), ((), ())),
                                           preferred_element_type=F32))
            kem = jnp.where(_idiv(rows, chunk) == ci, k_end, 0.0).astype(BF16)
            upd = jnp.dot(v_t, kem, preferred_element_type=F32)
            dl = jnp.exp(blast[ci * chunk:ci * chunk + 1, :])
            st_sc[...] = st * dl + jnp.where(bd, upd, 0.0)
        o = o + jnp.concatenate(o_inter, axis=0)
        o_b = _gla_finish(o, gg, gng_ref, bd_bf)
        o_c, _ = _chunk_mlp(c_ref[rs, :], lng_ref, lnb_ref, ws_ref, bst_ref[...], CMLP_CHUNK, bd_bf)
        o_ref[rs, :] = jnp.concatenate([o_b, o_c], axis=1).astype(o_ref.dtype)

    @pl.when(t == pl.num_programs(1) - 1)
    def _():
        st_ref[...] = st_sc[...].T


def _mixer_prompt(g_in, br, c_in, prm, nbatch, ts=256):
    t = g_in.shape[0]
    nt = t // nbatch // ts
    row = lambda n: pl.BlockSpec((ts, n), lambda b, i: (b * nt + i, 0))
    full = lambda a: pl.BlockSpec(a.shape, lambda b, i: (0,) * a.ndim)
    kern = functools.partial(_mixer_prompt_kernel, ts=ts, chunk=GLA_CHUNK_PROMPT)
    return pl.pallas_call(
        kern,
        grid=(nbatch, nt),
        in_specs=[row(4 * W_B), row(LANES), row(2 * W_C)] + [full(a) for a in prm],
        out_specs=[row(W_B + W_C), pl.BlockSpec((None, W_B, W_B), lambda b, i: (b, 0, 0))],
        out_shape=[jax.ShapeDtypeStruct((t, W_B + W_C), BF16),
                   jax.ShapeDtypeStruct((nbatch, W_B, W_B), F32)],
        scratch_shapes=[pltpu.VMEM((W_B, W_B), F32)],
        compiler_params=_cparams(("parallel", "arbitrary")),
        name="mixer_prompt",
    )(g_in, br, c_in, *prm)


def _mixer_sample_kernel(g_ref, br_ref, c_ref, s0_ref, wg_ref, bgate_ref, gng_ref, lng_ref, lnb_ref, ws_ref,
                         bst_ref, o_ref, vn_ref, st_ref, *, l_new):
    n = g_ref.shape[0]
    bd = _head_blockdiag(W_B)
    bd_bf = jnp.where(bd, 1.0, 0.0).astype(BF16)
    r, c = _iota2((n, n), 0), _iota2((n, n), 1)
    csum_sel = jnp.where(_chunk_causal(n, l_new), 1.0, 0.0).astype(BF16)
    last_sel = jnp.where(_idiv(r, l_new) == _idiv(c, l_new), 1.0, 0.0).astype(BF16)
    g = g_ref[...]
    gq = g[:, 0:W_B] * (DK_B ** -0.5)
    gk, gv, gg = g[:, W_B:2 * W_B], g[:, 2 * W_B:3 * W_B], g[:, 3 * W_B:4 * W_B]
    la = _gla_gate(br_ref[...], wg_ref, bgate_ref)
    bcum = _dot_sel(csum_sel, la)
    blast = _dot_sel(last_sel, la)
    q_in = gq * jnp.exp(bcum)
    k_in = gk * jnp.exp(-bcum)
    k_end = gk * jnp.exp(blast - bcum)
    o = _gla_intra(q_in, k_in, gv, l_new)
    zrows = jnp.zeros((LANES - n, W_B), F32)
    ke_t = jnp.concatenate([k_end, zrows], axis=0).T
    bl_t = jnp.concatenate([blast, zrows], axis=0).T
    v_pad = jnp.concatenate([gv, zrows], axis=0).astype(BF16)
    rows = _iota2((n, W_B), 0)
    cols = _iota2((W_B, LANES), 1)
    zblk = jnp.zeros((DK_B, DV_B), F32)
    for s in range(n // l_new):
        s0 = jnp.concatenate(
            [jnp.concatenate([s0_ref[s, h] if g == h else zblk for g in range(HB)], axis=1) for h in range(HB)],
            axis=0)
        qs = jnp.where(_idiv(rows, l_new) == s, q_in, 0.0).astype(BF16)
        o = o + jnp.dot(qs, s0.astype(BF16), preferred_element_type=F32)
        kes = jnp.where(_idiv(cols, l_new) == s, ke_t, 0.0).astype(BF16)
        upd = jnp.dot(kes, v_pad, preferred_element_type=F32)
        dl = jnp.exp(bl_t[:, s * l_new:s * l_new + 1])
        fin = s0 * dl + upd
        for h in range(HB):
            st_ref[s, h] = fin[h * DK_B:(h + 1) * DK_B, h * DV_B:(h + 1) * DV_B]
    o_b = _gla_finish(o, gg, gng_ref, bd_bf)
    o_c, vn = _chunk_mlp(c_ref[...], lng_ref, lnb_ref, ws_ref, bst_ref[...], l_new, bd_bf)
    o_ref[...] = jnp.concatenate([o_b, o_c], axis=1).astype(o_ref.dtype)
    vn_ref[...] = vn


def _mixer_sample(g_in, br, c_in, s0bd, prm, l_new, ts=64):
    t = g_in.shape[0]
    ns = ts // l_new
    row = lambda n: pl.BlockSpec((ts, n), lambda i: (i, 0))
    full = lambda a: pl.BlockSpec(a.shape, lambda i: (0,) * a.ndim)
    st = pl.BlockSpec((ns, HB, DK_B, DV_B), lambda i: (i, 0, 0, 0))
    kern = functools.partial(_mixer_sample_kernel, l_new=l_new)
    return pl.pallas_call(
        kern,
        grid=(t // ts,),
        in_specs=[row(4 * W_B), row(LANES), row(2 * W_C), st] + [full(a) for a in prm],
        out_specs=[row(W_B + W_C), row(W_C), st],
        out_shape=[jax.ShapeDtypeStruct((t, W_B + W_C), BF16),
                   jax.ShapeDtypeStruct((t, W_C), F32),
                   jax.ShapeDtypeStruct(s0bd.shape, F32)],
        compiler_params=_cparams(("parallel",)),
        name="mixer_sample",
    )(g_in, br, c_in, s0bd, *prm)


def _outproj_kernel(oa_ref, obc_ref, x_ref, wo_ref, g_ref, b_ref, rw_ref, rb_ref, h_ref, lg_ref, *, alpha):
    y = jnp.dot(oa_ref[...], wo_ref[0:W_A, :], preferred_element_type=F32)
    y = y + jnp.dot(obc_ref[...], wo_ref[W_A:, :], preferred_element_type=F32)
    h = _ln_rows(alpha * x_ref[...] + y, g_ref[...], b_ref[...])
    h_ref[...] = h
    lg_ref[...] = jnp.dot(h, rw_ref[...], preferred_element_type=F32, precision=lax.Precision.HIGHEST) + rb_ref[...]


def _outproj(o_a, o_bc, x, wo, g, b, rw, rb, alpha):
    t = x.shape[0]
    tm = min(512, t)
    row = lambda n: pl.BlockSpec((tm, n), lambda i: (i, 0))
    full = lambda a: pl.BlockSpec(a.shape, lambda i: (0,) * a.ndim)
    return pl.pallas_call(
        functools.partial(_outproj_kernel, alpha=alpha),
        grid=(t // tm,),
        in_specs=[row(W_A), row(W_B + W_C), row(D_MODEL)] + [full(a) for a in (wo, g, b, rw, rb)],
        out_specs=[row(D_MODEL), row(LANES)],
        out_shape=[jax.ShapeDtypeStruct((t, D_MODEL), F32), jax.ShapeDtypeStruct((t, LANES), F32)],
        compiler_params=_cparams(("parallel",)),
        name="outproj",
    )(o_a, o_bc, x, wo, g, b, rw, rb)


def _moe_kernel(be_ref, nu_ref, x_ref, w1_ref, b1_ref, w2_ref, b2_ref, y_ref, w1b_sc, w2b_sc):
    i = pl.program_id(0)

    @pl.when((i == 0) | (be_ref[i] != be_ref[jnp.maximum(i - 1, 0)]))
    def _():
        w1b_sc[...] = w1_ref[...].astype(BF16)
        w2b_sc[...] = w2_ref[...].astype(BF16)

    @pl.when(i >= nu_ref[0])
    def _():
        y_ref[...] = jnp.zeros(y_ref.shape, F32)

    @pl.when(i < nu_ref[0])
    def _():
        xb = x_ref[...].astype(BF16)
        acc = None
        for c in range(D_FF // MOE_FF_CHUNK):
            gs = slice(c * MOE_FF_CHUNK, (c + 1) * MOE_FF_CHUNK)
            us = slice(D_FF + c * MOE_FF_CHUNK, D_FF + (c + 1) * MOE_FF_CHUNK)
            g = jnp.dot(xb, w1b_sc[:, gs], preferred_element_type=F32) + b1_ref[:, gs]
            u = jnp.dot(xb, w1b_sc[:, us], preferred_element_type=F32) + b1_ref[:, us]
            g = jnp.minimum(g, SWIGLU_LIMIT)
            u = jnp.clip(u, -SWIGLU_LIMIT, SWIGLU_LIMIT)
            act = (u + 1.0) * g * (1.0 / (1.0 + jnp.exp(-SWIGLU_ALPHA * g)))
            part = jnp.dot(act.astype(BF16), w2b_sc[gs, :], preferred_element_type=F32)
            acc = part if acc is None else acc + part
        y_ref[...] = acc + b2_ref[...]


def _moe_experts(x_pad, block_e, n_used, w1, b1, w2, b2, layer, bm):
    nb = x_pad.shape[0] // bm
    return pl.pallas_call(
        _moe_kernel,
        grid_spec=pltpu.PrefetchScalarGridSpec(
            num_scalar_prefetch=2,
            grid=(nb,),
            in_specs=[pl.BlockSpec((bm, D_MODEL), lambda i, be, nu: (i, 0)),
                      pl.BlockSpec((None, None, D_MODEL, 2 * D_FF), lambda i, be, nu: (layer, be[i], 0, 0)),
                      pl.BlockSpec((None, None, 1, 2 * D_FF), lambda i, be, nu: (layer, be[i], 0, 0)),
                      pl.BlockSpec((None, None, D_FF, D_MODEL), lambda i, be, nu: (layer, be[i], 0, 0)),
                      pl.BlockSpec((None, None, 1, D_MODEL), lambda i, be, nu: (layer, be[i], 0, 0))],
            out_specs=pl.BlockSpec((bm, D_MODEL), lambda i, be, nu: (i, 0)),
            scratch_shapes=[pltpu.VMEM((D_MODEL, 2 * D_FF), BF16), pltpu.VMEM((D_FF, D_MODEL), BF16)],
        ),
        out_shape=jax.ShapeDtypeStruct((nb * bm, D_MODEL), F32),
        compiler_params=_cparams(("arbitrary",)),
        name="moe_experts",
    )(block_e, n_used, x_pad, w1, b1, w2, b2)


def _route(logits, bm):
    t = logits.shape[0]
    m = t * TOP_K
    top_v, top_i = lax.top_k(logits[:, :N_EXPERTS], TOP_K)
    gate = jax.nn.softmax(top_v, axis=-1)
    flat_e = top_i.reshape(m)
    onehot = (flat_e[:, None] == jnp.arange(N_EXPERTS)[None, :]).astype(jnp.int32)
    csum = jnp.cumsum(onehot, axis=0)
    rank = jnp.take_along_axis(csum, flat_e[:, None], axis=1)[:, 0] - 1
    counts = csum[-1]
    padded = ((counts + bm - 1) // bm) * bm
    pad_end = jnp.cumsum(padded)
    pad_start = pad_end - padded
    dest = pad_start[flat_e] + rank
    nb = -(-m // bm) + N_EXPERTS
    block_e = jnp.sum((pad_end[None, :] <= (jnp.arange(nb) * bm)[:, None]).astype(jnp.int32), axis=1)
    block_e = jnp.minimum(block_e, N_EXPERTS - 1)
    n_used = (pad_end[-1] // bm).astype(jnp.int32).reshape(1)
    block_e = jnp.where(jnp.arange(nb) < n_used[0], block_e, block_e[jnp.maximum(n_used[0] - 1, 0)])
    src_tok = jnp.zeros((nb * bm,), jnp.int32).at[dest].set(jnp.arange(m, dtype=jnp.int32) // TOP_K)
    return gate, dest.reshape(t, TOP_K), src_tok, block_e.astype(jnp.int32), n_used


def _ln2_kernel(h_ref, gate_ref, *rest, alpha):
    y_refs, (g_ref, b_ref, o_ref) = rest[:TOP_K], rest[TOP_K:]
    gate = gate_ref[...]
    x = alpha * h_ref[...]
    for k in range(TOP_K):
        x = x + gate[:, k:k + 1] * y_refs[k][...]
    o_ref[...] = _ln_rows(x, g_ref[...], b_ref[...])


def _ln2(h, gate, yg, row0, g, b, alpha):
    t = h.shape[0]
    t_all = yg.shape[0] // TOP_K
    tm = min(512, t)
    assert row0 % tm == 0 and t_all % tm == 0
    off = row0 // tm
    nt_all = t_all // tm
    y_specs = [pl.BlockSpec((tm, D_MODEL), functools.partial(lambda i, k: (k * nt_all + off + i, 0), k=k))
               for k in range(TOP_K)]
    return pl.pallas_call(
        functools.partial(_ln2_kernel, alpha=alpha),
        grid=(t // tm,),
        in_specs=[pl.BlockSpec((tm, D_MODEL), lambda i: (i, 0)),
                  pl.BlockSpec((tm, LANES), lambda i: (i + off, 0))] + y_specs +
                 [pl.BlockSpec((1, D_MODEL), lambda i: (0, 0)),
                  pl.BlockSpec((1, D_MODEL), lambda i: (0, 0))],
        out_specs=pl.BlockSpec((tm, D_MODEL), lambda i: (i, 0)),
        out_shape=jax.ShapeDtypeStruct((t, D_MODEL), F32),
        compiler_params=_cparams(("parallel",)),
        name="ln2",
    )(h, gate, *([yg] * TOP_K), g, b)


SC_CORES, SC_SUBCORES = 2, 16
SC_CHUNK = 64


def _sc_gather(table, idx):
    b, d = idx.shape[0], table.shape[1]
    workers = SC_CORES * SC_SUBCORES
    per_w = b // workers
    assert per_w * workers == b and per_w % SC_CHUNK == 0
    mesh = plsc.VectorSubcoreMesh(core_axis_name="c", subcore_axis_name="s")

    @functools.partial(
        pl.kernel, mesh=mesh, out_type=jax.ShapeDtypeStruct((b, d), table.dtype),
        scratch_types=[pltpu.VMEM((SC_CHUNK,), jnp.int32), pltpu.VMEM((SC_CHUNK, d), table.dtype),
                       pltpu.SemaphoreType.DMA],
        name="sc_gather")
    def gather(table_hbm, idx_hbm, out_hbm, idx_v, rows_v, sem):
        wid = lax.axis_index("s") * SC_CORES + lax.axis_index("c")

        @pl.loop(0, per_w // SC_CHUNK)
        def _(i):
            base = wid * per_w + i * SC_CHUNK
            pltpu.sync_copy(idx_hbm.at[pl.ds(base, SC_CHUNK)], idx_v)
            pltpu.async_copy(table_hbm.at[idx_v], rows_v, sem).wait()
            pltpu.sync_copy(rows_v, out_hbm.at[pl.ds(base, SC_CHUNK)])

    return gather(table, idx)


def _rope_tables(pos):
    half = ROT_DIM // 2
    inv_freq = ROPE_THETA ** (-jnp.arange(0, ROT_DIM, 2, dtype=F32) / ROT_DIM)
    ang = pos.astype(F32)[:, None] * inv_freq[None, :]
    cos, sin = jnp.cos(ang), jnp.sin(ang)
    m = np.arange(LANES) % DQK_A
    idx = m % half
    cos_l = jnp.where(m < ROT_DIM, cos[:, idx], 1.0)
    sa = jnp.where(m < half, -sin[:, idx], 0.0)
    sb = jnp.where((m >= half) & (m < ROT_DIM), sin[:, idx], 0.0)
    return cos_l, sa, sb


def _prep_w_in(w):
    r0 = COL_C
    r1 = r0 + GATE_RANK
    pad = jnp.zeros((w.shape[0], LANES - GATE_RANK), w.dtype)
    return jnp.concatenate([w[:, :r0], w[:, r1:], w[:, r0:r1], pad], axis=1).astype(BF16)


def _tile_lanes(v, reps):
    return jnp.tile(v.reshape(1, -1), (1, reps)).astype(F32)


def _blockdiag_states(s):
    n = s.shape[0]
    eye = jnp.eye(HB, dtype=s.dtype)
    return jnp.einsum('nhde,hg->nhdge', s, eye).reshape(n, HB * DK_B, HB * DV_B)


def _diag_states(sbd):
    n = sbd.shape[0]
    s = sbd.reshape(n, HB, DK_B, HB, DV_B)
    return jnp.stack([s[:, h, :, h, :] for h in range(HB)], axis=1)


def kernel(x_prompt, x_sample, cache_k, cache_v, page_table, state_gla, w_in, lam_q1, lam_k1, lam_q2, lam_k2, attn_norm_g, gla_w_gate, gla_b_gate, gla_norm_g, cmlp_ln_g, cmlp_ln_b, cmlp_ws, cmlp_bs, w_o, ln1_g, ln1_b, router_w, router_b, exp_w1, exp_b1, exp_w2, exp_b2, ln2_g, ln2_b):
    depth = w_in.shape[0]
    bp, s_len, _ = x_prompt.shape
    db, l_new, _ = x_sample.shape
    n_phys, page = cache_k.shape[1], cache_k.shape[2]
    past_len = page_table.shape[1] * page
    alpha = (2 * depth) ** 0.25
    tp, ts = bp * s_len, db * l_new
    bm = MOE_BLOCK

    tabs_p = _rope_tables(jnp.arange(s_len))
    tabs_s = _rope_tables(past_len + (jnp.arange(ts) % l_new))
    page_table = page_table.astype(jnp.int32)

    hp = x_prompt.reshape(tp, D_MODEL)
    hs = x_sample.reshape(ts, D_MODEL)
    outs = {k: [] for k in ("gp", "gs", "cs")}
    kp_all, vp_all = (jnp.zeros((depth * tp, HA, DV_A), F32) for _ in range(2))
    ks_all, vs_all = (jnp.zeros((depth * ts, HA, DV_A), F32) for _ in range(2))
    for l in range(depth):
        lam_init = 0.8 - 0.6 * math.exp(-0.3 * l)
        w = _prep_w_in(w_in[l])
        lamv = jnp.pad(jnp.stack([lam_q1[l], lam_k1[l], lam_q2[l], lam_k2[l]]).astype(F32),
                       ((0, 0), (0, LANES - DQK_A)))
        g_attn = attn_norm_g[l].reshape(1, DV_A).astype(F32)
        wg = jnp.pad(gla_w_gate[l], ((0, LANES - GATE_RANK), (0, 0))).astype(BF16)
        wo = w_o[l].astype(BF16)
        rw = jnp.pad(router_w[l].astype(F32), ((0, 0), (0, LANES - N_EXPERTS)))
        rb = jnp.pad(router_b[l].astype(F32), (0, LANES - N_EXPERTS), constant_values=NEG_INF).reshape(1, LANES)
        ln1 = (ln1_g[l].reshape(1, D_MODEL), ln1_b[l].reshape(1, D_MODEL))

        def mixer_params(lc, n_rows):
            reps = n_rows // lc
            ws = jnp.tile(cmlp_ws[l][:, :lc, :lc], (1, reps, reps))
            bst = jnp.tile(jnp.repeat(cmlp_bs[l][:, :lc].T, DC, axis=1), (reps, 1))
            return (wg, gla_b_gate[l].reshape(1, W_B), _tile_lanes(gla_norm_g[l], HB),
                    _tile_lanes(cmlp_ln_g[l], HC), _tile_lanes(cmlp_ln_b[l], HC), ws, bst)

        q, kp_all, vp_all, kb, vb, g_in, c_in, br = _inproj(hp, w, tabs_p, l, kp_all, vp_all)
        o_a = _attn_prompt(q, kb, vb, lamv, g_attn, bp, lam_init)
        o_bc, st_p = _mixer_prompt(g_in, br, c_in, mixer_params(CMLP_CHUNK, CMLP_CHUNK), bp)
        hp1, lg_p = _outproj(o_a, o_bc, hp, wo, *ln1, rw, rb, alpha)
        outs["gp"].append(_diag_states(st_p))

        q, ks_all, vs_all, kb, vb, g_in, c_in, br = _inproj(hs, w, tabs_s, l, ks_all, vs_all)
        o_a = _attn_sample(q, kb, vb, cache_k, cache_v, l, page_table, lamv, g_attn, l_new, lam_init)
        rows_s = min(64, ts)
        lc = min(l_new, CMLP_CHUNK)
        o_bc, vn, st_s = _mixer_sample(g_in, br, c_in, state_gla[l].astype(F32),
                                       mixer_params(lc, rows_s), l_new, rows_s)
        hs1, lg_s = _outproj(o_a, o_bc, hs, wo, *ln1, rw, rb, alpha)
        outs["gs"].append(st_s)
        outs["cs"].append(vn.reshape(db, l_new, W_C))

        gate, dest, src_tok, block_e, n_used = _route(jnp.concatenate([lg_p, lg_s], axis=0), bm)
        x_pad = _sc_gather(jnp.concatenate([hp1, hs1], axis=0), src_tok)
        y_pad = _moe_experts(x_pad, block_e, n_used, exp_w1, exp_b1.reshape(depth, N_EXPERTS, 1, -1),
                             exp_w2, exp_b2.reshape(depth, N_EXPERTS, 1, -1), l, bm)
        yg = _sc_gather(y_pad, dest.T.reshape(-1))
        gate = jnp.pad(gate, ((0, 0), (0, LANES - TOP_K)))
        hp = _ln2(hp1, gate, yg, 0, ln2_g[l].reshape(1, -1), ln2_b[l].reshape(1, -1), alpha)
        hs = _ln2(hs1, gate, yg, tp, ln2_g[l].reshape(1, -1), ln2_b[l].reshape(1, -1), alpha)

    return (hp.reshape(bp, s_len, D_MODEL), hs.reshape(db, l_new, D_MODEL),
            kp_all.reshape(depth, bp, s_len, HA, DV_A), vp_all.reshape(depth, bp, s_len, HA, DV_A),
            jnp.stack(outs["gp"]),
            ks_all.reshape(depth, db, l_new, HA, DV_A), vs_all.reshape(depth, db, l_new, HA, DV_A),
            jnp.stack(outs["gs"]), jnp.stack(outs["cs"]))
```

```python
import functools
import math

import numpy as np
import jax
import jax.numpy as jnp
from jax import lax
from jax.experimental import pallas as pl
from jax.experimental.pallas import tpu as pltpu
from jax.experimental.pallas import tpu_sc as plsc

F32, BF16 = jnp.float32, jnp.bfloat16
LANES = 128
VMEM_LIMIT = 48 * 1024 * 1024

D_MODEL = 1024
HA, DQK_A, DV_A = 4, 64, 128
ROT_DIM = DQK_A // 4
ROPE_THETA = 500000.0
HB, DK_B, DV_B = 4, 64, 64
GATE_RANK = 16
GATE_NORM = 16.0
HC, DC = 4, 64
CMLP_CHUNK = 128
N_EXPERTS = 32
TOP_K = 4
D_FF = D_MODEL
SWIGLU_LIMIT = 7.0
SWIGLU_ALPHA = 1.702
LN_EPS = 1e-5
RMS_EPS = 1e-6
NEG_INF = -1e30
LOG2E = math.log2(math.e)

W_A = HA * 2 * DQK_A
W_B = HB * DK_B
W_C = HC * DC
COL_G = 3 * W_A
COL_C = COL_G + 4 * W_B
COL_R = COL_C + 2 * W_C
COL_END = COL_R + LANES
GLA_CHUNK_PROMPT = 32
ATTN_BQ, ATTN_BK = 512, 512
D_PACK = D_MODEL // 2
MOE_BLOCK = 256
MOE_FF_CHUNK = 512


def _cparams(sem):
    return pltpu.CompilerParams(dimension_semantics=sem, vmem_limit_bytes=VMEM_LIMIT)


def _split3(x):
    hi = x.astype(BF16)
    r = x - hi.astype(F32)
    mid = r.astype(BF16)
    lo = (r - mid.astype(F32)).astype(BF16)
    return hi, mid, lo


def _dot_sel(sel_bf16, x):
    acc = None
    for p in _split3(x):
        d = jnp.dot(sel_bf16, p, preferred_element_type=F32)
        acc = d if acc is None else acc + d
    return acc


def _seg_sum(x, bd_bf16):
    acc = None
    for p in _split3(x):
        d = jnp.dot(p, bd_bf16, preferred_element_type=F32)
        acc = d if acc is None else acc + d
    return acc


def _pack_bf16_pairs(x):
    u = lax.bitcast_convert_type(x, jnp.uint32)
    r = u + (jnp.uint32(0x7FFF) + ((u >> 16) & jnp.uint32(1)))
    w = x.shape[1] // 2
    word = (r[:, :w] & jnp.uint32(0xFFFF0000)) | (r[:, w:] >> 16)
    return lax.bitcast_convert_type(word, F32)


def _unpack_bf16_pairs(words):
    u = lax.bitcast_convert_type(words, jnp.uint32)
    hi = lax.bitcast_convert_type(u & jnp.uint32(0xFFFF0000), F32)
    lo = lax.bitcast_convert_type(u << 16, F32)
    return jnp.concatenate([hi, lo], axis=1)


def _iota2(shape, dim):
    return lax.broadcasted_iota(jnp.int32, shape, dim)


def _idiv(x, n):
    shift = n.bit_length() - 1
    assert n == 1 << shift
    return x >> shift


def _head_blockdiag(n):
    r, c = _iota2((n, n), 0), _iota2((n, n), 1)
    return _idiv(r, DK_B) == _idiv(c, DK_B)


def _chunk_causal(n, chunk):
    r, c = _iota2((n, n), 0), _iota2((n, n), 1)
    return (_idiv(r, chunk) == _idiv(c, chunk)) & (c <= r)


def _ln_rows(x, g, b):
    mu = jnp.mean(x, axis=-1, keepdims=True)
    xc = x - mu
    var = jnp.mean(xc * xc, axis=-1, keepdims=True)
    return xc * lax.rsqrt(var + LN_EPS) * g + b


def _inproj_kernel(x_ref, w_ref, cos_ref, sa_ref, sb_ref, k_all_ref, v_all_ref,
                   q_ref, k_ref, v_ref, kb_ref, vb_ref, g_ref, c_ref, br_ref):
    del k_all_ref, v_all_ref
    xb = x_ref[...].astype(BF16)

    def proj(a, b):
        return jnp.dot(xb, w_ref[:, a:b], preferred_element_type=F32)

    cos, sa, sb = cos_ref[...], sa_ref[...], sb_ref[...]

    def rope(z):
        outs = []
        for i in range(z.shape[1] // LANES):
            zi = z[:, i * LANES:(i + 1) * LANES]
            outs.append(zi * cos + pltpu.roll(zi, LANES - ROT_DIM // 2, 1) * sa
                        + pltpu.roll(zi, ROT_DIM // 2, 1) * sb)
        return jnp.concatenate(outs, axis=1)

    q_ref[...] = (rope(proj(0, W_A)) * (DQK_A ** -0.5 * LOG2E)).astype(BF16)
    k = rope(proj(W_A, 2 * W_A))
    kb_ref[...] = k.astype(BF16)
    v = proj(2 * W_A, 3 * W_A)
    vb_ref[...] = v.astype(BF16)
    for h in range(HA):
        k_ref[:, h, :] = k[:, h * LANES:(h + 1) * LANES]
        v_ref[:, h, :] = v[:, h * LANES:(h + 1) * LANES]
    g_ref[...] = proj(COL_G, COL_C)
    c_ref[...] = proj(COL_C, COL_R)
    br_ref[...] = proj(COL_R, COL_END)


def _inproj(x, w, tabs, layer, k_all, v_all):
    t = x.shape[0]
    tm = min(512, t)
    nt = t // tm
    cos, sa, sb = tabs
    ntab = cos.shape[0] // tm
    row = lambda n: pl.BlockSpec((tm, n), lambda i: (i, 0))
    tab = pl.BlockSpec((tm, LANES), lambda i: (i % ntab, 0))
    heads = pl.BlockSpec((tm, HA, DV_A), lambda i: (layer * nt + i, 0, 0))
    anywhere = pl.BlockSpec(memory_space=pl.ANY)
    shapes = [((W_A,), BF16), None, None, ((W_A,), BF16), ((W_A,), BF16),
              ((4 * W_B,), F32), ((2 * W_C,), F32), ((LANES,), F32)]
    return pl.pallas_call(
        _inproj_kernel,
        grid=(nt,),
        in_specs=[row(D_MODEL), pl.BlockSpec((D_MODEL, COL_END), lambda i: (0, 0)), tab, tab, tab,
                  anywhere, anywhere],
        out_specs=[heads if s is None else row(s[0][0]) for s in shapes],
        out_shape=[jax.ShapeDtypeStruct(k_all.shape, F32) if s is None else jax.ShapeDtypeStruct((t,) + s[0], s[1])
                   for s in shapes],
        input_output_aliases={5: 1, 6: 2},
        compiler_params=_cparams(("parallel",)),
        name="inproj",
    )(x, w, cos, sa, sb, k_all, v_all)


def _diff_lambda(lamv, lam_init):
    a = jnp.sum(lamv[0:1] * lamv[1:2], axis=1, keepdims=True)
    b = jnp.sum(lamv[2:3] * lamv[3:4], axis=1, keepdims=True)
    return jnp.exp(a) - jnp.exp(b) + lam_init


def _diff_finish(o1, o2, lam, g, lam_init):
    o = o1 - lam * o2
    ms = jnp.mean(o * o, axis=-1, keepdims=True)
    return o * lax.rsqrt(ms + RMS_EPS) * g * (1.0 - lam_init)


def _split_maps(q):
    lane = _iota2(q.shape, 1)
    zero = jnp.zeros_like(q)
    return jnp.concatenate([jnp.where(lane < DQK_A, q, zero), jnp.where(lane >= DQK_A, q, zero)], axis=0)


def _attn_prompt_kernel(q_ref, k_ref, v_ref, lamv_ref, g_ref, o_ref, m_sc, l_sc, acc_sc, *, bq, bk, lam_init):
    qi = pl.program_id(2)
    qq = _split_maps(q_ref[...])
    m_sc[...] = jnp.full(m_sc.shape, NEG_INF, F32)
    l_sc[...] = jnp.zeros(l_sc.shape, F32)
    acc_sc[...] = jnp.zeros(acc_sc.shape, F32)

    def step(j, masked):
        start = pl.multiple_of(j * bk, bk)
        k = k_ref[pl.ds(start, bk), :]
        v = v_ref[pl.ds(start, bk), :]
        s = lax.dot_general(qq, k, (((1,), (1,)), ((), ())), preferred_element_type=F32)
        if masked:
            r = (_iota2(s.shape, 0) & (bq - 1)) + qi * bq
            c = _iota2(s.shape, 1) + j * bk
            s = jnp.where(c <= r, s, NEG_INF)
        m_prev = m_sc[...]
        m_new = jnp.maximum(m_prev, jnp.max(s, axis=1, keepdims=True))
        alpha = jnp.exp2(m_prev - m_new)
        p = jnp.exp2(s - jnp.tile(m_new, (1, bk // LANES)))
        l_sc[...] = alpha * l_sc[...] + jnp.sum(p, axis=1, keepdims=True)
        acc_sc[...] = alpha * acc_sc[...] + jnp.dot(p.astype(BF16), v, preferred_element_type=F32)
        m_sc[...] = m_new

    def body(j, carry):
        step(j, False)
        return carry

    n_full = (qi * bq) // bk
    lax.fori_loop(0, n_full, body, 0)
    for d in range(max(bq // bk, 1)):
        step(n_full + d, True)

    o = acc_sc[...] / l_sc[...]
    lam = _diff_lambda(lamv_ref[...], lam_init)
    o_ref[...] = _diff_finish(o[:bq], o[bq:], lam, g_ref[...], lam_init).astype(o_ref.dtype)


def _attn_prompt(q, kb, vb, lamv, g, nbatch, lam_init):
    t = q.shape[0]
    s = t // nbatch
    bq, bk = min(ATTN_BQ, s), min(ATTN_BK, s)
    assert max(bq, bk) % min(bq, bk) == 0
    blk = bq
    nq = s // blk
    kern = functools.partial(_attn_prompt_kernel, bq=bq, bk=bk, lam_init=lam_init)
    return pl.pallas_call(
        kern,
        grid=(nbatch, HA, nq),
        in_specs=[pl.BlockSpec((blk, LANES), lambda b, h, i: (b * nq + i, h)),
                  pl.BlockSpec((s, LANES), lambda b, h, i: (b, h)),
                  pl.BlockSpec((s, LANES), lambda b, h, i: (b, h)),
                  pl.BlockSpec((4, LANES), lambda b, h, i: (0, 0)),
                  pl.BlockSpec((1, LANES), lambda b, h, i: (0, 0))],
        out_specs=pl.BlockSpec((blk, LANES), lambda b, h, i: (b * nq + i, h)),
        out_shape=jax.ShapeDtypeStruct((t, W_A), BF16),
        scratch_shapes=[pltpu.VMEM((2 * blk, LANES), F32)] * 3,
        compiler_params=_cparams(("parallel", "parallel", "arbitrary")),
        name="attn_prompt",
    )(q, kb, vb, lamv, g)


def _attn_sample_kernel(pt_ref, q_ref, kn_ref, vn_ref, lamv_ref, g_ref, *rest, n_pages, l_new, lam_init):
    del pt_ref
    kp = rest[:n_pages]
    vp = rest[n_pages:2 * n_pages]
    o_ref = rest[2 * n_pages]
    rows_pg = kp[0].shape[0]
    rpad = q_ref.shape[0]
    nq = 2 * rpad
    lam = _diff_lambda(lamv_ref[...], lam_init)
    q = q_ref[...]
    qq = jnp.concatenate([_split_maps(q[:, h * LANES:(h + 1) * LANES]) for h in range(HA)],
                         axis=0).astype(BF16)
    r = _iota2((HA * nq, rows_pg), 0)
    c = _iota2((HA * nq, rows_pg), 1)
    head_ok = (c & (HA - 1)) == _idiv(r, nq)
    new_ok = head_ok & (_idiv(c, HA) <= (r & (rpad - 1))) & (c < l_new * HA)
    nt = (((1,), (1,)), ((), ()))
    zpad = jnp.zeros((rows_pg - kn_ref.shape[0], LANES), F32)
    k_new = jnp.concatenate([kn_ref[...], zpad], axis=0).astype(BF16)
    v_new = jnp.concatenate([vn_ref[...], zpad], axis=0).astype(BF16)
    s_new = jnp.where(new_ok, lax.dot_general(qq, k_new, nt, preferred_element_type=F32), NEG_INF)
    s_past = [jnp.where(head_ok, lax.dot_general(qq, kp[j][...].astype(BF16), nt, preferred_element_type=F32),
                        NEG_INF) for j in range(n_pages)]
    m = jnp.max(s_new, axis=1, keepdims=True)
    for sj in s_past:
        m = jnp.maximum(m, jnp.max(sj, axis=1, keepdims=True))
    p_new = jnp.exp2(s_new - m)
    l = jnp.sum(p_new, axis=1, keepdims=True)
    acc = jnp.dot(p_new.astype(BF16), v_new, preferred_element_type=F32)
    for j in range(n_pages):
        pj = jnp.exp2(s_past[j] - m)
        l = l + jnp.sum(pj, axis=1, keepdims=True)
        acc = acc + jnp.dot(pj.astype(BF16), vp[j][...].astype(BF16), preferred_element_type=F32)
    o = acc / l
    outs = [_diff_finish(o[h * nq:h * nq + rpad], o[h * nq + rpad:(h + 1) * nq], lam, g_ref[...], lam_init)
            for h in range(HA)]
    o_ref[...] = jnp.concatenate(outs, axis=1)


def _attn_sample(q, k, v, cache_k, cache_v, layer, page_table, lamv, g, l_new, lam_init):
    t = q.shape[0]
    db = t // l_new
    n_pages = page_table.shape[1]
    page = cache_k.shape[2]
    rpad = 8

    def pad_rows(a):
        a = a.reshape(db, l_new, W_A).astype(F32)
        return jnp.concatenate([a, jnp.zeros((db, rpad - l_new, W_A), F32)], axis=1)

    def new_rows(a):
        return a.reshape(db, l_new * HA, DV_A).astype(F32)

    depth, n_phys = cache_k.shape[:2]
    cache_k = cache_k.reshape(depth, n_phys, page * HA, DV_A)
    cache_v = cache_v.reshape(depth, n_phys, page * HA, DV_A)
    q_spec = pl.BlockSpec((None, rpad, W_A), lambda b, pt: (b, 0, 0))
    new_spec = pl.BlockSpec((None, l_new * HA, DV_A), lambda b, pt: (b, 0, 0))
    page_specs = [pl.BlockSpec((None, None, page * HA, DV_A),
                               functools.partial(lambda b, pt, j: (layer, pt[b, j], 0, 0), j=j))
                  for j in range(n_pages)]
    kern = functools.partial(_attn_sample_kernel, n_pages=n_pages, l_new=l_new, lam_init=lam_init)
    out = pl.pallas_call(
        kern,
        grid_spec=pltpu.PrefetchScalarGridSpec(
            num_scalar_prefetch=1,
            grid=(db,),
            in_specs=[q_spec, new_spec, new_spec,
                      pl.BlockSpec((4, LANES), lambda b, pt: (0, 0)),
                      pl.BlockSpec((1, LANES), lambda b, pt: (0, 0))] + page_specs + page_specs,
            out_specs=q_spec,
        ),
        out_shape=jax.ShapeDtypeStruct((db, rpad, W_A), F32),
        compiler_params=_cparams(("arbitrary",)),
        name="attn_sample",
    )(page_table, pad_rows(q), new_rows(k), new_rows(v), lamv, g,
      *([cache_k] * n_pages), *([cache_v] * n_pages))
    return out[:, :l_new].reshape(t, W_A).astype(BF16)


def _log_sigmoid(x):
    return jnp.minimum(x, 0.0) - jnp.log1p(jnp.exp(-jnp.abs(x)))


def _gla_gate(br, wg_ref, bgate_ref):
    x = jnp.dot(br.astype(BF16), wg_ref[...], preferred_element_type=F32) + bgate_ref[...]
    return _log_sigmoid(x) / GATE_NORM


def _gla_intra(q_att, k_in, v, chunk):
    keep = _chunk_causal(q_att.shape[0], chunk)
    lane = _idiv(_iota2(q_att.shape, 1), DK_B)
    kb = k_in.astype(BF16)
    atts, vs = [], []
    for h in range(HB):
        qh = jnp.where(lane == h, q_att, 0.0).astype(BF16)
        a = lax.dot_general(qh, kb, (((1,), (1,)), ((), ())), preferred_element_type=F32)
        atts.append(jnp.where(keep, a, 0.0).astype(BF16))
        vs.append(jnp.where(lane == h, v, 0.0).astype(BF16))
    return jnp.dot(jnp.concatenate(atts, axis=1), jnp.concatenate(vs, axis=0), preferred_element_type=F32)


def _gla_finish(o, gate_in, gng_ref, bd):
    ms = _seg_sum(o * o, bd) * (1.0 / DV_B)
    o = o * lax.rsqrt(ms + RMS_EPS) * gng_ref[...]
    return o * (gate_in * (1.0 / (1.0 + jnp.exp(-gate_in))))


def _chunk_mlp(c_in, lng_ref, lnb_ref, ws_ref, bst, chunk, bd):
    n = c_in.shape[0]
    cu, cv = c_in[:, :W_C], c_in[:, W_C:]
    mu = _seg_sum(cv, bd) * (1.0 / DC)
    xc = cv - mu
    var = _seg_sum(xc * xc, bd) * (1.0 / DC)
    vn = xc * lax.rsqrt(var + LN_EPS) * lng_ref[...] + lnb_ref[...]
    keep = _chunk_causal(n, chunk)
    lane = _idiv(_iota2(vn.shape, 1), DC)
    ws, vs = [], []
    for g in range(HC):
        ws.append(jnp.where(keep, ws_ref[g], 0.0).astype(BF16))
        vs.append(jnp.where(lane == g, vn, 0.0).astype(BF16))
    mixed = jnp.dot(jnp.concatenate(ws, axis=1), jnp.concatenate(vs, axis=0), preferred_element_type=F32) + bst
    return cu * mixed, vn


def _mixer_prompt_kernel(g_ref, br_ref, c_ref, wg_ref, bgate_ref, gng_ref, lng_ref, lnb_ref, ws_ref, bst_ref,
                         o_ref, st_ref, st_sc, *, ts, chunk):
    t = pl.program_id(1)

    @pl.when(t == 0)
    def _():
        st_sc[...] = jnp.zeros(st_sc.shape, F32)

    grp = CMLP_CHUNK
    bd = _head_blockdiag(W_B)
    bd_bf = jnp.where(bd, 1.0, 0.0).astype(BF16)
    csum_sel = jnp.where(_chunk_causal(grp, chunk), 1.0, 0.0).astype(BF16)
    rows = _iota2((grp, W_B), 0)
    half = chunk // 2
    for gi in range(ts // grp):
        rs = slice(gi * grp, (gi + 1) * grp)
        g = g_ref[rs, :]
        gq = g[:, 0:W_B] * (DK_B ** -0.5)
        gk, gv, gg = g[:, W_B:2 * W_B], g[:, 2 * W_B:3 * W_B], g[:, 3 * W_B:4 * W_B]
        la = _gla_gate(br_ref[rs, :], wg_ref, bgate_ref)
        bcum = _dot_sel(csum_sel, la)
        mids, lasts = [], []
        for ci in range(grp // chunk):
            mids.append(jnp.broadcast_to(bcum[ci * chunk + half - 1:ci * chunk + half, :], (chunk, W_B)))
            lasts.append(jnp.broadcast_to(bcum[(ci + 1) * chunk - 1:(ci + 1) * chunk, :], (chunk, W_B)))
        bmid = jnp.concatenate(mids, axis=0)
        blast = jnp.concatenate(lasts, axis=0)
        q_att = gq * jnp.exp(bcum - bmid)
        k_in = gk * jnp.exp(bmid - bcum)
        k_end = gk * jnp.exp(blast - bcum)
        q_dec = (gq * jnp.exp(bcum)).astype(BF16)
        o = _gla_intra(q_att, k_in, gv, chunk)
        v_t = gv.T.astype(BF16)
        o_inter = []
        for ci in range(grp // chunk):
            cs = slice(ci * chunk, (ci + 1) * chunk)
            st = st_sc[...]
            o_inter.append(lax.dot_general(q_dec[cs], st.astype(BF16), (((1,), (1,)), ((), ())),
                                           preferred_element_type=F32))
            kem = jnp.where(_idiv(rows, chunk) == ci, k_end, 0.0).astype(BF16)
            upd = jnp.dot(v_t, kem, preferred_element_type=F32)
            dl = jnp.exp(blast[ci * chunk:ci * chunk + 1, :])
            st_sc[...] = st * dl + jnp.where(bd, upd, 0.0)
        o = o + jnp.concatenate(o_inter, axis=0)
        o_b = _gla_finish(o, gg, gng_ref, bd_bf)
        o_c, _ = _chunk_mlp(c_ref[rs, :], lng_ref, lnb_ref, ws_ref, bst_ref[...], CMLP_CHUNK, bd_bf)
        o_ref[rs, :] = jnp.concatenate([o_b, o_c], axis=1).astype(o_ref.dtype)

    @pl.when(t == pl.num_programs(1) - 1)
    def _():
        st_ref[...] = st_sc[...].T


def _mixer_prompt(g_in, br, c_in, prm, nbatch, ts=256):
    t = g_in.shape[0]
    nt = t // nbatch // ts
    row = lambda n: pl.BlockSpec((ts, n), lambda b, i: (b * nt + i, 0))
    full = lambda a: pl.BlockSpec(a.shape, lambda b, i: (0,) * a.ndim)
    kern = functools.partial(_mixer_prompt_kernel, ts=ts, chunk=GLA_CHUNK_PROMPT)
    return pl.pallas_call(
        kern,
        grid=(nbatch, nt),
        in_specs=[row(4 * W_B), row(LANES), row(2 * W_C)] + [full(a) for a in prm],
        out_specs=[row(W_B + W_C), pl.BlockSpec((None, W_B, W_B), lambda b, i: (b, 0, 0))],
        out_shape=[jax.ShapeDtypeStruct((t, W_B + W_C), BF16),
                   jax.ShapeDtypeStruct((nbatch, W_B, W_B), F32)],
        scratch_shapes=[pltpu.VMEM((W_B, W_B), F32)],
        compiler_params=_cparams(("parallel", "arbitrary")),
        name="mixer_prompt",
    )(g_in, br, c_in, *prm)


def _mixer_sample_kernel(g_ref, br_ref, c_ref, s0_ref, wg_ref, bgate_ref, gng_ref, lng_ref, lnb_ref, ws_ref,
                         bst_ref, o_ref, vn_ref, st_ref, *, l_new):
    n = g_ref.shape[0]
    bd = _head_blockdiag(W_B)
    bd_bf = jnp.where(bd, 1.0, 0.0).astype(BF16)
    r, c = _iota2((n, n), 0), _iota2((n, n), 1)
    csum_sel = jnp.where(_chunk_causal(n, l_new), 1.0, 0.0).astype(BF16)
    last_sel = jnp.where(_idiv(r, l_new) == _idiv(c, l_new), 1.0, 0.0).astype(BF16)
    g = g_ref[...]
    gq = g[:, 0:W_B] * (DK_B ** -0.5)
    gk, gv, gg = g[:, W_B:2 * W_B], g[:, 2 * W_B:3 * W_B], g[:, 3 * W_B:4 * W_B]
    la = _gla_gate(br_ref[...], wg_ref, bgate_ref)
    bcum = _dot_sel(csum_sel, la)
    blast = _dot_sel(last_sel, la)
    q_in = gq * jnp.exp(bcum)
    k_in = gk * jnp.exp(-bcum)
    k_end = gk * jnp.exp(blast - bcum)
    o = _gla_intra(q_in, k_in, gv, l_new)
    zrows = jnp.zeros((LANES - n, W_B), F32)
    ke_t = jnp.concatenate([k_end, zrows], axis=0).T
    bl_t = jnp.concatenate([blast, zrows], axis=0).T
    v_pad = jnp.concatenate([gv, zrows], axis=0).astype(BF16)
    rows = _iota2((n, W_B), 0)
    cols = _iota2((W_B, LANES), 1)
    zblk = jnp.zeros((DK_B, DV_B), F32)
    for s in range(n // l_new):
        s0 = jnp.concatenate(
            [jnp.concatenate([s0_ref[s, h] if g == h else zblk for g in range(HB)], axis=1) for h in range(HB)],
            axis=0)
        qs = jnp.where(_idiv(rows, l_new) == s, q_in, 0.0).astype(BF16)
        o = o + jnp.dot(qs, s0.astype(BF16), preferred_element_type=F32)
        kes = jnp.where(_idiv(cols, l_new) == s, ke_t, 0.0).astype(BF16)
        upd = jnp.dot(kes, v_pad, preferred_element_type=F32)
        dl = jnp.exp(bl_t[:, s * l_new:s * l_new + 1])
        fin = s0 * dl + upd
        for h in range(HB):
            st_ref[s, h] = fin[h * DK_B:(h + 1) * DK_B, h * DV_B:(h + 1) * DV_B]
    o_b = _gla_finish(o, gg, gng_ref, bd_bf)
    o_c, vn = _chunk_mlp(c_ref[...], lng_ref, lnb_ref, ws_ref, bst_ref[...], l_new, bd_bf)
    o_ref[...] = jnp.concatenate([o_b, o_c], axis=1).astype(o_ref.dtype)
    vn_ref[...] = vn


def _mixer_sample(g_in, br, c_in, s0bd, prm, l_new, ts=64):
    t = g_in.shape[0]
    ns = ts // l_new
    row = lambda n: pl.BlockSpec((ts, n), lambda i: (i, 0))
    full = lambda a: pl.BlockSpec(a.shape, lambda i: (0,) * a.ndim)
    st = pl.BlockSpec((ns, HB, DK_B, DV_B), lambda i: (i, 0, 0, 0))
    kern = functools.partial(_mixer_sample_kernel, l_new=l_new)
    return pl.pallas_call(
        kern,
        grid=(t // ts,),
        in_specs=[row(4 * W_B), row(LANES), row(2 * W_C), st] + [full(a) for a in prm],
        out_specs=[row(W_B + W_C), row(W_C), st],
        out_shape=[jax.ShapeDtypeStruct((t, W_B + W_C), BF16),
                   jax.ShapeDtypeStruct((t, W_C), F32),
                   jax.ShapeDtypeStruct(s0bd.shape, F32)],
        compiler_params=_cparams(("parallel",)),
        name="mixer_sample",
    )(g_in, br, c_in, s0bd, *prm)


def _outproj_kernel(oa_ref, obc_ref, x_ref, wo_ref, g_ref, b_ref, rw_ref, rb_ref, h_ref, hp_ref, lg_ref, *, alpha):
    y = jnp.dot(oa_ref[...], wo_ref[0:W_A, :], preferred_element_type=F32)
    y = y + jnp.dot(obc_ref[...], wo_ref[W_A:, :], preferred_element_type=F32)
    h = _ln_rows(alpha * x_ref[...] + y, g_ref[...], b_ref[...])
    h_ref[...] = h
    hp_ref[...] = _pack_bf16_pairs(h)
    lg_ref[...] = jnp.dot(h, rw_ref[...], preferred_element_type=F32, precision=lax.Precision.HIGHEST) + rb_ref[...]


def _outproj(o_a, o_bc, x, wo, g, b, rw, rb, alpha):
    t = x.shape[0]
    tm = min(512, t)
    row = lambda n: pl.BlockSpec((tm, n), lambda i: (i, 0))
    full = lambda a: pl.BlockSpec(a.shape, lambda i: (0,) * a.ndim)
    return pl.pallas_call(
        functools.partial(_outproj_kernel, alpha=alpha),
        grid=(t // tm,),
        in_specs=[row(W_A), row(W_B + W_C), row(D_MODEL)] + [full(a) for a in (wo, g, b, rw, rb)],
        out_specs=[row(D_MODEL), row(D_PACK), row(LANES)],
        out_shape=[jax.ShapeDtypeStruct((t, D_MODEL), F32), jax.ShapeDtypeStruct((t, D_PACK), F32),
                   jax.ShapeDtypeStruct((t, LANES), F32)],
        compiler_params=_cparams(("parallel",)),
        name="outproj",
    )(o_a, o_bc, x, wo, g, b, rw, rb)


def _moe_kernel(be_ref, nu_ref, x_ref, w1_ref, b1_ref, w2_ref, b2_ref, y_ref, w1b_sc, w2b_sc):
    i = pl.program_id(0)

    @pl.when((i == 0) | (be_ref[i] != be_ref[jnp.maximum(i - 1, 0)]))
    def _():
        w1b_sc[...] = w1_ref[...].astype(BF16)
        w2b_sc[...] = w2_ref[...].astype(BF16)

    @pl.when(i >= nu_ref[0])
    def _():
        y_ref[...] = jnp.zeros(y_ref.shape, F32)

    @pl.when(i < nu_ref[0])
    def _():
        xb = _unpack_bf16_pairs(x_ref[...]).astype(BF16)
        acc = None
        for c in range(D_FF // MOE_FF_CHUNK):
            gs = slice(c * MOE_FF_CHUNK, (c + 1) * MOE_FF_CHUNK)
            us = slice(D_FF + c * MOE_FF_CHUNK, D_FF + (c + 1) * MOE_FF_CHUNK)
            g = jnp.dot(xb, w1b_sc[:, gs], preferred_element_type=F32) + b1_ref[:, gs]
            u = jnp.dot(xb, w1b_sc[:, us], preferred_element_type=F32) + b1_ref[:, us]
            g = jnp.minimum(g, SWIGLU_LIMIT)
            u = jnp.clip(u, -SWIGLU_LIMIT, SWIGLU_LIMIT)
            act = (u + 1.0) * g * (1.0 / (1.0 + jnp.exp(-SWIGLU_ALPHA * g)))
            part = jnp.dot(act.astype(BF16), w2b_sc[gs, :], preferred_element_type=F32)
            acc = part if acc is None else acc + part
        y_ref[...] = _pack_bf16_pairs(acc + b2_ref[...])


def _moe_experts(x_pad, block_e, n_used, w1, b1, w2, b2, layer, bm):
    nb = x_pad.shape[0] // bm
    return pl.pallas_call(
        _moe_kernel,
        grid_spec=pltpu.PrefetchScalarGridSpec(
            num_scalar_prefetch=2,
            grid=(nb,),
            in_specs=[pl.BlockSpec((bm, D_PACK), lambda i, be, nu: (i, 0)),
                      pl.BlockSpec((None, None, D_MODEL, 2 * D_FF), lambda i, be, nu: (layer, be[i], 0, 0)),
                      pl.BlockSpec((None, None, 1, 2 * D_FF), lambda i, be, nu: (layer, be[i], 0, 0)),
                      pl.BlockSpec((None, None, D_FF, D_MODEL), lambda i, be, nu: (layer, be[i], 0, 0)),
                      pl.BlockSpec((None, None, 1, D_MODEL), lambda i, be, nu: (layer, be[i], 0, 0))],
            out_specs=pl.BlockSpec((bm, D_PACK), lambda i, be, nu: (i, 0)),
            scratch_shapes=[pltpu.VMEM((D_MODEL, 2 * D_FF), BF16), pltpu.VMEM((D_FF, D_MODEL), BF16)],
        ),
        out_shape=jax.ShapeDtypeStruct((nb * bm, D_PACK), F32),
        compiler_params=_cparams(("arbitrary",)),
        name="moe_experts",
    )(block_e, n_used, x_pad, w1, b1, w2, b2)


def _route(logits, bm):
    t = logits.shape[0]
    m = t * TOP_K
    top_v, top_i = lax.top_k(logits[:, :N_EXPERTS], TOP_K)
    gate = jax.nn.softmax(top_v, axis=-1)
    flat_e = top_i.reshape(m)
    onehot = (flat_e[:, None] == jnp.arange(N_EXPERTS)[None, :]).astype(jnp.int32)
    csum = jnp.cumsum(onehot, axis=0)
    rank = jnp.take_along_axis(csum, flat_e[:, None], axis=1)[:, 0] - 1
    counts = csum[-1]
    padded = ((counts + bm - 1) // bm) * bm
    pad_end = jnp.cumsum(padded)
    pad_start = pad_end - padded
    dest = pad_start[flat_e] + rank
    nb = -(-m // bm) + N_EXPERTS
    block_e = jnp.sum((pad_end[None, :] <= (jnp.arange(nb) * bm)[:, None]).astype(jnp.int32), axis=1)
    block_e = jnp.minimum(block_e, N_EXPERTS - 1)
    n_used = (pad_end[-1] // bm).astype(jnp.int32).reshape(1)
    block_e = jnp.where(jnp.arange(nb) < n_used[0], block_e, block_e[jnp.maximum(n_used[0] - 1, 0)])
    sorted_tok = (jnp.argsort(flat_e, stable=True) // TOP_K).astype(jnp.int32)
    srt_start = jnp.cumsum(counts) - counts
    slot = jnp.arange(nb * bm, dtype=jnp.int32).reshape(nb, bm)
    shift = (srt_start - pad_start)[block_e][:, None]
    valid = slot < (pad_start + counts)[block_e][:, None]
    src_tok = jnp.where(valid, sorted_tok[jnp.clip(slot + shift, 0, m - 1)], 0).reshape(nb * bm)
    return gate, dest.reshape(t, TOP_K), src_tok, block_e.astype(jnp.int32), n_used


def _ln2_kernel(h_ref, gate_ref, *rest, alpha):
    y_refs, (g_ref, b_ref, o_ref) = rest[:TOP_K], rest[TOP_K:]
    gate = gate_ref[...]
    x = alpha * h_ref[...]
    for k in range(TOP_K):
        x = x + gate[:, k:k + 1] * _unpack_bf16_pairs(y_refs[k][...])
    o_ref[...] = _ln_rows(x, g_ref[...], b_ref[...])


def _ln2(h, gate, yg, row0, g, b, alpha):
    t = h.shape[0]
    t_all = yg.shape[0] // TOP_K
    tm = min(512, t)
    assert row0 % tm == 0 and t_all % tm == 0
    off = row0 // tm
    nt_all = t_all // tm
    y_specs = [pl.BlockSpec((tm, D_PACK), functools.partial(lambda i, k: (k * nt_all + off + i, 0), k=k))
               for k in range(TOP_K)]
    return pl.pallas_call(
        functools.partial(_ln2_kernel, alpha=alpha),
        grid=(t // tm,),
        in_specs=[pl.BlockSpec((tm, D_MODEL), lambda i: (i, 0)),
                  pl.BlockSpec((tm, LANES), lambda i: (i + off, 0))] + y_specs +
                 [pl.BlockSpec((1, D_MODEL), lambda i: (0, 0)),
                  pl.BlockSpec((1, D_MODEL), lambda i: (0, 0))],
        out_specs=pl.BlockSpec((tm, D_MODEL), lambda i: (i, 0)),
        out_shape=jax.ShapeDtypeStruct((t, D_MODEL), F32),
        compiler_params=_cparams(("parallel",)),
        name="ln2",
    )(h, gate, *([yg] * TOP_K), g, b)


SC_CORES, SC_SUBCORES = 2, 16
SC_CHUNK = 64


def _sc_gather(table, idx):
    b, d = idx.shape[0], table.shape[1]
    workers = SC_CORES * SC_SUBCORES
    per_w = b // workers
    assert per_w * workers == b and per_w % SC_CHUNK == 0
    mesh = plsc.VectorSubcoreMesh(core_axis_name="c", subcore_axis_name="s")

    n_chunks = per_w // SC_CHUNK

    @functools.partial(
        pl.kernel, mesh=mesh, out_type=jax.ShapeDtypeStruct((b, d), table.dtype),
        scratch_types=[pltpu.VMEM((per_w,), jnp.int32), pltpu.VMEM((2, SC_CHUNK, d), table.dtype),
                       pltpu.SemaphoreType.DMA((2,))],
        name="sc_gather")
    def gather(table_hbm, idx_hbm, out_hbm, idx_v, rows_v, sems):
        wid = lax.axis_index("s") * SC_CORES + lax.axis_index("c")
        pltpu.sync_copy(idx_hbm.at[pl.ds(wid * per_w, per_w)], idx_v)

        def fetch(i, slot):
            return pltpu.make_async_copy(table_hbm.at[idx_v.at[pl.ds(i * SC_CHUNK, SC_CHUNK)]],
                                         rows_v.at[slot], sems.at[slot])

        fetch(0, 0).start()

        @pl.loop(0, n_chunks)
        def _(i):
            slot = lax.rem(i, 2)

            @pl.when(i + 1 < n_chunks)
            def _():
                fetch(i + 1, 1 - slot).start()

            fetch(i, slot).wait()
            pltpu.sync_copy(rows_v.at[slot], out_hbm.at[pl.ds(wid * per_w + i * SC_CHUNK, SC_CHUNK)])

    return gather(table, idx)


def _rope_tables(pos):
    half = ROT_DIM // 2
    inv_freq = ROPE_THETA ** (-jnp.arange(0, ROT_DIM, 2, dtype=F32) / ROT_DIM)
    ang = pos.astype(F32)[:, None] * inv_freq[None, :]
    cos, sin = jnp.cos(ang), jnp.sin(ang)
    m = np.arange(LANES) % DQK_A
    idx = m % half
    cos_l = jnp.where(m < ROT_DIM, cos[:, idx], 1.0)
    sa = jnp.where(m < half, -sin[:, idx], 0.0)
    sb = jnp.where((m >= half) & (m < ROT_DIM), sin[:, idx], 0.0)
    return cos_l, sa, sb


def _prep_w_in(w):
    r0 = COL_C
    r1 = r0 + GATE_RANK
    pad = jnp.zeros((w.shape[0], LANES - GATE_RANK), w.dtype)
    return jnp.concatenate([w[:, :r0], w[:, r1:], w[:, r0:r1], pad], axis=1).astype(BF16)


def _tile_lanes(v, reps):
    return jnp.tile(v.reshape(1, -1), (1, reps)).astype(F32)


def _blockdiag_states(s):
    n = s.shape[0]
    eye = jnp.eye(HB, dtype=s.dtype)
    return jnp.einsum('nhde,hg->nhdge', s, eye).reshape(n, HB * DK_B, HB * DV_B)


def _diag_states(sbd):
    n = sbd.shape[0]
    s = sbd.reshape(n, HB, DK_B, HB, DV_B)
    return jnp.stack([s[:, h, :, h, :] for h in range(HB)], axis=1)


def kernel(x_prompt, x_sample, cache_k, cache_v, page_table, state_gla, w_in, lam_q1, lam_k1, lam_q2, lam_k2, attn_norm_g, gla_w_gate, gla_b_gate, gla_norm_g, cmlp_ln_g, cmlp_ln_b, cmlp_ws, cmlp_bs, w_o, ln1_g, ln1_b, router_w, router_b, exp_w1, exp_b1, exp_w2, exp_b2, ln2_g, ln2_b):
    depth = w_in.shape[0]
    bp, s_len, _ = x_prompt.shape
    db, l_new, _ = x_sample.shape
    n_phys, page = cache_k.shape[1], cache_k.shape[2]
    past_len = page_table.shape[1] * page
    alpha = (2 * depth) ** 0.25
    tp, ts = bp * s_len, db * l_new
    bm = MOE_BLOCK

    tabs_p = _rope_tables(jnp.arange(s_len))
    tabs_s = _rope_tables(past_len + (jnp.arange(ts) % l_new))
    page_table = page_table.astype(jnp.int32)

    hp = x_prompt.reshape(tp, D_MODEL)
    hs = x_sample.reshape(ts, D_MODEL)
    outs = {k: [] for k in ("gp", "gs", "cs")}
    kp_all, vp_all = (jnp.zeros((depth * tp, HA, DV_A), F32) for _ in range(2))
    ks_all, vs_all = (jnp.zeros((depth * ts, HA, DV_A), F32) for _ in range(2))
    for l in range(depth):
        lam_init = 0.8 - 0.6 * math.exp(-0.3 * l)
        w = _prep_w_in(w_in[l])
        lamv = jnp.pad(jnp.stack([lam_q1[l], lam_k1[l], lam_q2[l], lam_k2[l]]).astype(F32),
                       ((0, 0), (0, LANES - DQK_A)))
        g_attn = attn_norm_g[l].reshape(1, DV_A).astype(F32)
        wg = jnp.pad(gla_w_gate[l], ((0, LANES - GATE_RANK), (0, 0))).astype(BF16)
        wo = w_o[l].astype(BF16)
        rw = jnp.pad(router_w[l].astype(F32), ((0, 0), (0, LANES - N_EXPERTS)))
        rb = jnp.pad(router_b[l].astype(F32), (0, LANES - N_EXPERTS), constant_values=NEG_INF).reshape(1, LANES)
        ln1 = (ln1_g[l].reshape(1, D_MODEL), ln1_b[l].reshape(1, D_MODEL))

        def mixer_params(lc, n_rows):
            reps = n_rows // lc
            ws = jnp.tile(cmlp_ws[l][:, :lc, :lc], (1, reps, reps))
            bst = jnp.tile(jnp.repeat(cmlp_bs[l][:, :lc].T, DC, axis=1), (reps, 1))
            return (wg, gla_b_gate[l].reshape(1, W_B), _tile_lanes(gla_norm_g[l], HB),
                    _tile_lanes(cmlp_ln_g[l], HC), _tile_lanes(cmlp_ln_b[l], HC), ws, bst)

        q, kp_all, vp_all, kb, vb, g_in, c_in, br = _inproj(hp, w, tabs_p, l, kp_all, vp_all)
        o_a = _attn_prompt(q, kb, vb, lamv, g_attn, bp, lam_init)
        o_bc, st_p = _mixer_prompt(g_in, br, c_in, mixer_params(CMLP_CHUNK, CMLP_CHUNK), bp)
        hp1, hp1k, lg_p = _outproj(o_a, o_bc, hp, wo, *ln1, rw, rb, alpha)
        outs["gp"].append(_diag_states(st_p))

        q, ks_all, vs_all, kb, vb, g_in, c_in, br = _inproj(hs, w, tabs_s, l, ks_all, vs_all)
        o_a = _attn_sample(q, kb, vb, cache_k, cache_v, l, page_table, lamv, g_attn, l_new, lam_init)
        rows_s = min(64, ts)
        lc = min(l_new, CMLP_CHUNK)
        o_bc, vn, st_s = _mixer_sample(g_in, br, c_in, state_gla[l].astype(F32),
                                       mixer_params(lc, rows_s), l_new, rows_s)
        hs1, hs1k, lg_s = _outproj(o_a, o_bc, hs, wo, *ln1, rw, rb, alpha)
        outs["gs"].append(st_s)
        outs["cs"].append(vn.reshape(db, l_new, W_C))

        gate, dest, src_tok, block_e, n_used = _route(jnp.concatenate([lg_p, lg_s], axis=0), bm)
        x_pad = _sc_gather(jnp.concatenate([hp1k, hs1k], axis=0), src_tok)
        y_pad = _moe_experts(x_pad, block_e, n_used, exp_w1, exp_b1.reshape(depth, N_EXPERTS, 1, -1),
                             exp_w2, exp_b2.reshape(depth, N_EXPERTS, 1, -1), l, bm)
        yg = _sc_gather(y_pad, dest.T.reshape(-1))
        gate = jnp.pad(gate, ((0, 0), (0, LANES - TOP_K)))
        hp = _ln2(hp1, gate, yg, 0, ln2_g[l].reshape(1, -1), ln2_b[l].reshape(1, -1), alpha)
        hs = _ln2(hs1, gate, yg, tp, ln2_g[l].reshape(1, -1), ln2_b[l].reshape(1, -1), alpha)

    return (hp.reshape(bp, s_len, D_MODEL), hs.reshape(db, l_new, D_MODEL),
            kp_all.reshape(depth, bp, s_len, HA, DV_A), vp_all.reshape(depth, bp, s_len, HA, DV_A),
            jnp.stack(outs["gp"]),
            ks_all.reshape(depth, db, l_new, HA, DV_A), vs_all.reshape(depth, db, l_new, HA, DV_A),
            jnp.stack(outs["gs"]), jnp.stack(outs["cs"]))
```

```python
import functools
import math

import numpy as np
import jax
import jax.numpy as jnp
from jax import lax
from jax.experimental import pallas as pl
from jax.experimental.pallas import tpu as pltpu
from jax.experimental.pallas import tpu_sc as plsc

F32, BF16 = jnp.float32, jnp.bfloat16
LANES = 128
VMEM_LIMIT = 48 * 1024 * 1024

D_MODEL = 1024
HA, DQK_A, DV_A = 4, 64, 128
ROT_DIM = DQK_A // 4
ROPE_THETA = 500000.0
HB, DK_B, DV_B = 4, 64, 64
GATE_RANK = 16
GATE_NORM = 16.0
HC, DC = 4, 64
CMLP_CHUNK = 128
N_EXPERTS = 32
TOP_K = 4
D_FF = D_MODEL
SWIGLU_LIMIT = 7.0
SWIGLU_ALPHA = 1.702
LN_EPS = 1e-5
RMS_EPS = 1e-6
NEG_INF = -1e30
LOG2E = math.log2(math.e)

W_A = HA * 2 * DQK_A
W_B = HB * DK_B
W_C = HC * DC
COL_G = 3 * W_A
COL_C = COL_G + 4 * W_B
COL_R = COL_C + 2 * W_C
COL_END = COL_R + LANES
GLA_CHUNK_PROMPT = 32
ATTN_BQ, ATTN_BK = 512, 512
D_PACK = D_MODEL // 2
MOE_BLOCK = 256
MOE_FF_CHUNK = 512


def _cparams(sem):
    return pltpu.CompilerParams(dimension_semantics=sem, vmem_limit_bytes=VMEM_LIMIT)


def _split3(x):
    hi = x.astype(BF16)
    r = x - hi.astype(F32)
    mid = r.astype(BF16)
    lo = (r - mid.astype(F32)).astype(BF16)
    return hi, mid, lo


def _dot_sel(sel_bf16, x):
    acc = None
    for p in _split3(x):
        d = jnp.dot(sel_bf16, p, preferred_element_type=F32)
        acc = d if acc is None else acc + d
    return acc


def _seg_sum(x, bd_bf16):
    acc = None
    for p in _split3(x):
        d = jnp.dot(p, bd_bf16, preferred_element_type=F32)
        acc = d if acc is None else acc + d
    return acc


def _pack_bf16_pairs(x):
    u = lax.bitcast_convert_type(x, jnp.uint32)
    r = u + (jnp.uint32(0x7FFF) + ((u >> 16) & jnp.uint32(1)))
    w = x.shape[1] // 2
    word = (r[:, :w] & jnp.uint32(0xFFFF0000)) | (r[:, w:] >> 16)
    return lax.bitcast_convert_type(word, F32)


def _unpack_bf16_pairs(words):
    u = lax.bitcast_convert_type(words, jnp.uint32)
    hi = lax.bitcast_convert_type(u & jnp.uint32(0xFFFF0000), F32)
    lo = lax.bitcast_convert_type(u << 16, F32)
    return jnp.concatenate([hi, lo], axis=1)


def _iota2(shape, dim):
    return lax.broadcasted_iota(jnp.int32, shape, dim)


def _idiv(x, n):
    shift = n.bit_length() - 1
    assert n == 1 << shift
    return x >> shift


def _head_blockdiag(n):
    r, c = _iota2((n, n), 0), _iota2((n, n), 1)
    return _idiv(r, DK_B) == _idiv(c, DK_B)


def _chunk_causal(n, chunk):
    r, c = _iota2((n, n), 0), _iota2((n, n), 1)
    return (_idiv(r, chunk) == _idiv(c, chunk)) & (c <= r)


def _ln_rows(x, g, b):
    mu = jnp.mean(x, axis=-1, keepdims=True)
    xc = x - mu
    var = jnp.mean(xc * xc, axis=-1, keepdims=True)
    return xc * lax.rsqrt(var + LN_EPS) * g + b


def _inproj_kernel(x_ref, w_ref, cos_ref, sa_ref, sb_ref, k_all_ref, v_all_ref,
                   q_ref, k_ref, v_ref, kb_ref, vb_ref, g_ref, c_ref, br_ref):
    del k_all_ref, v_all_ref
    xb = x_ref[...].astype(BF16)

    def proj(a, b):
        return jnp.dot(xb, w_ref[:, a:b], preferred_element_type=F32)

    cos, sa, sb = cos_ref[...], sa_ref[...], sb_ref[...]

    def rope(z):
        outs = []
        for i in range(z.shape[1] // LANES):
            zi = z[:, i * LANES:(i + 1) * LANES]
            outs.append(zi * cos + pltpu.roll(zi, LANES - ROT_DIM // 2, 1) * sa
                        + pltpu.roll(zi, ROT_DIM // 2, 1) * sb)
        return jnp.concatenate(outs, axis=1)

    q_ref[...] = (rope(proj(0, W_A)) * (DQK_A ** -0.5 * LOG2E)).astype(BF16)
    k = rope(proj(W_A, 2 * W_A))
    kb_ref[...] = k.astype(BF16)
    v = proj(2 * W_A, 3 * W_A)
    vb_ref[...] = v.astype(BF16)
    for h in range(HA):
        k_ref[:, h, :] = k[:, h * LANES:(h + 1) * LANES]
        v_ref[:, h, :] = v[:, h * LANES:(h + 1) * LANES]
    g_ref[...] = proj(COL_G, COL_C)
    c_ref[...] = proj(COL_C, COL_R)
    br_ref[...] = proj(COL_R, COL_END)


def _inproj(x, w, tabs, layer, k_all, v_all):
    t = x.shape[0]
    tm = min(512, t)
    nt = t // tm
    cos, sa, sb = tabs
    ntab = cos.shape[0] // tm
    row = lambda n: pl.BlockSpec((tm, n), lambda i: (i, 0))
    tab = pl.BlockSpec((tm, LANES), lambda i: (i % ntab, 0))
    heads = pl.BlockSpec((tm, HA, DV_A), lambda i: (layer * nt + i, 0, 0))
    anywhere = pl.BlockSpec(memory_space=pl.ANY)
    shapes = [((W_A,), BF16), None, None, ((W_A,), BF16), ((W_A,), BF16),
              ((4 * W_B,), F32), ((2 * W_C,), F32), ((LANES,), F32)]
    return pl.pallas_call(
        _inproj_kernel,
        grid=(nt,),
        in_specs=[row(D_MODEL), pl.BlockSpec((D_MODEL, COL_END), lambda i: (0, 0)), tab, tab, tab,
                  anywhere, anywhere],
        out_specs=[heads if s is None else row(s[0][0]) for s in shapes],
        out_shape=[jax.ShapeDtypeStruct(k_all.shape, F32) if s is None else jax.ShapeDtypeStruct((t,) + s[0], s[1])
                   for s in shapes],
        input_output_aliases={5: 1, 6: 2},
        compiler_params=_cparams(("parallel",)),
        name="inproj",
    )(x, w, cos, sa, sb, k_all, v_all)


def _diff_lambda(lamv, lam_init):
    a = jnp.sum(lamv[0:1] * lamv[1:2], axis=1, keepdims=True)
    b = jnp.sum(lamv[2:3] * lamv[3:4], axis=1, keepdims=True)
    return jnp.exp(a) - jnp.exp(b) + lam_init


def _diff_finish(o1, o2, lam, g, lam_init):
    o = o1 - lam * o2
    ms = jnp.mean(o * o, axis=-1, keepdims=True)
    return o * lax.rsqrt(ms + RMS_EPS) * g * (1.0 - lam_init)


def _split_maps(q):
    lane = _iota2(q.shape, 1)
    zero = jnp.zeros_like(q)
    return jnp.concatenate([jnp.where(lane < DQK_A, q, zero), jnp.where(lane >= DQK_A, q, zero)], axis=0)


def _attn_prompt_kernel(q_ref, k_ref, v_ref, lamv_ref, g_ref, o_ref, m_sc, l_sc, acc_sc, *, bq, bk, lam_init):
    qi = pl.program_id(2)
    qq = _split_maps(q_ref[...])
    m_sc[...] = jnp.full(m_sc.shape, NEG_INF, F32)
    l_sc[...] = jnp.zeros(l_sc.shape, F32)
    acc_sc[...] = jnp.zeros(acc_sc.shape, F32)

    def step(j, masked):
        start = pl.multiple_of(j * bk, bk)
        k = k_ref[pl.ds(start, bk), :]
        v = v_ref[pl.ds(start, bk), :]
        s = lax.dot_general(qq, k, (((1,), (1,)), ((), ())), preferred_element_type=F32)
        if masked:
            r = (_iota2(s.shape, 0) & (bq - 1)) + qi * bq
            c = _iota2(s.shape, 1) + j * bk
            s = jnp.where(c <= r, s, NEG_INF)
        tiles = range(bk // LANES)
        m_prev = m_sc[...]
        s_max = functools.reduce(jnp.maximum, [s[:, i * LANES:(i + 1) * LANES] for i in tiles])
        m_new = jnp.maximum(m_prev, jnp.max(s_max, axis=1, keepdims=True))
        alpha = jnp.exp2(m_prev - m_new)
        p = jnp.exp2(s - jnp.tile(m_new, (1, bk // LANES)))
        p_sum = functools.reduce(jnp.add, [p[:, i * LANES:(i + 1) * LANES] for i in tiles])
        l_sc[...] = alpha * l_sc[...] + jnp.sum(p_sum, axis=1, keepdims=True)
        acc_sc[...] = alpha * acc_sc[...] + jnp.dot(p.astype(BF16), v, preferred_element_type=F32)
        m_sc[...] = m_new

    def body(j, carry):
        step(j, False)
        return carry

    n_full = (qi * bq) // bk
    lax.fori_loop(0, n_full, body, 0)
    for d in range(max(bq // bk, 1)):
        step(n_full + d, True)

    o = acc_sc[...] / l_sc[...]
    lam = _diff_lambda(lamv_ref[...], lam_init)
    o_ref[...] = _diff_finish(o[:bq], o[bq:], lam, g_ref[...], lam_init).astype(o_ref.dtype)


def _attn_prompt(q, kb, vb, lamv, g, nbatch, lam_init):
    t = q.shape[0]
    s = t // nbatch
    bq, bk = min(ATTN_BQ, s), min(ATTN_BK, s)
    assert max(bq, bk) % min(bq, bk) == 0
    blk = bq
    nq = s // blk
    kern = functools.partial(_attn_prompt_kernel, bq=bq, bk=bk, lam_init=lam_init)
    return pl.pallas_call(
        kern,
        grid=(nbatch, HA, nq),
        in_specs=[pl.BlockSpec((blk, LANES), lambda b, h, i: (b * nq + i, h)),
                  pl.BlockSpec((s, LANES), lambda b, h, i: (b, h)),
                  pl.BlockSpec((s, LANES), lambda b, h, i: (b, h)),
                  pl.BlockSpec((4, LANES), lambda b, h, i: (0, 0)),
                  pl.BlockSpec((1, LANES), lambda b, h, i: (0, 0))],
        out_specs=pl.BlockSpec((blk, LANES), lambda b, h, i: (b * nq + i, h)),
        out_shape=jax.ShapeDtypeStruct((t, W_A), BF16),
        scratch_shapes=[pltpu.VMEM((2 * blk, LANES), F32)] * 3,
        compiler_params=_cparams(("parallel", "parallel", "arbitrary")),
        name="attn_prompt",
    )(q, kb, vb, lamv, g)


def _attn_sample_kernel(pt_ref, q_ref, kn_ref, vn_ref, lamv_ref, g_ref, *rest, n_pages, l_new, lam_init):
    del pt_ref
    kp = rest[:n_pages]
    vp = rest[n_pages:2 * n_pages]
    o_ref = rest[2 * n_pages]
    rows_pg = kp[0].shape[0]
    rpad = q_ref.shape[0]
    nq = 2 * rpad
    lam = _diff_lambda(lamv_ref[...], lam_init)
    q = q_ref[...]
    qq = jnp.concatenate([_split_maps(q[:, h * LANES:(h + 1) * LANES]) for h in range(HA)],
                         axis=0).astype(BF16)
    r = _iota2((HA * nq, rows_pg), 0)
    c = _iota2((HA * nq, rows_pg), 1)
    head_ok = (c & (HA - 1)) == _idiv(r, nq)
    new_ok = head_ok & (_idiv(c, HA) <= (r & (rpad - 1))) & (c < l_new * HA)
    nt = (((1,), (1,)), ((), ()))
    zpad = jnp.zeros((rows_pg - kn_ref.shape[0], LANES), F32)
    k_new = jnp.concatenate([kn_ref[...], zpad], axis=0).astype(BF16)
    v_new = jnp.concatenate([vn_ref[...], zpad], axis=0).astype(BF16)
    s_new = jnp.where(new_ok, lax.dot_general(qq, k_new, nt, preferred_element_type=F32), NEG_INF)
    s_past = [jnp.where(head_ok, lax.dot_general(qq, kp[j][...].astype(BF16), nt, preferred_element_type=F32),
                        NEG_INF) for j in range(n_pages)]
    m = jnp.max(s_new, axis=1, keepdims=True)
    for sj in s_past:
        m = jnp.maximum(m, jnp.max(sj, axis=1, keepdims=True))
    p_new = jnp.exp2(s_new - m)
    l = jnp.sum(p_new, axis=1, keepdims=True)
    acc = jnp.dot(p_new.astype(BF16), v_new, preferred_element_type=F32)
    for j in range(n_pages):
        pj = jnp.exp2(s_past[j] - m)
        l = l + jnp.sum(pj, axis=1, keepdims=True)
        acc = acc + jnp.dot(pj.astype(BF16), vp[j][...].astype(BF16), preferred_element_type=F32)
    o = acc / l
    outs = [_diff_finish(o[h * nq:h * nq + rpad], o[h * nq + rpad:(h + 1) * nq], lam, g_ref[...], lam_init)
            for h in range(HA)]
    o_ref[...] = jnp.concatenate(outs, axis=1)


def _attn_sample(q, k, v, cache_k, cache_v, layer, page_table, lamv, g, l_new, lam_init):
    t = q.shape[0]
    db = t // l_new
    n_pages = page_table.shape[1]
    page = cache_k.shape[2]
    rpad = 8

    def pad_rows(a):
        a = a.reshape(db, l_new, W_A).astype(F32)
        return jnp.concatenate([a, jnp.zeros((db, rpad - l_new, W_A), F32)], axis=1)

    def new_rows(a):
        return a.reshape(db, l_new * HA, DV_A).astype(F32)

    depth, n_phys = cache_k.shape[:2]
    cache_k = cache_k.reshape(depth, n_phys, page * HA, DV_A)
    cache_v = cache_v.reshape(depth, n_phys, page * HA, DV_A)
    q_spec = pl.BlockSpec((None, rpad, W_A), lambda b, pt: (b, 0, 0))
    new_spec = pl.BlockSpec((None, l_new * HA, DV_A), lambda b, pt: (b, 0, 0))
    page_specs = [pl.BlockSpec((None, None, page * HA, DV_A),
                               functools.partial(lambda b, pt, j: (layer, pt[b, j], 0, 0), j=j))
                  for j in range(n_pages)]
    kern = functools.partial(_attn_sample_kernel, n_pages=n_pages, l_new=l_new, lam_init=lam_init)
    out = pl.pallas_call(
        kern,
        grid_spec=pltpu.PrefetchScalarGridSpec(
            num_scalar_prefetch=1,
            grid=(db,),
            in_specs=[q_spec, new_spec, new_spec,
                      pl.BlockSpec((4, LANES), lambda b, pt: (0, 0)),
                      pl.BlockSpec((1, LANES), lambda b, pt: (0, 0))] + page_specs + page_specs,
            out_specs=q_spec,
        ),
        out_shape=jax.ShapeDtypeStruct((db, rpad, W_A), F32),
        compiler_params=_cparams(("arbitrary",)),
        name="attn_sample",
    )(page_table, pad_rows(q), new_rows(k), new_rows(v), lamv, g,
      *([cache_k] * n_pages), *([cache_v] * n_pages))
    return out[:, :l_new].reshape(t, W_A).astype(BF16)


def _log_sigmoid(x):
    return jnp.minimum(x, 0.0) - jnp.log1p(jnp.exp(-jnp.abs(x)))


def _gla_gate(br, wg_ref, bgate_ref):
    x = jnp.dot(br.astype(BF16), wg_ref[...], preferred_element_type=F32) + bgate_ref[...]
    return _log_sigmoid(x) / GATE_NORM


def _gla_intra(q_att, k_in, v, chunk):
    keep = _chunk_causal(q_att.shape[0], chunk)
    lane = _idiv(_iota2(q_att.shape, 1), DK_B)
    kb = k_in.astype(BF16)
    atts, vs = [], []
    for h in range(HB):
        qh = jnp.where(lane == h, q_att, 0.0).astype(BF16)
        a = lax.dot_general(qh, kb, (((1,), (1,)), ((), ())), preferred_element_type=F32)
        atts.append(jnp.where(keep, a, 0.0).astype(BF16))
        vs.append(jnp.where(lane == h, v, 0.0).astype(BF16))
    return jnp.dot(jnp.concatenate(atts, axis=1), jnp.concatenate(vs, axis=0), preferred_element_type=F32)


def _gla_finish(o, gate_in, gng_ref, bd):
    ms = _seg_sum(o * o, bd) * (1.0 / DV_B)
    o = o * lax.rsqrt(ms + RMS_EPS) * gng_ref[...]
    return o * (gate_in * (1.0 / (1.0 + jnp.exp(-gate_in))))


def _chunk_mlp(c_in, lng_ref, lnb_ref, ws_ref, bst, chunk, bd):
    n = c_in.shape[0]
    cu, cv = c_in[:, :W_C], c_in[:, W_C:]
    mu = _seg_sum(cv, bd) * (1.0 / DC)
    xc = cv - mu
    var = _seg_sum(xc * xc, bd) * (1.0 / DC)
    vn = xc * lax.rsqrt(var + LN_EPS) * lng_ref[...] + lnb_ref[...]
    keep = _chunk_causal(n, chunk)
    lane = _idiv(_iota2(vn.shape, 1), DC)
    ws, vs = [], []
    for g in range(HC):
        ws.append(jnp.where(keep, ws_ref[g], 0.0).astype(BF16))
        vs.append(jnp.where(lane == g, vn, 0.0).astype(BF16))
    mixed = jnp.dot(jnp.concatenate(ws, axis=1), jnp.concatenate(vs, axis=0), preferred_element_type=F32) + bst
    return cu * mixed, vn


def _mixer_prompt_kernel(g_ref, br_ref, c_ref, wg_ref, bgate_ref, gng_ref, lng_ref, lnb_ref, ws_ref, bst_ref,
                         o_ref, st_ref, st_sc, *, ts, chunk):
    t = pl.program_id(1)

    @pl.when(t == 0)
    def _():
        st_sc[...] = jnp.zeros(st_sc.shape, F32)

    grp = CMLP_CHUNK
    bd = _head_blockdiag(W_B)
    bd_bf = jnp.where(bd, 1.0, 0.0).astype(BF16)
    csum_sel = jnp.where(_chunk_causal(grp, chunk), 1.0, 0.0).astype(BF16)
    rows = _iota2((grp, W_B), 0)
    half = chunk // 2
    for gi in range(ts // grp):
        rs = slice(gi * grp, (gi + 1) * grp)
        g = g_ref[rs, :]
        gq = g[:, 0:W_B] * (DK_B ** -0.5)
        gk, gv, gg = g[:, W_B:2 * W_B], g[:, 2 * W_B:3 * W_B], g[:, 3 * W_B:4 * W_B]
        la = _gla_gate(br_ref[rs, :], wg_ref, bgate_ref)
        bcum = _dot_sel(csum_sel, la)
        mids, lasts = [], []
        for ci in range(grp // chunk):
            mids.append(jnp.broadcast_to(bcum[ci * chunk + half - 1:ci * chunk + half, :], (chunk, W_B)))
            lasts.append(jnp.broadcast_to(bcum[(ci + 1) * chunk - 1:(ci + 1) * chunk, :], (chunk, W_B)))
        bmid = jnp.concatenate(mids, axis=0)
        blast = jnp.concatenate(lasts, axis=0)
        q_att = gq * jnp.exp(bcum - bmid)
        k_in = gk * jnp.exp(bmid - bcum)
        k_end = gk * jnp.exp(blast - bcum)
        q_dec = (gq * jnp.exp(bcum)).astype(BF16)
        o = _gla_intra(q_att, k_in, gv, chunk)
        v_t = gv.T.astype(BF16)
        o_inter = []
        for ci in range(grp // chunk):
            cs = slice(ci * chunk, (ci + 1) * chunk)
            st = st_sc[...]
            o_inter.append(lax.dot_general(q_dec[cs], st.astype(BF16), (((1,), (1,)), ((), ())),
                                           preferred_element_type=F32))
            kem = jnp.where(_idiv(rows, chunk) == ci, k_end, 0.0).astype(BF16)
            upd = jnp.dot(v_t, kem, preferred_element_type=F32)
            dl = jnp.exp(blast[ci * chunk:ci * chunk + 1, :])
            st_sc[...] = st * dl + jnp.where(bd, upd, 0.0)
        o = o + jnp.concatenate(o_inter, axis=0)
        o_b = _gla_finish(o, gg, gng_ref, bd_bf)
        o_c, _ = _chunk_mlp(c_ref[rs, :], lng_ref, lnb_ref, ws_ref, bst_ref[...], CMLP_CHUNK, bd_bf)
        o_ref[rs, :] = jnp.concatenate([o_b, o_c], axis=1).astype(o_ref.dtype)

    @pl.when(t == pl.num_programs(1) - 1)
    def _():
        st_ref[...] = st_sc[...].T


def _mixer_prompt(g_in, br, c_in, prm, nbatch, ts=256):
    t = g_in.shape[0]
    nt = t // nbatch // ts
    row = lambda n: pl.BlockSpec((ts, n), lambda b, i: (b * nt + i, 0))
    full = lambda a: pl.BlockSpec(a.shape, lambda b, i: (0,) * a.ndim)
    kern = functools.partial(_mixer_prompt_kernel, ts=ts, chunk=GLA_CHUNK_PROMPT)
    return pl.pallas_call(
        kern,
        grid=(nbatch, nt),
        in_specs=[row(4 * W_B), row(LANES), row(2 * W_C)] + [full(a) for a in prm],
        out_specs=[row(W_B + W_C), pl.BlockSpec((None, W_B, W_B), lambda b, i: (b, 0, 0))],
        out_shape=[jax.ShapeDtypeStruct((t, W_B + W_C), BF16),
                   jax.ShapeDtypeStruct((nbatch, W_B, W_B), F32)],
        scratch_shapes=[pltpu.VMEM((W_B, W_B), F32)],
        compiler_params=_cparams(("parallel", "arbitrary")),
        name="mixer_prompt",
    )(g_in, br, c_in, *prm)


def _mixer_sample_kernel(g_ref, br_ref, c_ref, s0_ref, wg_ref, bgate_ref, gng_ref, lng_ref, lnb_ref, ws_ref,
                         bst_ref, o_ref, vn_ref, st_ref, *, l_new):
    n = g_ref.shape[0]
    bd = _head_blockdiag(W_B)
    bd_bf = jnp.where(bd, 1.0, 0.0).astype(BF16)
    r, c = _iota2((n, n), 0), _iota2((n, n), 1)
    csum_sel = jnp.where(_chunk_causal(n, l_new), 1.0, 0.0).astype(BF16)
    last_sel = jnp.where(_idiv(r, l_new) == _idiv(c, l_new), 1.0, 0.0).astype(BF16)
    g = g_ref[...]
    gq = g[:, 0:W_B] * (DK_B ** -0.5)
    gk, gv, gg = g[:, W_B:2 * W_B], g[:, 2 * W_B:3 * W_B], g[:, 3 * W_B:4 * W_B]
    la = _gla_gate(br_ref[...], wg_ref, bgate_ref)
    bcum = _dot_sel(csum_sel, la)
    blast = _dot_sel(last_sel, la)
    q_in = gq * jnp.exp(bcum)
    k_in = gk * jnp.exp(-bcum)
    k_end = gk * jnp.exp(blast - bcum)
    o = _gla_intra(q_in, k_in, gv, l_new)
    zrows = jnp.zeros((LANES - n, W_B), F32)
    ke_t = jnp.concatenate([k_end, zrows], axis=0).T
    bl_t = jnp.concatenate([blast, zrows], axis=0).T
    v_pad = jnp.concatenate([gv, zrows], axis=0).astype(BF16)
    rows = _iota2((n, W_B), 0)
    cols = _iota2((W_B, LANES), 1)
    zblk = jnp.zeros((DK_B, DV_B), F32)
    for s in range(n // l_new):
        s0 = jnp.concatenate(
            [jnp.concatenate([s0_ref[s, h] if g == h else zblk for g in range(HB)], axis=1) for h in range(HB)],
            axis=0)
        qs = jnp.where(_idiv(rows, l_new) == s, q_in, 0.0).astype(BF16)
        o = o + jnp.dot(qs, s0.astype(BF16), preferred_element_type=F32)
        kes = jnp.where(_idiv(cols, l_new) == s, ke_t, 0.0).astype(BF16)
        upd = jnp.dot(kes, v_pad, preferred_element_type=F32)
        dl = jnp.exp(bl_t[:, s * l_new:s * l_new + 1])
        fin = s0 * dl + upd
        for h in range(HB):
            st_ref[s, h] = fin[h * DK_B:(h + 1) * DK_B, h * DV_B:(h + 1) * DV_B]
    o_b = _gla_finish(o, gg, gng_ref, bd_bf)
    o_c, vn = _chunk_mlp(c_ref[...], lng_ref, lnb_ref, ws_ref, bst_ref[...], l_new, bd_bf)
    o_ref[...] = jnp.concatenate([o_b, o_c], axis=1).astype(o_ref.dtype)
    vn_ref[...] = vn


def _mixer_sample(g_in, br, c_in, s0bd, prm, l_new, ts=64):
    t = g_in.shape[0]
    ns = ts // l_new
    row = lambda n: pl.BlockSpec((ts, n), lambda i: (i, 0))
    full = lambda a: pl.BlockSpec(a.shape, lambda i: (0,) * a.ndim)
    st = pl.BlockSpec((ns, HB, DK_B, DV_B), lambda i: (i, 0, 0, 0))
    kern = functools.partial(_mixer_sample_kernel, l_new=l_new)
    return pl.pallas_call(
        kern,
        grid=(t // ts,),
        in_specs=[row(4 * W_B), row(LANES), row(2 * W_C), st] + [full(a) for a in prm],
        out_specs=[row(W_B + W_C), row(W_C), st],
        out_shape=[jax.ShapeDtypeStruct((t, W_B + W_C), BF16),
                   jax.ShapeDtypeStruct((t, W_C), F32),
                   jax.ShapeDtypeStruct(s0bd.shape, F32)],
        compiler_params=_cparams(("parallel",)),
        name="mixer_sample",
    )(g_in, br, c_in, s0bd, *prm)


def _outproj_kernel(oa_ref, obc_ref, x_ref, wo_ref, g_ref, b_ref, rw_ref, rb_ref, h_ref, hp_ref, lg_ref, *, alpha):
    y = jnp.dot(oa_ref[...], wo_ref[0:W_A, :], preferred_element_type=F32)
    y = y + jnp.dot(obc_ref[...], wo_ref[W_A:, :], preferred_element_type=F32)
    h = _ln_rows(alpha * x_ref[...] + y, g_ref[...], b_ref[...])
    h_ref[...] = h
    hp_ref[...] = _pack_bf16_pairs(h)
    lg_ref[...] = jnp.dot(h, rw_ref[...], preferred_element_type=F32, precision=lax.Precision.HIGHEST) + rb_ref[...]


def _outproj(o_a, o_bc, x, wo, g, b, rw, rb, alpha):
    t = x.shape[0]
    tm = min(512, t)
    row = lambda n: pl.BlockSpec((tm, n), lambda i: (i, 0))
    full = lambda a: pl.BlockSpec(a.shape, lambda i: (0,) * a.ndim)
    return pl.pallas_call(
        functools.partial(_outproj_kernel, alpha=alpha),
        grid=(t // tm,),
        in_specs=[row(W_A), row(W_B + W_C), row(D_MODEL)] + [full(a) for a in (wo, g, b, rw, rb)],
        out_specs=[row(D_MODEL), row(D_PACK), row(LANES)],
        out_shape=[jax.ShapeDtypeStruct((t, D_MODEL), F32), jax.ShapeDtypeStruct((t, D_PACK), F32),
                   jax.ShapeDtypeStruct((t, LANES), F32)],
        compiler_params=_cparams(("parallel",)),
        name="outproj",
    )(o_a, o_bc, x, wo, g, b, rw, rb)


def _moe_kernel(be_ref, nu_ref, x_ref, w1_ref, b1_ref, w2_ref, b2_ref, y_ref, w1b_sc, w2b_sc):
    i = pl.program_id(0)

    @pl.when((i == 0) | (be_ref[i] != be_ref[jnp.maximum(i - 1, 0)]))
    def _():
        w1b_sc[...] = w1_ref[...].astype(BF16)
        w2b_sc[...] = w2_ref[...].astype(BF16)

    @pl.when(i >= nu_ref[0])
    def _():
        y_ref[...] = jnp.zeros(y_ref.shape, F32)

    @pl.when(i < nu_ref[0])
    def _():
        xb = _unpack_bf16_pairs(x_ref[...]).astype(BF16)
        acc = None
        for c in range(D_FF // MOE_FF_CHUNK):
            gs = slice(c * MOE_FF_CHUNK, (c + 1) * MOE_FF_CHUNK)
            us = slice(D_FF + c * MOE_FF_CHUNK, D_FF + (c + 1) * MOE_FF_CHUNK)
            g = jnp.dot(xb, w1b_sc[:, gs], preferred_element_type=F32) + b1_ref[:, gs]
            u = jnp.dot(xb, w1b_sc[:, us], preferred_element_type=F32) + b1_ref[:, us]
            g = jnp.minimum(g, SWIGLU_LIMIT)
            u = jnp.clip(u, -SWIGLU_LIMIT, SWIGLU_LIMIT)
            act = (u + 1.0) * g * (1.0 / (1.0 + jnp.exp(-SWIGLU_ALPHA * g)))
            part = jnp.dot(act.astype(BF16), w2b_sc[gs, :], preferred_element_type=F32)
            acc = part if acc is None else acc + part
        y_ref[...] = _pack_bf16_pairs(acc + b2_ref[...])


def _moe_experts(x_pad, block_e, n_used, w1, b1, w2, b2, layer, bm):
    nb = x_pad.shape[0] // bm
    return pl.pallas_call(
        _moe_kernel,
        grid_spec=pltpu.PrefetchScalarGridSpec(
            num_scalar_prefetch=2,
            grid=(nb,),
            in_specs=[pl.BlockSpec((bm, D_PACK), lambda i, be, nu: (i, 0)),
                      pl.BlockSpec((None, None, D_MODEL, 2 * D_FF), lambda i, be, nu: (layer, be[i], 0, 0)),
                      pl.BlockSpec((None, None, 1, 2 * D_FF), lambda i, be, nu: (layer, be[i], 0, 0)),
                      pl.BlockSpec((None, None, D_FF, D_MODEL), lambda i, be, nu: (layer, be[i], 0, 0)),
                      pl.BlockSpec((None, None, 1, D_MODEL), lambda i, be, nu: (layer, be[i], 0, 0))],
            out_specs=pl.BlockSpec((bm, D_PACK), lambda i, be, nu: (i, 0)),
            scratch_shapes=[pltpu.VMEM((D_MODEL, 2 * D_FF), BF16), pltpu.VMEM((D_FF, D_MODEL), BF16)],
        ),
        out_shape=jax.ShapeDtypeStruct((nb * bm, D_PACK), F32),
        compiler_params=_cparams(("arbitrary",)),
        name="moe_experts",
    )(block_e, n_used, x_pad, w1, b1, w2, b2)


ROUTE_E, ROUTE_RANK, ROUTE_GATE = 0, TOP_K, 2 * TOP_K


def _router_kernel(lg_ref, route_ref, cnt_ref, cnt_sc):
    @pl.when(pl.program_id(0) == 0)
    def _():
        cnt_sc[...] = jnp.zeros(cnt_sc.shape, F32)

    work = lg_ref[...]
    tm = work.shape[0]
    lane = _iota2(work.shape, 1)
    lane_f = lane.astype(F32)
    sels, vals, ids = [], [], []
    for _ in range(TOP_K):
        mx = jnp.max(work, axis=1, keepdims=True)
        idx = jnp.min(jnp.where(work == mx, lane_f, float(LANES)), axis=1, keepdims=True)
        sel = lane_f == idx
        sels.append(sel)
        vals.append(mx)
        ids.append(idx)
        work = jnp.where(sel, -jnp.inf, work)
    ex = [jnp.exp(v - vals[0]) for v in vals]
    den = ex[0]
    for x in ex[1:]:
        den = den + x
    picked = jnp.zeros(work.shape, F32)
    for sel in sels:
        picked = jnp.where(sel, 1.0, picked)
    r, c = _iota2((tm, tm), 0), _iota2((tm, tm), 1)
    before = jnp.dot(jnp.where(c < r, 1.0, 0.0).astype(BF16), picked.astype(BF16), preferred_element_type=F32)
    before = before + cnt_sc[...]
    out = jnp.zeros(work.shape, F32)
    for k in range(TOP_K):
        rank = jnp.sum(jnp.where(sels[k], before, 0.0), axis=1, keepdims=True)
        out = jnp.where(lane == ROUTE_E + k, ids[k], out)
        out = jnp.where(lane == ROUTE_RANK + k, rank, out)
        out = jnp.where(lane == ROUTE_GATE + k, ex[k] / den, out)
    route_ref[...] = out
    cnt_sc[...] = cnt_sc[...] + jnp.sum(picked, axis=0, keepdims=True)
    cnt_ref[...] = cnt_sc[...]


def _router(logits):
    t = logits.shape[0]
    tm = min(512, t)
    return pl.pallas_call(
        _router_kernel,
        grid=(t // tm,),
        in_specs=[pl.BlockSpec((tm, LANES), lambda i: (i, 0))],
        out_specs=[pl.BlockSpec((tm, LANES), lambda i: (i, 0)), pl.BlockSpec((1, LANES), lambda i: (0, 0))],
        out_shape=[jax.ShapeDtypeStruct((t, LANES), F32), jax.ShapeDtypeStruct((1, LANES), F32)],
        scratch_shapes=[pltpu.VMEM((1, LANES), F32)],
        compiler_params=_cparams(("arbitrary",)),
        name="router",
    )(logits)


def _route(logits, bm):
    t = logits.shape[0]
    m = t * TOP_K
    route, cnt = _router(logits)
    top_i = route[:, ROUTE_E:ROUTE_E + TOP_K].astype(jnp.int32)
    rank = route[:, ROUTE_RANK:ROUTE_RANK + TOP_K].astype(jnp.int32).reshape(m)
    flat_e = top_i.reshape(m)
    counts = cnt[0, :N_EXPERTS].astype(jnp.int32)
    padded = ((counts + bm - 1) // bm) * bm
    pad_end = jnp.cumsum(padded)
    pad_start = pad_end - padded
    dest = pad_start[flat_e] + rank
    nb = -(-m // bm) + N_EXPERTS
    block_e = jnp.sum((pad_end[None, :] <= (jnp.arange(nb) * bm)[:, None]).astype(jnp.int32), axis=1)
    block_e = jnp.minimum(block_e, N_EXPERTS - 1)
    n_used = (pad_end[-1] // bm).astype(jnp.int32).reshape(1)
    block_e = jnp.where(jnp.arange(nb) < n_used[0], block_e, block_e[jnp.maximum(n_used[0] - 1, 0)])
    sorted_tok = (jnp.argsort(flat_e, stable=True) // TOP_K).astype(jnp.int32)
    srt_start = jnp.cumsum(counts) - counts
    slot = jnp.arange(nb * bm, dtype=jnp.int32).reshape(nb, bm)
    shift = (srt_start - pad_start)[block_e][:, None]
    valid = slot < (pad_start + counts)[block_e][:, None]
    src_tok = jnp.where(valid, sorted_tok[jnp.clip(slot + shift, 0, m - 1)], slot % t).reshape(nb * bm)
    return route, dest.reshape(t, TOP_K), src_tok, block_e.astype(jnp.int32), n_used


def _ln2_kernel(h_ref, gate_ref, *rest, alpha):
    y_refs, (g_ref, b_ref, o_ref) = rest[:TOP_K], rest[TOP_K:]
    gate = gate_ref[...]
    x = alpha * h_ref[...]
    for k in range(TOP_K):
        x = x + gate[:, ROUTE_GATE + k:ROUTE_GATE + k + 1] * _unpack_bf16_pairs(y_refs[k][...])
    o_ref[...] = _ln_rows(x, g_ref[...], b_ref[...])


def _ln2(h, gate, yg, row0, g, b, alpha):
    t = h.shape[0]
    t_all = yg.shape[0] // TOP_K
    tm = min(512, t)
    assert row0 % tm == 0 and t_all % tm == 0
    off = row0 // tm
    nt_all = t_all // tm
    y_specs = [pl.BlockSpec((tm, D_PACK), functools.partial(lambda i, k: (k * nt_all + off + i, 0), k=k))
               for k in range(TOP_K)]
    return pl.pallas_call(
        functools.partial(_ln2_kernel, alpha=alpha),
        grid=(t // tm,),
        in_specs=[pl.BlockSpec((tm, D_MODEL), lambda i: (i, 0)),
                  pl.BlockSpec((tm, LANES), lambda i: (i + off, 0))] + y_specs +
                 [pl.BlockSpec((1, D_MODEL), lambda i: (0, 0)),
                  pl.BlockSpec((1, D_MODEL), lambda i: (0, 0))],
        out_specs=pl.BlockSpec((tm, D_MODEL), lambda i: (i, 0)),
        out_shape=jax.ShapeDtypeStruct((t, D_MODEL), F32),
        compiler_params=_cparams(("parallel",)),
        name="ln2",
    )(h, gate, *([yg] * TOP_K), g, b)


SC_CORES, SC_SUBCORES = 2, 16
SC_CHUNK = 64


def _sc_gather(table, idx):
    b, d = idx.shape[0], table.shape[1]
    workers = SC_CORES * SC_SUBCORES
    per_w = b // workers
    assert per_w * workers == b and per_w % SC_CHUNK == 0
    mesh = plsc.VectorSubcoreMesh(core_axis_name="c", subcore_axis_name="s")

    n_chunks = per_w // SC_CHUNK

    @functools.partial(
        pl.kernel, mesh=mesh, out_type=jax.ShapeDtypeStruct((b, d), table.dtype),
        scratch_types=[pltpu.VMEM((per_w,), jnp.int32), pltpu.VMEM((2, SC_CHUNK, d), table.dtype),
                       pltpu.SemaphoreType.DMA((2,))],
        name="sc_gather")
    def gather(table_hbm, idx_hbm, out_hbm, idx_v, rows_v, sems):
        wid = lax.axis_index("s") * SC_CORES + lax.axis_index("c")
        pltpu.sync_copy(idx_hbm.at[pl.ds(wid * per_w, per_w)], idx_v)

        def fetch(i, slot):
            return pltpu.make_async_copy(table_hbm.at[idx_v.at[pl.ds(i * SC_CHUNK, SC_CHUNK)]],
                                         rows_v.at[slot], sems.at[slot])

        fetch(0, 0).start()

        @pl.loop(0, n_chunks)
        def _(i):
            slot = lax.rem(i, 2)

            @pl.when(i + 1 < n_chunks)
            def _():
                fetch(i + 1, 1 - slot).start()

            fetch(i, slot).wait()
            pltpu.sync_copy(rows_v.at[slot], out_hbm.at[pl.ds(wid * per_w + i * SC_CHUNK, SC_CHUNK)])

    return gather(table, idx)


def _rope_tables(pos):
    half = ROT_DIM // 2
    inv_freq = ROPE_THETA ** (-jnp.arange(0, ROT_DIM, 2, dtype=F32) / ROT_DIM)
    ang = pos.astype(F32)[:, None] * inv_freq[None, :]
    cos, sin = jnp.cos(ang), jnp.sin(ang)
    m = np.arange(LANES) % DQK_A
    idx = m % half
    cos_l = jnp.where(m < ROT_DIM, cos[:, idx], 1.0)
    sa = jnp.where(m < half, -sin[:, idx], 0.0)
    sb = jnp.where((m >= half) & (m < ROT_DIM), sin[:, idx], 0.0)
    return cos_l, sa, sb


def _prep_w_in(w):
    r0 = COL_C
    r1 = r0 + GATE_RANK
    pad = jnp.zeros((w.shape[0], LANES - GATE_RANK), w.dtype)
    return jnp.concatenate([w[:, :r0], w[:, r1:], w[:, r0:r1], pad], axis=1).astype(BF16)


def _tile_lanes(v, reps):
    return jnp.tile(v.reshape(1, -1), (1, reps)).astype(F32)


def _blockdiag_states(s):
    n = s.shape[0]
    eye = jnp.eye(HB, dtype=s.dtype)
    return jnp.einsum('nhde,hg->nhdge', s, eye).reshape(n, HB * DK_B, HB * DV_B)


def _diag_states(sbd):
    n = sbd.shape[0]
    s = sbd.reshape(n, HB, DK_B, HB, DV_B)
    return jnp.stack([s[:, h, :, h, :] for h in range(HB)], axis=1)


def kernel(x_prompt, x_sample, cache_k, cache_v, page_table, state_gla, w_in, lam_q1, lam_k1, lam_q2, lam_k2, attn_norm_g, gla_w_gate, gla_b_gate, gla_norm_g, cmlp_ln_g, cmlp_ln_b, cmlp_ws, cmlp_bs, w_o, ln1_g, ln1_b, router_w, router_b, exp_w1, exp_b1, exp_w2, exp_b2, ln2_g, ln2_b):
    depth = w_in.shape[0]
    bp, s_len, _ = x_prompt.shape
    db, l_new, _ = x_sample.shape
    n_phys, page = cache_k.shape[1], cache_k.shape[2]
    past_len = page_table.shape[1] * page
    alpha = (2 * depth) ** 0.25
    tp, ts = bp * s_len, db * l_new
    bm = MOE_BLOCK

    tabs_p = _rope_tables(jnp.arange(s_len))
    tabs_s = _rope_tables(past_len + (jnp.arange(ts) % l_new))
    page_table = page_table.astype(jnp.int32)

    hp = x_prompt.reshape(tp, D_MODEL)
    hs = x_sample.reshape(ts, D_MODEL)
    outs = {k: [] for k in ("gp", "gs", "cs")}
    kp_all, vp_all = (jnp.zeros((depth * tp, HA, DV_A), F32) for _ in range(2))
    ks_all, vs_all = (jnp.zeros((depth * ts, HA, DV_A), F32) for _ in range(2))
    for l in range(depth):
        lam_init = 0.8 - 0.6 * math.exp(-0.3 * l)
        w = _prep_w_in(w_in[l])
        lamv = jnp.pad(jnp.stack([lam_q1[l], lam_k1[l], lam_q2[l], lam_k2[l]]).astype(F32),
                       ((0, 0), (0, LANES - DQK_A)))
        g_attn = attn_norm_g[l].reshape(1, DV_A).astype(F32)
        wg = jnp.pad(gla_w_gate[l], ((0, LANES - GATE_RANK), (0, 0))).astype(BF16)
        wo = w_o[l].astype(BF16)
        rw = jnp.pad(router_w[l].astype(F32), ((0, 0), (0, LANES - N_EXPERTS)))
        rb = jnp.pad(router_b[l].astype(F32), (0, LANES - N_EXPERTS), constant_values=NEG_INF).reshape(1, LANES)
        ln1 = (ln1_g[l].reshape(1, D_MODEL), ln1_b[l].reshape(1, D_MODEL))

        def mixer_params(lc, n_rows):
            reps = n_rows // lc
            ws = jnp.tile(cmlp_ws[l][:, :lc, :lc], (1, reps, reps))
            bst = jnp.tile(jnp.repeat(cmlp_bs[l][:, :lc].T, DC, axis=1), (reps, 1))
            return (wg, gla_b_gate[l].reshape(1, W_B), _tile_lanes(gla_norm_g[l], HB),
                    _tile_lanes(cmlp_ln_g[l], HC), _tile_lanes(cmlp_ln_b[l], HC), ws, bst)

        q, kp_all, vp_all, kb, vb, g_in, c_in, br = _inproj(hp, w, tabs_p, l, kp_all, vp_all)
        o_a = _attn_prompt(q, kb, vb, lamv, g_attn, bp, lam_init)
        o_bc, st_p = _mixer_prompt(g_in, br, c_in, mixer_params(CMLP_CHUNK, CMLP_CHUNK), bp)
        hp1, hp1k, lg_p = _outproj(o_a, o_bc, hp, wo, *ln1, rw, rb, alpha)
        outs["gp"].append(_diag_states(st_p))

        q, ks_all, vs_all, kb, vb, g_in, c_in, br = _inproj(hs, w, tabs_s, l, ks_all, vs_all)
        o_a = _attn_sample(q, kb, vb, cache_k, cache_v, l, page_table, lamv, g_attn, l_new, lam_init)
        rows_s = min(64, ts)
        lc = min(l_new, CMLP_CHUNK)
        o_bc, vn, st_s = _mixer_sample(g_in, br, c_in, state_gla[l].astype(F32),
                                       mixer_params(lc, rows_s), l_new, rows_s)
        hs1, hs1k, lg_s = _outproj(o_a, o_bc, hs, wo, *ln1, rw, rb, alpha)
        outs["gs"].append(st_s)
        outs["cs"].append(vn.reshape(db, l_new, W_C))

        gate, dest, src_tok, block_e, n_used = _route(jnp.concatenate([lg_p, lg_s], axis=0), bm)
        x_pad = _sc_gather(jnp.concatenate([hp1k, hs1k], axis=0), src_tok)
        y_pad = _moe_experts(x_pad, block_e, n_used, exp_w1, exp_b1.reshape(depth, N_EXPERTS, 1, -1),
                             exp_w2, exp_b2.reshape(depth, N_EXPERTS, 1, -1), l, bm)
        yg = _sc_gather(y_pad, dest.T.reshape(-1))
        hp = _ln2(hp1, gate, yg, 0, ln2_g[l].reshape(1, -1), ln2_b[l].reshape(1, -1), alpha)
        hs = _ln2(hs1, gate, yg, tp, ln2_g[l].reshape(1, -1), ln2_b[l].reshape(1, -1), alpha)

    return (hp.reshape(bp, s_len, D_MODEL), hs.reshape(db, l_new, D_MODEL),
            kp_all.reshape(depth, bp, s_len, HA, DV_A), vp_all.reshape(depth, bp, s_len, HA, DV_A),
            jnp.stack(outs["gp"]),
            ks_all.reshape(depth, db, l_new, HA, DV_A), vs_all.reshape(depth, db, l_new, HA, DV_A),
            jnp.stack(outs["gs"]), jnp.stack(outs["cs"]))
```

```python
import functools
import math

import numpy as np
import jax
import jax.numpy as jnp
from jax import lax
from jax.experimental import pallas as pl
from jax.experimental.pallas import tpu as pltpu
from jax.experimental.pallas import tpu_sc as plsc

F32, BF16 = jnp.float32, jnp.bfloat16
LANES = 128
VMEM_LIMIT = 48 * 1024 * 1024

D_MODEL = 1024
HA, DQK_A, DV_A = 4, 64, 128
ROT_DIM = DQK_A // 4
ROPE_THETA = 500000.0
HB, DK_B, DV_B = 4, 64, 64
GATE_RANK = 16
GATE_NORM = 16.0
HC, DC = 4, 64
CMLP_CHUNK = 128
N_EXPERTS = 32
TOP_K = 4
D_FF = D_MODEL
SWIGLU_LIMIT = 7.0
SWIGLU_ALPHA = 1.702
LN_EPS = 1e-5
RMS_EPS = 1e-6
NEG_INF = -1e30
LOG2E = math.log2(math.e)

W_A = HA * 2 * DQK_A
W_B = HB * DK_B
W_C = HC * DC
COL_G = 3 * W_A
COL_C = COL_G + 4 * W_B
COL_R = COL_C + 2 * W_C
COL_END = COL_R + LANES
GLA_CHUNK_PROMPT = 32
ATTN_BQ, ATTN_BK = 512, 512
ATTN_ROWS = 2 * ATTN_BQ
D_PACK = D_MODEL // 2
MOE_BLOCK = 512
MOE_FF_CHUNK = 512


def _cparams(sem):
    return pltpu.CompilerParams(dimension_semantics=sem, vmem_limit_bytes=VMEM_LIMIT)


def _split3(x):
    hi = x.astype(BF16)
    r = x - hi.astype(F32)
    mid = r.astype(BF16)
    lo = (r - mid.astype(F32)).astype(BF16)
    return hi, mid, lo


def _dot_sel(sel_bf16, x):
    acc = None
    for p in _split3(x):
        d = jnp.dot(sel_bf16, p, preferred_element_type=F32)
        acc = d if acc is None else acc + d
    return acc


def _seg_sum(x, bd_bf16):
    acc = None
    for p in _split3(x):
        d = jnp.dot(p, bd_bf16, preferred_element_type=F32)
        acc = d if acc is None else acc + d
    return acc


def _pack_bf16_pairs(x):
    u = lax.bitcast_convert_type(x, jnp.uint32)
    r = u + (jnp.uint32(0x7FFF) + ((u >> 16) & jnp.uint32(1)))
    w = x.shape[1] // 2
    word = (r[:, :w] & jnp.uint32(0xFFFF0000)) | (r[:, w:] >> 16)
    return lax.bitcast_convert_type(word, F32)


def _unpack_bf16_pairs(words):
    u = lax.bitcast_convert_type(words, jnp.uint32)
    hi = lax.bitcast_convert_type(u & jnp.uint32(0xFFFF0000), F32)
    lo = lax.bitcast_convert_type(u << 16, F32)
    return jnp.concatenate([hi, lo], axis=1)


def _iota2(shape, dim):
    return lax.broadcasted_iota(jnp.int32, shape, dim)


def _idiv(x, n):
    shift = n.bit_length() - 1
    assert n == 1 << shift
    return x >> shift


def _head_blockdiag(n):
    r, c = _iota2((n, n), 0), _iota2((n, n), 1)
    return _idiv(r, DK_B) == _idiv(c, DK_B)


def _chunk_causal(n, chunk):
    r, c = _iota2((n, n), 0), _iota2((n, n), 1)
    return (_idiv(r, chunk) == _idiv(c, chunk)) & (c <= r)


def _ln_rows(x, g, b):
    mu = jnp.mean(x, axis=-1, keepdims=True)
    xc = x - mu
    var = jnp.mean(xc * xc, axis=-1, keepdims=True)
    return xc * lax.rsqrt(var + LN_EPS) * g + b


def _inproj_kernel(x_ref, w_ref, cos_ref, sa_ref, sb_ref, k_all_ref, v_all_ref,
                   q_ref, k_ref, v_ref, kb_ref, vb_ref, g_ref, c_ref, br_ref):
    del k_all_ref, v_all_ref
    xb = x_ref[...].astype(BF16)

    def proj(a, b):
        return jnp.dot(xb, w_ref[:, a:b], preferred_element_type=F32)

    cos, sa, sb = cos_ref[...], sa_ref[...], sb_ref[...]

    def rope(z):
        outs = []
        for i in range(z.shape[1] // LANES):
            zi = z[:, i * LANES:(i + 1) * LANES]
            outs.append(zi * cos + pltpu.roll(zi, LANES - ROT_DIM // 2, 1) * sa
                        + pltpu.roll(zi, ROT_DIM // 2, 1) * sb)
        return jnp.concatenate(outs, axis=1)

    q_ref[...] = (rope(proj(0, W_A)) * (DQK_A ** -0.5 * LOG2E)).astype(BF16)
    k = rope(proj(W_A, 2 * W_A))
    kb_ref[...] = k.astype(BF16)
    v = proj(2 * W_A, 3 * W_A)
    vb_ref[...] = v.astype(BF16)
    for h in range(HA):
        k_ref[:, h, :] = k[:, h * LANES:(h + 1) * LANES]
        v_ref[:, h, :] = v[:, h * LANES:(h + 1) * LANES]
    g_ref[...] = proj(COL_G, COL_C)
    c_ref[...] = proj(COL_C, COL_R)
    br_ref[...] = proj(COL_R, COL_END)


def _inproj(x, w, tabs, layer, k_all, v_all):
    t = x.shape[0]
    tm = min(512, t)
    nt = t // tm
    cos, sa, sb = tabs
    ntab = cos.shape[0] // tm
    row = lambda n: pl.BlockSpec((tm, n), lambda i: (i, 0))
    tab = pl.BlockSpec((tm, LANES), lambda i: (i % ntab, 0))
    heads = pl.BlockSpec((tm, HA, DV_A), lambda i: (layer * nt + i, 0, 0))
    anywhere = pl.BlockSpec(memory_space=pl.ANY)
    shapes = [((W_A,), BF16), None, None, ((W_A,), BF16), ((W_A,), BF16),
              ((4 * W_B,), F32), ((2 * W_C,), F32), ((LANES,), F32)]
    return pl.pallas_call(
        _inproj_kernel,
        grid=(nt,),
        in_specs=[row(D_MODEL), pl.BlockSpec((D_MODEL, COL_END), lambda i: (0, 0)), tab, tab, tab,
                  anywhere, anywhere],
        out_specs=[heads if s is None else row(s[0][0]) for s in shapes],
        out_shape=[jax.ShapeDtypeStruct(k_all.shape, F32) if s is None else jax.ShapeDtypeStruct((t,) + s[0], s[1])
                   for s in shapes],
        input_output_aliases={5: 1, 6: 2},
        compiler_params=_cparams(("parallel",)),
        name="inproj",
    )(x, w, cos, sa, sb, k_all, v_all)


def _diff_lambda(lamv, lam_init):
    a = jnp.sum(lamv[0:1] * lamv[1:2], axis=1, keepdims=True)
    b = jnp.sum(lamv[2:3] * lamv[3:4], axis=1, keepdims=True)
    return jnp.exp(a) - jnp.exp(b) + lam_init


def _diff_finish(o1, o2, lam, g, lam_init):
    o = o1 - lam * o2
    ms = jnp.mean(o * o, axis=-1, keepdims=True)
    return o * lax.rsqrt(ms + RMS_EPS) * g * (1.0 - lam_init)


def _split_maps(q):
    lane = _iota2(q.shape, 1)
    zero = jnp.zeros_like(q)
    return jnp.concatenate([jnp.where(lane < DQK_A, q, zero), jnp.where(lane >= DQK_A, q, zero)], axis=0)


def _attn_prompt_kernel(q_ref, k_ref, v_ref, lamv_ref, g_ref, o_ref, m_sc, l_sc, acc_sc, *, bq, bk, lam_init):
    qi = pl.program_id(2)
    qq = _split_maps(q_ref[...])
    m_sc[...] = jnp.full(m_sc.shape, NEG_INF, F32)
    l_sc[...] = jnp.zeros(l_sc.shape, F32)
    acc_sc[...] = jnp.zeros(acc_sc.shape, F32)

    def step(j, masked):
        start = pl.multiple_of(j * bk, bk)
        k = k_ref[pl.ds(start, bk), :]
        v = v_ref[pl.ds(start, bk), :]
        tiles = range(bk // LANES)
        for rc in range(2 * bq // ATTN_ROWS):
            rs = slice(rc * ATTN_ROWS, (rc + 1) * ATTN_ROWS)
            s = lax.dot_general(qq[rs], k, (((1,), (1,)), ((), ())), preferred_element_type=F32)
            if masked:
                r = (_iota2(s.shape, 0) + rc * ATTN_ROWS) & (bq - 1)
                c = _iota2(s.shape, 1)
                s = jnp.where(c <= r, s, NEG_INF)
            m_prev = m_sc[rs, :]
            s_max = functools.reduce(jnp.maximum, [s[:, i * LANES:(i + 1) * LANES] for i in tiles])
            m_new = jnp.maximum(m_prev, jnp.max(s_max, axis=1, keepdims=True))
            alpha = jnp.exp2(m_prev - m_new)
            p = jnp.exp2(s - jnp.tile(m_new, (1, bk // LANES)))
            p_sum = functools.reduce(jnp.add, [p[:, i * LANES:(i + 1) * LANES] for i in tiles])
            l_sc[rs, :] = alpha * l_sc[rs, :] + jnp.sum(p_sum, axis=1, keepdims=True)
            acc_sc[rs, :] = alpha * acc_sc[rs, :] + jnp.dot(p.astype(BF16), v, preferred_element_type=F32)
            m_sc[rs, :] = m_new

    def body(j, carry):
        step(j, False)
        return carry

    lax.fori_loop(0, qi, body, 0)
    step(qi, True)

    o = acc_sc[...] / l_sc[...]
    lam = _diff_lambda(lamv_ref[...], lam_init)
    o_ref[...] = _diff_finish(o[:bq], o[bq:], lam, g_ref[...], lam_init).astype(o_ref.dtype)


def _attn_prompt(q, kb, vb, lamv, g, nbatch, lam_init):
    t = q.shape[0]
    s = t // nbatch
    bq, bk = min(ATTN_BQ, s), min(ATTN_BK, s)
    assert bq == bk
    blk = bq
    nq = s // blk
    kern = functools.partial(_attn_prompt_kernel, bq=bq, bk=bk, lam_init=lam_init)
    return pl.pallas_call(
        kern,
        grid=(nbatch, HA, nq),
        in_specs=[pl.BlockSpec((blk, LANES), lambda b, h, i: (b * nq + i, h)),
                  pl.BlockSpec((s, LANES), lambda b, h, i: (b, h)),
                  pl.BlockSpec((s, LANES), lambda b, h, i: (b, h)),
                  pl.BlockSpec((4, LANES), lambda b, h, i: (0, 0)),
                  pl.BlockSpec((1, LANES), lambda b, h, i: (0, 0))],
        out_specs=pl.BlockSpec((blk, LANES), lambda b, h, i: (b * nq + i, h)),
        out_shape=jax.ShapeDtypeStruct((t, W_A), BF16),
        scratch_shapes=[pltpu.VMEM((2 * blk, LANES), F32)] * 3,
        compiler_params=_cparams(("parallel", "parallel", "arbitrary")),
        name="attn_prompt",
    )(q, kb, vb, lamv, g)


def _attn_sample_kernel(pt_ref, q_ref, kn_ref, vn_ref, lamv_ref, g_ref, *rest, n_pages, l_new, lam_init):
    del pt_ref
    kp = rest[:n_pages]
    vp = rest[n_pages:2 * n_pages]
    o_ref = rest[2 * n_pages]
    rows_pg = kp[0].shape[0]
    rpad = q_ref.shape[0]
    nq = 2 * rpad
    lam = _diff_lambda(lamv_ref[...], lam_init)
    q = q_ref[...]
    qq = jnp.concatenate([_split_maps(q[:, h * LANES:(h + 1) * LANES]) for h in range(HA)],
                         axis=0).astype(BF16)
    r = _iota2((HA * nq, rows_pg), 0)
    c = _iota2((HA * nq, rows_pg), 1)
    head_ok = (c & (HA - 1)) == _idiv(r, nq)
    new_ok = head_ok & (_idiv(c, HA) <= (r & (rpad - 1))) & (c < l_new * HA)
    nt = (((1,), (1,)), ((), ()))
    zpad = jnp.zeros((rows_pg - kn_ref.shape[0], LANES), F32)
    k_new = jnp.concatenate([kn_ref[...], zpad], axis=0).astype(BF16)
    v_new = jnp.concatenate([vn_ref[...], zpad], axis=0).astype(BF16)
    s_new = jnp.where(new_ok, lax.dot_general(qq, k_new, nt, preferred_element_type=F32), NEG_INF)
    s_past = [jnp.where(head_ok, lax.dot_general(qq, kp[j][...].astype(BF16), nt, preferred_element_type=F32),
                        NEG_INF) for j in range(n_pages)]
    m = jnp.max(s_new, axis=1, keepdims=True)
    for sj in s_past:
        m = jnp.maximum(m, jnp.max(sj, axis=1, keepdims=True))
    p_new = jnp.exp2(s_new - m)
    l = jnp.sum(p_new, axis=1, keepdims=True)
    acc = jnp.dot(p_new.astype(BF16), v_new, preferred_element_type=F32)
    for j in range(n_pages):
        pj = jnp.exp2(s_past[j] - m)
        l = l + jnp.sum(pj, axis=1, keepdims=True)
        acc = acc + jnp.dot(pj.astype(BF16), vp[j][...].astype(BF16), preferred_element_type=F32)
    o = acc / l
    outs = [_diff_finish(o[h * nq:h * nq + rpad], o[h * nq + rpad:(h + 1) * nq], lam, g_ref[...], lam_init)
            for h in range(HA)]
    o_ref[...] = jnp.concatenate(outs, axis=1)


def _attn_sample(q, k, v, cache_k, cache_v, layer, page_table, lamv, g, l_new, lam_init):
    t = q.shape[0]
    db = t // l_new
    n_pages = page_table.shape[1]
    page = cache_k.shape[2]
    rpad = 8

    def pad_rows(a):
        a = a.reshape(db, l_new, W_A).astype(F32)
        return jnp.concatenate([a, jnp.zeros((db, rpad - l_new, W_A), F32)], axis=1)

    def new_rows(a):
        return a.reshape(db, l_new * HA, DV_A).astype(F32)

    depth, n_phys = cache_k.shape[:2]
    cache_k = cache_k.reshape(depth, n_phys, page * HA, DV_A)
    cache_v = cache_v.reshape(depth, n_phys, page * HA, DV_A)
    q_spec = pl.BlockSpec((None, rpad, W_A), lambda b, pt: (b, 0, 0))
    new_spec = pl.BlockSpec((None, l_new * HA, DV_A), lambda b, pt: (b, 0, 0))
    page_specs = [pl.BlockSpec((None, None, page * HA, DV_A),
                               functools.partial(lambda b, pt, j: (layer, pt[b, j], 0, 0), j=j))
                  for j in range(n_pages)]
    kern = functools.partial(_attn_sample_kernel, n_pages=n_pages, l_new=l_new, lam_init=lam_init)
    out = pl.pallas_call(
        kern,
        grid_spec=pltpu.PrefetchScalarGridSpec(
            num_scalar_prefetch=1,
            grid=(db,),
            in_specs=[q_spec, new_spec, new_spec,
                      pl.BlockSpec((4, LANES), lambda b, pt: (0, 0)),
                      pl.BlockSpec((1, LANES), lambda b, pt: (0, 0))] + page_specs + page_specs,
            out_specs=q_spec,
        ),
        out_shape=jax.ShapeDtypeStruct((db, rpad, W_A), F32),
        compiler_params=_cparams(("arbitrary",)),
        name="attn_sample",
    )(page_table, pad_rows(q), new_rows(k), new_rows(v), lamv, g,
      *([cache_k] * n_pages), *([cache_v] * n_pages))
    return out[:, :l_new].reshape(t, W_A).astype(BF16)


def _log_sigmoid(x):
    return jnp.minimum(x, 0.0) - jnp.log1p(jnp.exp(-jnp.abs(x)))


def _gla_gate(br, wg_ref, bgate_ref):
    x = jnp.dot(br.astype(BF16), wg_ref[...], preferred_element_type=F32) + bgate_ref[...]
    return _log_sigmoid(x) / GATE_NORM


def _gla_intra(q_att, k_in, v, chunk):
    keep = _chunk_causal(q_att.shape[0], chunk)
    lane = _idiv(_iota2(q_att.shape, 1), DK_B)
    kb = k_in.astype(BF16)
    atts, vs = [], []
    for h in range(HB):
        qh = jnp.where(lane == h, q_att, 0.0).astype(BF16)
        a = lax.dot_general(qh, kb, (((1,), (1,)), ((), ())), preferred_element_type=F32)
        atts.append(jnp.where(keep, a, 0.0).astype(BF16))
        vs.append(jnp.where(lane == h, v, 0.0).astype(BF16))
    return jnp.dot(jnp.concatenate(atts, axis=1), jnp.concatenate(vs, axis=0), preferred_element_type=F32)


def _gla_finish(o, gate_in, gng_ref, bd):
    ms = _seg_sum(o * o, bd) * (1.0 / DV_B)
    o = o * lax.rsqrt(ms + RMS_EPS) * gng_ref[...]
    return o * (gate_in * (1.0 / (1.0 + jnp.exp(-gate_in))))


def _chunk_mlp(c_in, lng_ref, lnb_ref, ws_ref, bst, chunk, bd):
    n = c_in.shape[0]
    cu, cv = c_in[:, :W_C], c_in[:, W_C:]
    mu = _seg_sum(cv, bd) * (1.0 / DC)
    xc = cv - mu
    var = _seg_sum(xc * xc, bd) * (1.0 / DC)
    vn = xc * lax.rsqrt(var + LN_EPS) * lng_ref[...] + lnb_ref[...]
    keep = _chunk_causal(n, chunk)
    lane = _idiv(_iota2(vn.shape, 1), DC)
    ws, vs = [], []
    for g in range(HC):
        ws.append(jnp.where(keep, ws_ref[g], 0.0).astype(BF16))
        vs.append(jnp.where(lane == g, vn, 0.0).astype(BF16))
    mixed = jnp.dot(jnp.concatenate(ws, axis=1), jnp.concatenate(vs, axis=0), preferred_element_type=F32) + bst
    return cu * mixed, vn


def _mixer_prompt_kernel(g_ref, br_ref, c_ref, wg_ref, bgate_ref, gng_ref, lng_ref, lnb_ref, ws_ref, bst_ref,
                         o_ref, st_ref, st_sc, *, ts, chunk):
    t = pl.program_id(1)

    @pl.when(t == 0)
    def _():
        st_sc[...] = jnp.zeros(st_sc.shape, F32)

    grp = CMLP_CHUNK
    bd = _head_blockdiag(W_B)
    bd_bf = jnp.where(bd, 1.0, 0.0).astype(BF16)
    csum_sel = jnp.where(_chunk_causal(grp, chunk), 1.0, 0.0).astype(BF16)
    rows = _iota2((grp, W_B), 0)
    half = chunk // 2
    for gi in range(ts // grp):
        rs = slice(gi * grp, (gi + 1) * grp)
        g = g_ref[rs, :]
        gq = g[:, 0:W_B] * (DK_B ** -0.5)
        gk, gv, gg = g[:, W_B:2 * W_B], g[:, 2 * W_B:3 * W_B], g[:, 3 * W_B:4 * W_B]
        la = _gla_gate(br_ref[rs, :], wg_ref, bgate_ref)
        bcum = _dot_sel(csum_sel, la)
        mids, lasts = [], []
        for ci in range(grp // chunk):
            mids.append(jnp.broadcast_to(bcum[ci * chunk + half - 1:ci * chunk + half, :], (chunk, W_B)))
            lasts.append(jnp.broadcast_to(bcum[(ci + 1) * chunk - 1:(ci + 1) * chunk, :], (chunk, W_B)))
        bmid = jnp.concatenate(mids, axis=0)
        blast = jnp.concatenate(lasts, axis=0)
        q_att = gq * jnp.exp(bcum - bmid)
        k_in = gk * jnp.exp(bmid - bcum)
        k_end = gk * jnp.exp(blast - bcum)
        q_dec = (gq * jnp.exp(bcum)).astype(BF16)
        o = _gla_intra(q_att, k_in, gv, chunk)
        v_t = gv.T.astype(BF16)
        o_inter = []
        for ci in range(grp // chunk):
            cs = slice(ci * chunk, (ci + 1) * chunk)
            st = st_sc[...]
            o_inter.append(lax.dot_general(q_dec[cs], st.astype(BF16), (((1,), (1,)), ((), ())),
                                           preferred_element_type=F32))
            kem = jnp.where(_idiv(rows, chunk) == ci, k_end, 0.0).astype(BF16)
            upd = jnp.dot(v_t, kem, preferred_element_type=F32)
            dl = jnp.exp(blast[ci * chunk:ci * chunk + 1, :])
            st_sc[...] = st * dl + jnp.where(bd, upd, 0.0)
        o = o + jnp.concatenate(o_inter, axis=0)
        o_b = _gla_finish(o, gg, gng_ref, bd_bf)
        o_c, _ = _chunk_mlp(c_ref[rs, :], lng_ref, lnb_ref, ws_ref, bst_ref[...], CMLP_CHUNK, bd_bf)
        o_ref[rs, :] = jnp.concatenate([o_b, o_c], axis=1).astype(o_ref.dtype)

    @pl.when(t == pl.num_programs(1) - 1)
    def _():
        st_ref[...] = st_sc[...].T


def _mixer_prompt(g_in, br, c_in, prm, nbatch, ts=512):
    t = g_in.shape[0]
    nt = t // nbatch // ts
    row = lambda n: pl.BlockSpec((ts, n), lambda b, i: (b * nt + i, 0))
    full = lambda a: pl.BlockSpec(a.shape, lambda b, i: (0,) * a.ndim)
    kern = functools.partial(_mixer_prompt_kernel, ts=ts, chunk=GLA_CHUNK_PROMPT)
    return pl.pallas_call(
        kern,
        grid=(nbatch, nt),
        in_specs=[row(4 * W_B), row(LANES), row(2 * W_C)] + [full(a) for a in prm],
        out_specs=[row(W_B + W_C), pl.BlockSpec((None, W_B, W_B), lambda b, i: (b, 0, 0))],
        out_shape=[jax.ShapeDtypeStruct((t, W_B + W_C), BF16),
                   jax.ShapeDtypeStruct((nbatch, W_B, W_B), F32)],
        scratch_shapes=[pltpu.VMEM((W_B, W_B), F32)],
        compiler_params=_cparams(("parallel", "arbitrary")),
        name="mixer_prompt",
    )(g_in, br, c_in, *prm)


def _mixer_sample_kernel(g_ref, br_ref, c_ref, s0_ref, wg_ref, bgate_ref, gng_ref, lng_ref, lnb_ref, ws_ref,
                         bst_ref, o_ref, vn_ref, st_ref, *, l_new):
    n = g_ref.shape[0]
    bd = _head_blockdiag(W_B)
    bd_bf = jnp.where(bd, 1.0, 0.0).astype(BF16)
    r, c = _iota2((n, n), 0), _iota2((n, n), 1)
    csum_sel = jnp.where(_chunk_causal(n, l_new), 1.0, 0.0).astype(BF16)
    last_sel = jnp.where(_idiv(r, l_new) == _idiv(c, l_new), 1.0, 0.0).astype(BF16)
    g = g_ref[...]
    gq = g[:, 0:W_B] * (DK_B ** -0.5)
    gk, gv, gg = g[:, W_B:2 * W_B], g[:, 2 * W_B:3 * W_B], g[:, 3 * W_B:4 * W_B]
    la = _gla_gate(br_ref[...], wg_ref, bgate_ref)
    bcum = _dot_sel(csum_sel, la)
    blast = _dot_sel(last_sel, la)
    q_in = gq * jnp.exp(bcum)
    k_in = gk * jnp.exp(-bcum)
    k_end = gk * jnp.exp(blast - bcum)
    o = _gla_intra(q_in, k_in, gv, l_new)
    zrows = jnp.zeros((LANES - n, W_B), F32)
    ke_t = jnp.concatenate([k_end, zrows], axis=0).T
    bl_t = jnp.concatenate([blast, zrows], axis=0).T
    v_pad = jnp.concatenate([gv, zrows], axis=0).astype(BF16)
    rows = _iota2((n, W_B), 0)
    cols = _iota2((W_B, LANES), 1)
    zblk = jnp.zeros((DK_B, DV_B), F32)
    for s in range(n // l_new):
        s0 = jnp.concatenate(
            [jnp.concatenate([s0_ref[s, h] if g == h else zblk for g in range(HB)], axis=1) for h in range(HB)],
            axis=0)
        qs = jnp.where(_idiv(rows, l_new) == s, q_in, 0.0).astype(BF16)
        o = o + jnp.dot(qs, s0.astype(BF16), preferred_element_type=F32)
        kes = jnp.where(_idiv(cols, l_new) == s, ke_t, 0.0).astype(BF16)
        upd = jnp.dot(kes, v_pad, preferred_element_type=F32)
        dl = jnp.exp(bl_t[:, s * l_new:s * l_new + 1])
        fin = s0 * dl + upd
        for h in range(HB):
            st_ref[s, h] = fin[h * DK_B:(h + 1) * DK_B, h * DV_B:(h + 1) * DV_B]
    o_b = _gla_finish(o, gg, gng_ref, bd_bf)
    o_c, vn = _chunk_mlp(c_ref[...], lng_ref, lnb_ref, ws_ref, bst_ref[...], l_new, bd_bf)
    o_ref[...] = jnp.concatenate([o_b, o_c], axis=1).astype(o_ref.dtype)
    vn_ref[...] = vn


def _mixer_sample(g_in, br, c_in, states, layer, prm, l_new, ts=64):
    t = g_in.shape[0]
    ns = ts // l_new
    nblk = t // ts
    row = lambda n: pl.BlockSpec((ts, n), lambda i: (i, 0))
    full = lambda a: pl.BlockSpec(a.shape, lambda i: (0,) * a.ndim)
    st = pl.BlockSpec((ns, HB, DK_B, DV_B), lambda i: (layer * nblk + i, 0, 0, 0))
    kern = functools.partial(_mixer_sample_kernel, l_new=l_new)
    return pl.pallas_call(
        kern,
        grid=(nblk,),
        in_specs=[row(4 * W_B), row(LANES), row(2 * W_C), st] + [full(a) for a in prm],
        out_specs=[row(W_B + W_C), row(W_C), st],
        out_shape=[jax.ShapeDtypeStruct((t, W_B + W_C), BF16),
                   jax.ShapeDtypeStruct((t, W_C), F32),
                   jax.ShapeDtypeStruct(states.shape, F32)],
        input_output_aliases={3: 2},
        compiler_params=_cparams(("parallel",)),
        name="mixer_sample",
    )(g_in, br, c_in, states, *prm)


def _outproj_kernel(oa_ref, obc_ref, x_ref, wo_ref, g_ref, b_ref, rw_ref, rb_ref, *rest, alpha):
    h_ref, hp_ref, lg_ref = rest[-3:]
    y = jnp.dot(oa_ref[...], wo_ref[0:W_A, :], preferred_element_type=F32)
    y = y + jnp.dot(obc_ref[...], wo_ref[W_A:, :], preferred_element_type=F32)
    h = _ln_rows(alpha * x_ref[...] + y, g_ref[...], b_ref[...])
    h_ref[...] = h
    hp_ref[...] = _pack_bf16_pairs(h)
    lg_ref[...] = jnp.dot(h, rw_ref[...], preferred_element_type=F32, precision=lax.Precision.HIGHEST) + rb_ref[...]


def _outproj(o_a, o_bc, x, wo, g, b, rw, rb, alpha, t_all, row0, shared=None):
    t = x.shape[0]
    tm = min(512, t)
    assert row0 % tm == 0
    off = row0 // tm
    row = lambda n: pl.BlockSpec((tm, n), lambda i: (i, 0))
    row_at = lambda n: pl.BlockSpec((tm, n), lambda i: (i + off, 0))
    full = lambda a: pl.BlockSpec(a.shape, lambda i: (0,) * a.ndim)
    in_specs = [row(W_A), row(W_B + W_C), row(D_MODEL)] + [full(a) for a in (wo, g, b, rw, rb)]
    args = [o_a, o_bc, x, wo, g, b, rw, rb]
    aliases = {}
    if shared is not None:
        aliases = {len(args): 1, len(args) + 1: 2}
        in_specs += [pl.BlockSpec(memory_space=pl.ANY)] * 2
        args += list(shared)
    return pl.pallas_call(
        functools.partial(_outproj_kernel, alpha=alpha),
        grid=(t // tm,),
        in_specs=in_specs,
        out_specs=[row(D_MODEL), row_at(D_PACK), row_at(LANES)],
        out_shape=[jax.ShapeDtypeStruct((t, D_MODEL), F32), jax.ShapeDtypeStruct((t_all, D_PACK), F32),
                   jax.ShapeDtypeStruct((t_all, LANES), F32)],
        input_output_aliases=aliases,
        compiler_params=_cparams(("parallel",)),
        name="outproj",
    )(*args)


def _moe_kernel(be_ref, nu_ref, x_ref, w1_ref, b1_ref, w2_ref, b2_ref, y_ref, w1b_sc, w2b_sc):
    i = pl.program_id(0)

    @pl.when((i == 0) | (be_ref[i] != be_ref[jnp.maximum(i - 1, 0)]))
    def _():
        w1b_sc[...] = w1_ref[...].astype(BF16)
        w2b_sc[...] = w2_ref[...].astype(BF16)

    @pl.when(i >= nu_ref[0])
    def _():
        y_ref[...] = jnp.zeros(y_ref.shape, F32)

    @pl.when(i < nu_ref[0])
    def _():
        xb = _unpack_bf16_pairs(x_ref[...]).astype(BF16)
        acc = None
        for c in range(D_FF // MOE_FF_CHUNK):
            gs = slice(c * MOE_FF_CHUNK, (c + 1) * MOE_FF_CHUNK)
            us = slice(D_FF + c * MOE_FF_CHUNK, D_FF + (c + 1) * MOE_FF_CHUNK)
            g = jnp.dot(xb, w1b_sc[:, gs], preferred_element_type=F32) + b1_ref[:, gs]
            u = jnp.dot(xb, w1b_sc[:, us], preferred_element_type=F32) + b1_ref[:, us]
            g = jnp.minimum(g, SWIGLU_LIMIT)
            u = jnp.clip(u, -SWIGLU_LIMIT, SWIGLU_LIMIT)
            act = (u + 1.0) * g * (1.0 / (1.0 + jnp.exp(-SWIGLU_ALPHA * g)))
            part = jnp.dot(act.astype(BF16), w2b_sc[gs, :], preferred_element_type=F32)
            acc = part if acc is None else acc + part
        y_ref[...] = _pack_bf16_pairs(acc + b2_ref[...])


def _moe_experts(x_pad, block_e, n_used, w1, b1, w2, b2, layer, bm):
    nb = x_pad.shape[0] // bm
    return pl.pallas_call(
        _moe_kernel,
        grid_spec=pltpu.PrefetchScalarGridSpec(
            num_scalar_prefetch=2,
            grid=(nb,),
            in_specs=[pl.BlockSpec((bm, D_PACK), lambda i, be, nu: (i, 0)),
                      pl.BlockSpec((None, None, D_MODEL, 2 * D_FF), lambda i, be, nu: (layer, be[i], 0, 0)),
                      pl.BlockSpec((None, None, 1, 2 * D_FF), lambda i, be, nu: (layer, be[i], 0, 0)),
                      pl.BlockSpec((None, None, D_FF, D_MODEL), lambda i, be, nu: (layer, be[i], 0, 0)),
                      pl.BlockSpec((None, None, 1, D_MODEL), lambda i, be, nu: (layer, be[i], 0, 0))],
            out_specs=pl.BlockSpec((bm, D_PACK), lambda i, be, nu: (i, 0)),
            scratch_shapes=[pltpu.VMEM((D_MODEL, 2 * D_FF), BF16), pltpu.VMEM((D_FF, D_MODEL), BF16)],
        ),
        out_shape=jax.ShapeDtypeStruct((nb * bm, D_PACK), F32),
        compiler_params=_cparams(("arbitrary",)),
        name="moe_experts",
    )(block_e, n_used, x_pad, w1, b1, w2, b2)


ROUTE_E, ROUTE_RANK, ROUTE_GATE = 0, TOP_K, 2 * TOP_K


def _router_kernel(lg_ref, route_ref, route_t_ref, cnt_ref, cnt_sc):
    @pl.when(pl.program_id(0) == 0)
    def _():
        cnt_sc[...] = jnp.zeros(cnt_sc.shape, F32)

    work = lg_ref[...]
    tm = work.shape[0]
    lane = _iota2(work.shape, 1)
    lane_f = lane.astype(F32)
    sels, vals, ids = [], [], []
    for _ in range(TOP_K):
        mx = jnp.max(work, axis=1, keepdims=True)
        idx = jnp.min(jnp.where(work == mx, lane_f, float(LANES)), axis=1, keepdims=True)
        sel = lane_f == idx
        sels.append(sel)
        vals.append(mx)
        ids.append(idx)
        work = jnp.where(sel, -jnp.inf, work)
    ex = [jnp.exp(v - vals[0]) for v in vals]
    den = ex[0]
    for x in ex[1:]:
        den = den + x
    picked = jnp.zeros(work.shape, F32)
    for sel in sels:
        picked = jnp.where(sel, 1.0, picked)
    r, c = _iota2((tm, tm), 0), _iota2((tm, tm), 1)
    before = jnp.dot(jnp.where(c < r, 1.0, 0.0).astype(BF16), picked.astype(BF16), preferred_element_type=F32)
    before = before + cnt_sc[...]
    out = jnp.zeros(work.shape, F32)
    for k in range(TOP_K):
        rank = jnp.sum(jnp.where(sels[k], before, 0.0), axis=1, keepdims=True)
        out = jnp.where(lane == ROUTE_E + k, ids[k], out)
        out = jnp.where(lane == ROUTE_RANK + k, rank, out)
        out = jnp.where(lane == ROUTE_GATE + k, ex[k] / den, out)
    route_ref[...] = out
    route_t_ref[...] = out.T[0:ROUTE_GATE, :]
    cnt_sc[...] = cnt_sc[...] + jnp.sum(picked, axis=0, keepdims=True)
    cnt_ref[...] = cnt_sc[...]


def _router(logits):
    t = logits.shape[0]
    tm = min(512, t)
    return pl.pallas_call(
        _router_kernel,
        grid=(t // tm,),
        in_specs=[pl.BlockSpec((tm, LANES), lambda i: (i, 0))],
        out_specs=[pl.BlockSpec((tm, LANES), lambda i: (i, 0)), pl.BlockSpec((ROUTE_GATE, tm), lambda i: (0, i)),
                   pl.BlockSpec((1, LANES), lambda i: (0, 0))],
        out_shape=[jax.ShapeDtypeStruct((t, LANES), F32), jax.ShapeDtypeStruct((ROUTE_GATE, t), F32),
                   jax.ShapeDtypeStruct((1, LANES), F32)],
        scratch_shapes=[pltpu.VMEM((1, LANES), F32)],
        compiler_params=_cparams(("arbitrary",)),
        name="router",
    )(logits)


def _route(logits, bm):
    t = logits.shape[0]
    m = t * TOP_K
    route, route_t, cnt = _router(logits)
    e_t = route_t[ROUTE_E:ROUTE_E + TOP_K].astype(jnp.int32)
    rank_t = route_t[ROUTE_RANK:ROUTE_RANK + TOP_K].astype(jnp.int32)
    counts = cnt[0, :N_EXPERTS].astype(jnp.int32)
    padded = ((counts + bm - 1) // bm) * bm
    pad_end = jnp.cumsum(padded)
    pad_start = pad_end - padded
    experts = jnp.arange(N_EXPERTS)[:, None, None]
    dest = jnp.sum(jnp.where(e_t[None] == experts, pad_start[:, None, None], 0), axis=0) + rank_t
    nb = -(-m // bm) + N_EXPERTS
    block_e = jnp.sum((pad_end[None, :] <= (jnp.arange(nb) * bm)[:, None]).astype(jnp.int32), axis=1)
    block_e = jnp.minimum(block_e, N_EXPERTS - 1)
    n_used = (pad_end[-1] // bm).astype(jnp.int32).reshape(1)
    block_e = jnp.where(jnp.arange(nb) < n_used[0], block_e, block_e[jnp.maximum(n_used[0] - 1, 0)])
    sort_key = (e_t * t + jnp.arange(t, dtype=jnp.int32)[None, :]).reshape(m)
    sorted_tok = (jnp.argsort(sort_key) % t).astype(jnp.int32)
    srt_start = jnp.cumsum(counts) - counts
    slot = jnp.arange(nb * bm, dtype=jnp.int32).reshape(nb, bm)
    shift = (srt_start - pad_start)[block_e][:, None]
    valid = slot < (pad_start + counts)[block_e][:, None]
    src_tok = jnp.where(valid, sorted_tok[jnp.clip(slot + shift, 0, m - 1)], slot % t).reshape(nb * bm)
    return route, dest.reshape(m), src_tok, block_e.astype(jnp.int32), n_used


def _ln2_kernel(h_ref, gate_ref, *rest, alpha):
    y_refs, (g_ref, b_ref, o_ref) = rest[:TOP_K], rest[TOP_K:]
    gate = gate_ref[...]
    x = alpha * h_ref[...]
    for k in range(TOP_K):
        x = x + gate[:, ROUTE_GATE + k:ROUTE_GATE + k + 1] * _unpack_bf16_pairs(y_refs[k][...])
    o_ref[...] = _ln_rows(x, g_ref[...], b_ref[...])


def _ln2(h, gate, yg, row0, g, b, alpha):
    t = h.shape[0]
    t_all = yg.shape[0] // TOP_K
    tm = min(512, t)
    assert row0 % tm == 0 and t_all % tm == 0
    off = row0 // tm
    nt_all = t_all // tm
    y_specs = [pl.BlockSpec((tm, D_PACK), functools.partial(lambda i, k: (k * nt_all + off + i, 0), k=k))
               for k in range(TOP_K)]
    return pl.pallas_call(
        functools.partial(_ln2_kernel, alpha=alpha),
        grid=(t // tm,),
        in_specs=[pl.BlockSpec((tm, D_MODEL), lambda i: (i, 0)),
                  pl.BlockSpec((tm, LANES), lambda i: (i + off, 0))] + y_specs +
                 [pl.BlockSpec((1, D_MODEL), lambda i: (0, 0)),
                  pl.BlockSpec((1, D_MODEL), lambda i: (0, 0))],
        out_specs=pl.BlockSpec((tm, D_MODEL), lambda i: (i, 0)),
        out_shape=jax.ShapeDtypeStruct((t, D_MODEL), F32),
        compiler_params=_cparams(("parallel",)),
        name="ln2",
    )(h, gate, *([yg] * TOP_K), g, b)


SC_CORES, SC_SUBCORES = 2, 16
SC_CHUNK = 64


def _sc_gather(table, idx):
    b, d = idx.shape[0], table.shape[1]
    workers = SC_CORES * SC_SUBCORES
    per_w = b // workers
    assert per_w * workers == b and per_w % SC_CHUNK == 0
    mesh = plsc.VectorSubcoreMesh(core_axis_name="c", subcore_axis_name="s")

    n_chunks = per_w // SC_CHUNK

    @functools.partial(
        pl.kernel, mesh=mesh, out_type=jax.ShapeDtypeStruct((b, d), table.dtype),
        scratch_types=[pltpu.VMEM((per_w,), jnp.int32), pltpu.VMEM((2, SC_CHUNK, d), table.dtype),
                       pltpu.SemaphoreType.DMA((2,))],
        name="sc_gather")
    def gather(table_hbm, idx_hbm, out_hbm, idx_v, rows_v, sems):
        wid = lax.axis_index("s") * SC_CORES + lax.axis_index("c")
        pltpu.sync_copy(idx_hbm.at[pl.ds(wid * per_w, per_w)], idx_v)

        def fetch(i, slot):
            return pltpu.make_async_copy(table_hbm.at[idx_v.at[pl.ds(i * SC_CHUNK, SC_CHUNK)]],
                                         rows_v.at[slot], sems.at[slot])

        fetch(0, 0).start()

        @pl.loop(0, n_chunks)
        def _(i):
            slot = lax.rem(i, 2)

            @pl.when(i + 1 < n_chunks)
            def _():
                fetch(i + 1, 1 - slot).start()

            fetch(i, slot).wait()
            pltpu.sync_copy(rows_v.at[slot], out_hbm.at[pl.ds(wid * per_w + i * SC_CHUNK, SC_CHUNK)])

    return gather(table, idx)


def _rope_tables(pos):
    half = ROT_DIM // 2
    inv_freq = ROPE_THETA ** (-jnp.arange(0, ROT_DIM, 2, dtype=F32) / ROT_DIM)
    ang = pos.astype(F32)[:, None] * inv_freq[None, :]
    cos, sin = jnp.cos(ang), jnp.sin(ang)
    m = np.arange(LANES) % DQK_A
    idx = m % half
    cos_l = jnp.where(m < ROT_DIM, cos[:, idx], 1.0)
    sa = jnp.where(m < half, -sin[:, idx], 0.0)
    sb = jnp.where((m >= half) & (m < ROT_DIM), sin[:, idx], 0.0)
    return cos_l, sa, sb


def _prep_w_in(w):
    r0 = COL_C
    r1 = r0 + GATE_RANK
    pad = jnp.zeros((w.shape[0], LANES - GATE_RANK), w.dtype)
    return jnp.concatenate([w[:, :r0], w[:, r1:], w[:, r0:r1], pad], axis=1).astype(BF16)


def _tile_lanes(v, reps):
    return jnp.tile(v.reshape(1, -1), (1, reps)).astype(F32)


def _blockdiag_states(s):
    n = s.shape[0]
    eye = jnp.eye(HB, dtype=s.dtype)
    return jnp.einsum('nhde,hg->nhdge', s, eye).reshape(n, HB * DK_B, HB * DV_B)


def _diag_states(sbd):
    n = sbd.shape[0]
    s = sbd.reshape(n, HB, DK_B, HB, DV_B)
    return jnp.stack([s[:, h, :, h, :] for h in range(HB)], axis=1)


def kernel(x_prompt, x_sample, cache_k, cache_v, page_table, state_gla, w_in, lam_q1, lam_k1, lam_q2, lam_k2, attn_norm_g, gla_w_gate, gla_b_gate, gla_norm_g, cmlp_ln_g, cmlp_ln_b, cmlp_ws, cmlp_bs, w_o, ln1_g, ln1_b, router_w, router_b, exp_w1, exp_b1, exp_w2, exp_b2, ln2_g, ln2_b):
    depth = w_in.shape[0]
    bp, s_len, _ = x_prompt.shape
    db, l_new, _ = x_sample.shape
    n_phys, page = cache_k.shape[1], cache_k.shape[2]
    past_len = page_table.shape[1] * page
    alpha = (2 * depth) ** 0.25
    tp, ts = bp * s_len, db * l_new
    bm = MOE_BLOCK

    tabs_p = _rope_tables(jnp.arange(s_len))
    tabs_s = _rope_tables(past_len + (jnp.arange(ts) % l_new))
    page_table = page_table.astype(jnp.int32)

    hp = x_prompt.reshape(tp, D_MODEL)
    hs = x_sample.reshape(ts, D_MODEL)
    outs = {k: [] for k in ("gp", "cs")}
    gs_all = state_gla.astype(F32).reshape(depth * db, HB, DK_B, DV_B)
    kp_all, vp_all = (jnp.zeros((depth * tp, HA, DV_A), F32) for _ in range(2))
    ks_all, vs_all = (jnp.zeros((depth * ts, HA, DV_A), F32) for _ in range(2))
    for l in range(depth):
        lam_init = 0.8 - 0.6 * math.exp(-0.3 * l)
        w = _prep_w_in(w_in[l])
        lamv = jnp.pad(jnp.stack([lam_q1[l], lam_k1[l], lam_q2[l], lam_k2[l]]).astype(F32),
                       ((0, 0), (0, LANES - DQK_A)))
        g_attn = attn_norm_g[l].reshape(1, DV_A).astype(F32)
        wg = jnp.pad(gla_w_gate[l], ((0, LANES - GATE_RANK), (0, 0))).astype(BF16)
        wo = w_o[l].astype(BF16)
        rw = jnp.pad(router_w[l].astype(F32), ((0, 0), (0, LANES - N_EXPERTS)))
        rb = jnp.pad(router_b[l].astype(F32), (0, LANES - N_EXPERTS), constant_values=NEG_INF).reshape(1, LANES)
        ln1 = (ln1_g[l].reshape(1, D_MODEL), ln1_b[l].reshape(1, D_MODEL))

        def mixer_params(lc, n_rows):
            reps = n_rows // lc
            ws = jnp.tile(cmlp_ws[l][:, :lc, :lc], (1, reps, reps))
            bst = jnp.tile(jnp.repeat(cmlp_bs[l][:, :lc].T, DC, axis=1), (reps, 1))
            return (wg, gla_b_gate[l].reshape(1, W_B), _tile_lanes(gla_norm_g[l], HB),
                    _tile_lanes(cmlp_ln_g[l], HC), _tile_lanes(cmlp_ln_b[l], HC), ws, bst)

        q, kp_all, vp_all, kb, vb, g_in, c_in, br = _inproj(hp, w, tabs_p, l, kp_all, vp_all)
        o_a = _attn_prompt(q, kb, vb, lamv, g_attn, bp, lam_init)
        o_bc, st_p = _mixer_prompt(g_in, br, c_in, mixer_params(CMLP_CHUNK, CMLP_CHUNK), bp)
        hp1, rows_k, logits = _outproj(o_a, o_bc, hp, wo, *ln1, rw, rb, alpha, tp + ts, 0)
        outs["gp"].append(_diag_states(st_p))

        q, ks_all, vs_all, kb, vb, g_in, c_in, br = _inproj(hs, w, tabs_s, l, ks_all, vs_all)
        o_a = _attn_sample(q, kb, vb, cache_k, cache_v, l, page_table, lamv, g_attn, l_new, lam_init)
        rows_s = min(64, ts)
        lc = min(l_new, CMLP_CHUNK)
        o_bc, vn, gs_all = _mixer_sample(g_in, br, c_in, gs_all, l, mixer_params(lc, rows_s), l_new, rows_s)
        hs1, rows_k, logits = _outproj(o_a, o_bc, hs, wo, *ln1, rw, rb, alpha, tp + ts, tp, (rows_k, logits))
        outs["cs"].append(vn.reshape(db, l_new, W_C))

        gate, dest, src_tok, block_e, n_used = _route(logits, bm)
        x_pad = _sc_gather(rows_k, src_tok)
        y_pad = _moe_experts(x_pad, block_e, n_used, exp_w1, exp_b1.reshape(depth, N_EXPERTS, 1, -1),
                             exp_w2, exp_b2.reshape(depth, N_EXPERTS, 1, -1), l, bm)
        yg = _sc_gather(y_pad, dest)
        hp = _ln2(hp1, gate, yg, 0, ln2_g[l].reshape(1, -1), ln2_b[l].reshape(1, -1), alpha)
        hs = _ln2(hs1, gate, yg, tp, ln2_g[l].reshape(1, -1), ln2_b[l].reshape(1, -1), alpha)

    return (hp.reshape(bp, s_len, D_MODEL), hs.reshape(db, l_new, D_MODEL),
            kp_all.reshape(depth, bp, s_len, HA, DV_A), vp_all.reshape(depth, bp, s_len, HA, DV_A),
            jnp.stack(outs["gp"]),
            ks_all.reshape(depth, db, l_new, HA, DV_A), vs_all.reshape(depth, db, l_new, HA, DV_A),
            gs_all.reshape(depth, db, HB, DK_B, DV_B), jnp.stack(outs["cs"]))
```

```python
import functools
import math

import numpy as np
import jax
import jax.numpy as jnp
from jax import lax
from jax.experimental import pallas as pl
from jax.experimental.pallas import tpu as pltpu
from jax.experimental.pallas import tpu_sc as plsc

F32, BF16 = jnp.float32, jnp.bfloat16
LANES = 128
VMEM_LIMIT = 48 * 1024 * 1024

D_MODEL = 1024
HA, DQK_A, DV_A = 4, 64, 128
ROT_DIM = DQK_A // 4
ROPE_THETA = 500000.0
HB, DK_B, DV_B = 4, 64, 64
GATE_RANK = 16
GATE_NORM = 16.0
HC, DC = 4, 64
CMLP_CHUNK = 128
N_EXPERTS = 32
TOP_K = 4
D_FF = D_MODEL
SWIGLU_LIMIT = 7.0
SWIGLU_ALPHA = 1.702
LN_EPS = 1e-5
RMS_EPS = 1e-6
NEG_INF = -1e30
LOG2E = math.log2(math.e)

W_A = HA * 2 * DQK_A
W_B = HB * DK_B
W_C = HC * DC
COL_G = 3 * W_A
COL_C = COL_G + 4 * W_B
COL_R = COL_C + 2 * W_C
COL_END = COL_R + LANES
GLA_CHUNK_PROMPT = 32
ATTN_BQ, ATTN_BK = 512, 512
ATTN_ROWS = 2 * ATTN_BQ
D_PACK = D_MODEL // 2
MOE_BLOCK = 512
MOE_FF_CHUNK = 512


def _cparams(sem):
    return pltpu.CompilerParams(dimension_semantics=sem, vmem_limit_bytes=VMEM_LIMIT)


def _split3(x):
    hi = x.astype(BF16)
    r = x - hi.astype(F32)
    mid = r.astype(BF16)
    lo = (r - mid.astype(F32)).astype(BF16)
    return hi, mid, lo


def _dot_sel(sel_bf16, x):
    acc = None
    for p in _split3(x):
        d = jnp.dot(sel_bf16, p, preferred_element_type=F32)
        acc = d if acc is None else acc + d
    return acc


def _seg_sum(x, bd_bf16):
    acc = None
    for p in _split3(x):
        d = jnp.dot(p, bd_bf16, preferred_element_type=F32)
        acc = d if acc is None else acc + d
    return acc


def _pack_bf16_pairs(x):
    u = lax.bitcast_convert_type(x, jnp.uint32)
    r = u + (jnp.uint32(0x7FFF) + ((u >> 16) & jnp.uint32(1)))
    w = x.shape[1] // 2
    word = (r[:, :w] & jnp.uint32(0xFFFF0000)) | (r[:, w:] >> 16)
    return lax.bitcast_convert_type(word, F32)


def _unpack_bf16_pairs(words):
    u = lax.bitcast_convert_type(words, jnp.uint32)
    hi = lax.bitcast_convert_type(u & jnp.uint32(0xFFFF0000), F32)
    lo = lax.bitcast_convert_type(u << 16, F32)
    return jnp.concatenate([hi, lo], axis=1)


def _iota2(shape, dim):
    return lax.broadcasted_iota(jnp.int32, shape, dim)


def _idiv(x, n):
    shift = n.bit_length() - 1
    assert n == 1 << shift
    return x >> shift


def _head_blockdiag(n):
    r, c = _iota2((n, n), 0), _iota2((n, n), 1)
    return _idiv(r, DK_B) == _idiv(c, DK_B)


def _chunk_causal(n, chunk):
    r, c = _iota2((n, n), 0), _iota2((n, n), 1)
    return (_idiv(r, chunk) == _idiv(c, chunk)) & (c <= r)


def _ln_rows(x, g, b):
    mu = jnp.mean(x, axis=-1, keepdims=True)
    xc = x - mu
    var = jnp.mean(xc * xc, axis=-1, keepdims=True)
    return xc * lax.rsqrt(var + LN_EPS) * g + b


def _inproj_kernel(x_ref, w_ref, cos_ref, sa_ref, sb_ref, k_all_ref, v_all_ref,
                   q_ref, k_ref, v_ref, kb_ref, vb_ref, g_ref, c_ref, br_ref):
    del k_all_ref, v_all_ref
    xb = x_ref[...].astype(BF16)

    def proj(a, b):
        return jnp.dot(xb, w_ref[:, a:b], preferred_element_type=F32)

    cos, sa, sb = cos_ref[...], sa_ref[...], sb_ref[...]

    def rope(z):
        outs = []
        for i in range(z.shape[1] // LANES):
            zi = z[:, i * LANES:(i + 1) * LANES]
            outs.append(zi * cos + pltpu.roll(zi, LANES - ROT_DIM // 2, 1) * sa
                        + pltpu.roll(zi, ROT_DIM // 2, 1) * sb)
        return jnp.concatenate(outs, axis=1)

    q_ref[...] = (rope(proj(0, W_A)) * (DQK_A ** -0.5 * LOG2E)).astype(BF16)
    k = rope(proj(W_A, 2 * W_A))
    kb_ref[...] = k.astype(BF16)
    v = proj(2 * W_A, 3 * W_A)
    vb_ref[...] = v.astype(BF16)
    for h in range(HA):
        k_ref[:, h, :] = k[:, h * LANES:(h + 1) * LANES]
        v_ref[:, h, :] = v[:, h * LANES:(h + 1) * LANES]
    g_ref[...] = proj(COL_G, COL_C)
    c_ref[...] = proj(COL_C, COL_R)
    br_ref[...] = proj(COL_R, COL_END)


def _inproj(x, w, tabs, layer, k_all, v_all):
    t = x.shape[0]
    tm = min(512, t)
    nt = t // tm
    cos, sa, sb = tabs
    ntab = cos.shape[0] // tm
    row = lambda n: pl.BlockSpec((tm, n), lambda i: (i, 0))
    tab = pl.BlockSpec((tm, LANES), lambda i: (i % ntab, 0))
    heads = pl.BlockSpec((tm, HA, DV_A), lambda i: (layer * nt + i, 0, 0))
    anywhere = pl.BlockSpec(memory_space=pl.ANY)
    shapes = [((W_A,), BF16), None, None, ((W_A,), BF16), ((W_A,), BF16),
              ((4 * W_B,), F32), ((2 * W_C,), F32), ((LANES,), F32)]
    return pl.pallas_call(
        _inproj_kernel,
        grid=(nt,),
        in_specs=[row(D_MODEL), pl.BlockSpec((D_MODEL, COL_END), lambda i: (0, 0)), tab, tab, tab,
                  anywhere, anywhere],
        out_specs=[heads if s is None else row(s[0][0]) for s in shapes],
        out_shape=[jax.ShapeDtypeStruct(k_all.shape, F32) if s is None else jax.ShapeDtypeStruct((t,) + s[0], s[1])
                   for s in shapes],
        input_output_aliases={5: 1, 6: 2},
        compiler_params=_cparams(("parallel",)),
        name="inproj",
    )(x, w, cos, sa, sb, k_all, v_all)


def _diff_lambda(lamv, lam_init):
    a = jnp.sum(lamv[0:1] * lamv[1:2], axis=1, keepdims=True)
    b = jnp.sum(lamv[2:3] * lamv[3:4], axis=1, keepdims=True)
    return jnp.exp(a) - jnp.exp(b) + lam_init


def _diff_finish(o1, o2, lam, g, lam_init):
    o = o1 - lam * o2
    ms = jnp.mean(o * o, axis=-1, keepdims=True)
    return o * lax.rsqrt(ms + RMS_EPS) * g * (1.0 - lam_init)


def _split_maps(q):
    lane = _iota2(q.shape, 1)
    zero = jnp.zeros_like(q)
    return jnp.concatenate([jnp.where(lane < DQK_A, q, zero), jnp.where(lane >= DQK_A, q, zero)], axis=0)


def _attn_prompt_kernel(q_ref, k_ref, v_ref, lamv_ref, g_ref, o_ref, m_sc, l_sc, acc_sc, *, bq, bk, lam_init):
    qi = pl.program_id(2)
    qq = _split_maps(q_ref[...])
    m_sc[...] = jnp.full(m_sc.shape, NEG_INF, F32)
    l_sc[...] = jnp.zeros(l_sc.shape, F32)
    acc_sc[...] = jnp.zeros(acc_sc.shape, F32)

    def step(j, masked):
        start = pl.multiple_of(j * bk, bk)
        k = k_ref[pl.ds(start, bk), :]
        v = v_ref[pl.ds(start, bk), :]
        tiles = range(bk // LANES)
        for rc in range(2 * bq // ATTN_ROWS):
            rs = slice(rc * ATTN_ROWS, (rc + 1) * ATTN_ROWS)
            s = lax.dot_general(qq[rs], k, (((1,), (1,)), ((), ())), preferred_element_type=F32)
            if masked:
                r = (_iota2(s.shape, 0) + rc * ATTN_ROWS) & (bq - 1)
                c = _iota2(s.shape, 1)
                s = jnp.where(c <= r, s, NEG_INF)
            m_prev = m_sc[rs, :]
            s_max = functools.reduce(jnp.maximum, [s[:, i * LANES:(i + 1) * LANES] for i in tiles])
            m_new = jnp.maximum(m_prev, jnp.max(s_max, axis=1, keepdims=True))
            alpha = jnp.exp2(m_prev - m_new)
            p = jnp.exp2(s - jnp.tile(m_new, (1, bk // LANES)))
            p_sum = functools.reduce(jnp.add, [p[:, i * LANES:(i + 1) * LANES] for i in tiles])
            l_sc[rs, :] = alpha * l_sc[rs, :] + jnp.sum(p_sum, axis=1, keepdims=True)
            acc_sc[rs, :] = alpha * acc_sc[rs, :] + jnp.dot(p.astype(BF16), v, preferred_element_type=F32)
            m_sc[rs, :] = m_new

    def body(j, carry):
        step(j, False)
        return carry

    lax.fori_loop(0, qi, body, 0)
    step(qi, True)

    o = acc_sc[...] / l_sc[...]
    lam = _diff_lambda(lamv_ref[...], lam_init)
    o_ref[...] = _diff_finish(o[:bq], o[bq:], lam, g_ref[...], lam_init).astype(o_ref.dtype)


def _attn_prompt(q, kb, vb, lamv, g, nbatch, lam_init):
    t = q.shape[0]
    s = t // nbatch
    bq, bk = min(ATTN_BQ, s), min(ATTN_BK, s)
    assert bq == bk
    blk = bq
    nq = s // blk
    kern = functools.partial(_attn_prompt_kernel, bq=bq, bk=bk, lam_init=lam_init)
    return pl.pallas_call(
        kern,
        grid=(nbatch, HA, nq),
        in_specs=[pl.BlockSpec((blk, LANES), lambda b, h, i: (b * nq + i, h)),
                  pl.BlockSpec((s, LANES), lambda b, h, i: (b, h)),
                  pl.BlockSpec((s, LANES), lambda b, h, i: (b, h)),
                  pl.BlockSpec((4, LANES), lambda b, h, i: (0, 0)),
                  pl.BlockSpec((1, LANES), lambda b, h, i: (0, 0))],
        out_specs=pl.BlockSpec((blk, LANES), lambda b, h, i: (b * nq + i, h)),
        out_shape=jax.ShapeDtypeStruct((t, W_A), BF16),
        scratch_shapes=[pltpu.VMEM((2 * blk, LANES), F32)] * 3,
        compiler_params=_cparams(("parallel", "parallel", "arbitrary")),
        name="attn_prompt",
    )(q, kb, vb, lamv, g)


def _attn_sample_kernel(pt_ref, q_ref, kn_ref, vn_ref, lamv_ref, g_ref, *rest, n_pages, l_new, lam_init):
    del pt_ref
    kp = rest[:n_pages]
    vp = rest[n_pages:2 * n_pages]
    o_ref = rest[2 * n_pages]
    rows_pg = kp[0].shape[0]
    rpad = q_ref.shape[0]
    nq = 2 * rpad
    lam = _diff_lambda(lamv_ref[...], lam_init)
    q = q_ref[...]
    qq = jnp.concatenate([_split_maps(q[:, h * LANES:(h + 1) * LANES]) for h in range(HA)],
                         axis=0).astype(BF16)
    r = _iota2((HA * nq, rows_pg), 0)
    c = _iota2((HA * nq, rows_pg), 1)
    head_ok = (c & (HA - 1)) == _idiv(r, nq)
    new_ok = head_ok & (_idiv(c, HA) <= (r & (rpad - 1))) & (c < l_new * HA)
    nt = (((1,), (1,)), ((), ()))
    zpad = jnp.zeros((rows_pg - kn_ref.shape[0], LANES), F32)
    k_new = jnp.concatenate([kn_ref[...], zpad], axis=0).astype(BF16)
    v_new = jnp.concatenate([vn_ref[...], zpad], axis=0).astype(BF16)
    s_new = jnp.where(new_ok, lax.dot_general(qq, k_new, nt, preferred_element_type=F32), NEG_INF)
    s_past = [jnp.where(head_ok, lax.dot_general(qq, kp[j][...].astype(BF16), nt, preferred_element_type=F32),
                        NEG_INF) for j in range(n_pages)]
    m = jnp.max(s_new, axis=1, keepdims=True)
    for sj in s_past:
        m = jnp.maximum(m, jnp.max(sj, axis=1, keepdims=True))
    p_new = jnp.exp2(s_new - m)
    l = jnp.sum(p_new, axis=1, keepdims=True)
    acc = jnp.dot(p_new.astype(BF16), v_new, preferred_element_type=F32)
    for j in range(n_pages):
        pj = jnp.exp2(s_past[j] - m)
        l = l + jnp.sum(pj, axis=1, keepdims=True)
        acc = acc + jnp.dot(pj.astype(BF16), vp[j][...].astype(BF16), preferred_element_type=F32)
    o = acc / l
    outs = [_diff_finish(o[h * nq:h * nq + rpad], o[h * nq + rpad:(h + 1) * nq], lam, g_ref[...], lam_init)
            for h in range(HA)]
    o_ref[...] = jnp.concatenate(outs, axis=1)


def _attn_sample(q, k, v, cache_k, cache_v, layer, page_table, lamv, g, l_new, lam_init):
    t = q.shape[0]
    db = t // l_new
    n_pages = page_table.shape[1]
    page = cache_k.shape[2]
    rpad = 8

    def pad_rows(a):
        a = a.reshape(db, l_new, W_A).astype(F32)
        return jnp.concatenate([a, jnp.zeros((db, rpad - l_new, W_A), F32)], axis=1)

    def new_rows(a):
        return a.reshape(db, l_new * HA, DV_A).astype(F32)

    depth, n_phys = cache_k.shape[:2]
    cache_k = cache_k.reshape(depth, n_phys, page * HA, DV_A)
    cache_v = cache_v.reshape(depth, n_phys, page * HA, DV_A)
    q_spec = pl.BlockSpec((None, rpad, W_A), lambda b, pt: (b, 0, 0))
    new_spec = pl.BlockSpec((None, l_new * HA, DV_A), lambda b, pt: (b, 0, 0))
    page_specs = [pl.BlockSpec((None, None, page * HA, DV_A),
                               functools.partial(lambda b, pt, j: (layer, pt[b, j], 0, 0), j=j))
                  for j in range(n_pages)]
    kern = functools.partial(_attn_sample_kernel, n_pages=n_pages, l_new=l_new, lam_init=lam_init)
    out = pl.pallas_call(
        kern,
        grid_spec=pltpu.PrefetchScalarGridSpec(
            num_scalar_prefetch=1,
            grid=(db,),
            in_specs=[q_spec, new_spec, new_spec,
                      pl.BlockSpec((4, LANES), lambda b, pt: (0, 0)),
                      pl.BlockSpec((1, LANES), lambda b, pt: (0, 0))] + page_specs + page_specs,
            out_specs=q_spec,
        ),
        out_shape=jax.ShapeDtypeStruct((db, rpad, W_A), F32),
        compiler_params=_cparams(("arbitrary",)),
        name="attn_sample",
    )(page_table, pad_rows(q), new_rows(k), new_rows(v), lamv, g,
      *([cache_k] * n_pages), *([cache_v] * n_pages))
    return out[:, :l_new].reshape(t, W_A).astype(BF16)


def _log_sigmoid(x):
    return jnp.minimum(x, 0.0) - jnp.log1p(jnp.exp(-jnp.abs(x)))


def _gla_gate(br, wg_ref, bgate_ref):
    x = jnp.dot(br.astype(BF16), wg_ref[...], preferred_element_type=F32) + bgate_ref[...]
    return _log_sigmoid(x) / GATE_NORM


def _gla_intra(q_att, k_in, v, chunk):
    keep = _chunk_causal(q_att.shape[0], chunk)
    lane = _idiv(_iota2(q_att.shape, 1), DK_B)
    kb = k_in.astype(BF16)
    atts, vs = [], []
    for h in range(HB):
        qh = jnp.where(lane == h, q_att, 0.0).astype(BF16)
        a = lax.dot_general(qh, kb, (((1,), (1,)), ((), ())), preferred_element_type=F32)
        atts.append(jnp.where(keep, a, 0.0).astype(BF16))
        vs.append(jnp.where(lane == h, v, 0.0).astype(BF16))
    return jnp.dot(jnp.concatenate(atts, axis=1), jnp.concatenate(vs, axis=0), preferred_element_type=F32)


def _gla_finish(o, gate_in, gng_ref, bd):
    ms = _seg_sum(o * o, bd) * (1.0 / DV_B)
    o = o * lax.rsqrt(ms + RMS_EPS) * gng_ref[...]
    return o * (gate_in * (1.0 / (1.0 + jnp.exp(-gate_in))))


def _chunk_mlp(c_in, lng_ref, lnb_ref, ws_ref, bst, chunk, bd):
    n = c_in.shape[0]
    cu, cv = c_in[:, :W_C], c_in[:, W_C:]
    mu = _seg_sum(cv, bd) * (1.0 / DC)
    xc = cv - mu
    var = _seg_sum(xc * xc, bd) * (1.0 / DC)
    vn = xc * lax.rsqrt(var + LN_EPS) * lng_ref[...] + lnb_ref[...]
    keep = _chunk_causal(n, chunk)
    lane = _idiv(_iota2(vn.shape, 1), DC)
    ws, vs = [], []
    for g in range(HC):
        ws.append(jnp.where(keep, ws_ref[g], 0.0).astype(BF16))
        vs.append(jnp.where(lane == g, vn, 0.0).astype(BF16))
    mixed = jnp.dot(jnp.concatenate(ws, axis=1), jnp.concatenate(vs, axis=0), preferred_element_type=F32) + bst
    return cu * mixed, vn


def _mixer_prompt_kernel(g_ref, br_ref, c_ref, wg_ref, bgate_ref, gng_ref, lng_ref, lnb_ref, ws_ref, bst_ref,
                         o_ref, st_ref, st_sc, *, ts, chunk):
    t = pl.program_id(1)

    @pl.when(t == 0)
    def _():
        st_sc[...] = jnp.zeros(st_sc.shape, F32)

    grp = CMLP_CHUNK
    bd = _head_blockdiag(W_B)
    bd_bf = jnp.where(bd, 1.0, 0.0).astype(BF16)
    csum_sel = jnp.where(_chunk_causal(grp, chunk), 1.0, 0.0).astype(BF16)
    rows = _iota2((grp, W_B), 0)
    half = chunk // 2
    for gi in range(ts // grp):
        rs = slice(gi * grp, (gi + 1) * grp)
        g = g_ref[rs, :]
        gq = g[:, 0:W_B] * (DK_B ** -0.5)
        gk, gv, gg = g[:, W_B:2 * W_B], g[:, 2 * W_B:3 * W_B], g[:, 3 * W_B:4 * W_B]
        la = _gla_gate(br_ref[rs, :], wg_ref, bgate_ref)
        bcum = _dot_sel(csum_sel, la)
        mids, lasts = [], []
        for ci in range(grp // chunk):
            mids.append(jnp.broadcast_to(bcum[ci * chunk + half - 1:ci * chunk + half, :], (chunk, W_B)))
            lasts.append(jnp.broadcast_to(bcum[(ci + 1) * chunk - 1:(ci + 1) * chunk, :], (chunk, W_B)))
        bmid = jnp.concatenate(mids, axis=0)
        blast = jnp.concatenate(lasts, axis=0)
        q_att = gq * jnp.exp(bcum - bmid)
        k_in = gk * jnp.exp(bmid - bcum)
        k_end = gk * jnp.exp(blast - bcum)
        q_dec = (gq * jnp.exp(bcum)).astype(BF16)
        o = _gla_intra(q_att, k_in, gv, chunk)
        v_t = gv.T.astype(BF16)
        o_inter = []
        for ci in range(grp // chunk):
            cs = slice(ci * chunk, (ci + 1) * chunk)
            st = st_sc[...]
            o_inter.append(lax.dot_general(q_dec[cs], st.astype(BF16), (((1,), (1,)), ((), ())),
                                           preferred_element_type=F32))
            kem = jnp.where(_idiv(rows, chunk) == ci, k_end, 0.0).astype(BF16)
            upd = jnp.dot(v_t, kem, preferred_element_type=F32)
            dl = jnp.exp(blast[ci * chunk:ci * chunk + 1, :])
            st_sc[...] = st * dl + jnp.where(bd, upd, 0.0)
        o = o + jnp.concatenate(o_inter, axis=0)
        o_b = _gla_finish(o, gg, gng_ref, bd_bf)
        o_c, _ = _chunk_mlp(c_ref[rs, :], lng_ref, lnb_ref, ws_ref, bst_ref[...], CMLP_CHUNK, bd_bf)
        o_ref[rs, :] = jnp.concatenate([o_b, o_c], axis=1).astype(o_ref.dtype)

    @pl.when(t == pl.num_programs(1) - 1)
    def _():
        st_ref[...] = st_sc[...].T


def _mixer_prompt(g_in, br, c_in, prm, nbatch, ts=512):
    t = g_in.shape[0]
    nt = t // nbatch // ts
    row = lambda n: pl.BlockSpec((ts, n), lambda b, i: (b * nt + i, 0))
    full = lambda a: pl.BlockSpec(a.shape, lambda b, i: (0,) * a.ndim)
    kern = functools.partial(_mixer_prompt_kernel, ts=ts, chunk=GLA_CHUNK_PROMPT)
    return pl.pallas_call(
        kern,
        grid=(nbatch, nt),
        in_specs=[row(4 * W_B), row(LANES), row(2 * W_C)] + [full(a) for a in prm],
        out_specs=[row(W_B + W_C), pl.BlockSpec((None, W_B, W_B), lambda b, i: (b, 0, 0))],
        out_shape=[jax.ShapeDtypeStruct((t, W_B + W_C), BF16),
                   jax.ShapeDtypeStruct((nbatch, W_B, W_B), F32)],
        scratch_shapes=[pltpu.VMEM((W_B, W_B), F32)],
        compiler_params=_cparams(("parallel", "arbitrary")),
        name="mixer_prompt",
    )(g_in, br, c_in, *prm)


def _mixer_sample_kernel(g_ref, br_ref, c_ref, s0_ref, wg_ref, bgate_ref, gng_ref, lng_ref, lnb_ref, ws_ref,
                         bst_ref, o_ref, vn_ref, st_ref, *, l_new):
    n = g_ref.shape[0]
    bd = _head_blockdiag(W_B)
    bd_bf = jnp.where(bd, 1.0, 0.0).astype(BF16)
    r, c = _iota2((n, n), 0), _iota2((n, n), 1)
    csum_sel = jnp.where(_chunk_causal(n, l_new), 1.0, 0.0).astype(BF16)
    last_sel = jnp.where(_idiv(r, l_new) == _idiv(c, l_new), 1.0, 0.0).astype(BF16)
    g = g_ref[...]
    gq = g[:, 0:W_B] * (DK_B ** -0.5)
    gk, gv, gg = g[:, W_B:2 * W_B], g[:, 2 * W_B:3 * W_B], g[:, 3 * W_B:4 * W_B]
    la = _gla_gate(br_ref[...], wg_ref, bgate_ref)
    bcum = _dot_sel(csum_sel, la)
    blast = _dot_sel(last_sel, la)
    q_in = gq * jnp.exp(bcum)
    k_in = gk * jnp.exp(-bcum)
    k_end = gk * jnp.exp(blast - bcum)
    o = _gla_intra(q_in, k_in, gv, l_new)
    zrows = jnp.zeros((LANES - n, W_B), F32)
    ke_t = jnp.concatenate([k_end, zrows], axis=0).T
    bl_t = jnp.concatenate([blast, zrows], axis=0).T
    v_pad = jnp.concatenate([gv, zrows], axis=0).astype(BF16)
    rows = _iota2((n, W_B), 0)
    cols = _iota2((W_B, LANES), 1)
    zblk = jnp.zeros((DK_B, DV_B), F32)
    for s in range(n // l_new):
        s0 = jnp.concatenate(
            [jnp.concatenate([s0_ref[s, h] if g == h else zblk for g in range(HB)], axis=1) for h in range(HB)],
            axis=0)
        qs = jnp.where(_idiv(rows, l_new) == s, q_in, 0.0).astype(BF16)
        o = o + jnp.dot(qs, s0.astype(BF16), preferred_element_type=F32)
        kes = jnp.where(_idiv(cols, l_new) == s, ke_t, 0.0).astype(BF16)
        upd = jnp.dot(kes, v_pad, preferred_element_type=F32)
        dl = jnp.exp(bl_t[:, s * l_new:s * l_new + 1])
        fin = s0 * dl + upd
        for h in range(HB):
            st_ref[s, h] = fin[h * DK_B:(h + 1) * DK_B, h * DV_B:(h + 1) * DV_B]
    o_b = _gla_finish(o, gg, gng_ref, bd_bf)
    o_c, vn = _chunk_mlp(c_ref[...], lng_ref, lnb_ref, ws_ref, bst_ref[...], l_new, bd_bf)
    o_ref[...] = jnp.concatenate([o_b, o_c], axis=1).astype(o_ref.dtype)
    vn_ref[...] = vn


def _mixer_sample(g_in, br, c_in, states, layer, prm, l_new, ts=64):
    t = g_in.shape[0]
    ns = ts // l_new
    nblk = t // ts
    row = lambda n: pl.BlockSpec((ts, n), lambda i: (i, 0))
    full = lambda a: pl.BlockSpec(a.shape, lambda i: (0,) * a.ndim)
    st = pl.BlockSpec((ns, HB, DK_B, DV_B), lambda i: (layer * nblk + i, 0, 0, 0))
    kern = functools.partial(_mixer_sample_kernel, l_new=l_new)
    return pl.pallas_call(
        kern,
        grid=(nblk,),
        in_specs=[row(4 * W_B), row(LANES), row(2 * W_C), st] + [full(a) for a in prm],
        out_specs=[row(W_B + W_C), row(W_C), st],
        out_shape=[jax.ShapeDtypeStruct((t, W_B + W_C), BF16),
                   jax.ShapeDtypeStruct((t, W_C), F32),
                   jax.ShapeDtypeStruct(states.shape, F32)],
        input_output_aliases={3: 2},
        compiler_params=_cparams(("parallel",)),
        name="mixer_sample",
    )(g_in, br, c_in, states, *prm)


def _outproj_kernel(oa_ref, obc_ref, x_ref, wo_ref, g_ref, b_ref, rw_ref, rb_ref, *rest, alpha):
    h_ref, hp_ref, lg_ref = rest[-3:]
    y = jnp.dot(oa_ref[...], wo_ref[0:W_A, :], preferred_element_type=F32)
    y = y + jnp.dot(obc_ref[...], wo_ref[W_A:, :], preferred_element_type=F32)
    h = _ln_rows(alpha * x_ref[...] + y, g_ref[...], b_ref[...])
    h_ref[...] = h
    hp_ref[...] = _pack_bf16_pairs(h)
    lg_ref[...] = jnp.dot(h, rw_ref[...], preferred_element_type=F32, precision=lax.Precision.HIGHEST) + rb_ref[...]


def _outproj(o_a, o_bc, x, wo, g, b, rw, rb, alpha, row0, rows_all, logits_all):
    t = x.shape[0]
    tm = min(512, t)
    assert row0 % tm == 0
    off = row0 // tm
    row = lambda n: pl.BlockSpec((tm, n), lambda i: (i, 0))
    row_at = lambda n: pl.BlockSpec((tm, n), lambda i: (i + off, 0))
    full = lambda a: pl.BlockSpec(a.shape, lambda i: (0,) * a.ndim)
    anywhere = pl.BlockSpec(memory_space=pl.ANY)
    return pl.pallas_call(
        functools.partial(_outproj_kernel, alpha=alpha),
        grid=(t // tm,),
        in_specs=[row(W_A), row(W_B + W_C), row(D_MODEL)] + [full(a) for a in (wo, g, b, rw, rb)]
                 + [anywhere, anywhere],
        out_specs=[row(D_MODEL), row_at(D_PACK), row_at(LANES)],
        out_shape=[jax.ShapeDtypeStruct((t, D_MODEL), F32), jax.ShapeDtypeStruct(rows_all.shape, F32),
                   jax.ShapeDtypeStruct(logits_all.shape, F32)],
        input_output_aliases={8: 1, 9: 2},
        compiler_params=_cparams(("parallel",)),
        name="outproj",
    )(o_a, o_bc, x, wo, g, b, rw, rb, rows_all, logits_all)


def _moe_kernel(be_ref, rv_ref, x_ref, w1_ref, b1_ref, w2_ref, b2_ref, y_ref, w1b_sc, w2b_sc):
    i = pl.program_id(0)

    @pl.when((i == 0) | (be_ref[i] != be_ref[jnp.maximum(i - 1, 0)]))
    def _():
        w1b_sc[...] = w1_ref[...].astype(BF16)
        w2b_sc[...] = w2_ref[...].astype(BF16)

    @pl.when(rv_ref[i] == 0)
    def _():
        y_ref[...] = jnp.zeros(y_ref.shape, F32)

    @pl.when(rv_ref[i] > 0)
    def _():
        x = _unpack_bf16_pairs(x_ref[...])
        xb = jnp.where(_iota2(x.shape, 0) < rv_ref[i], x, 0.0).astype(BF16)
        acc = None
        for c in range(D_FF // MOE_FF_CHUNK):
            gs = slice(c * MOE_FF_CHUNK, (c + 1) * MOE_FF_CHUNK)
            us = slice(D_FF + c * MOE_FF_CHUNK, D_FF + (c + 1) * MOE_FF_CHUNK)
            g = jnp.dot(xb, w1b_sc[:, gs], preferred_element_type=F32) + b1_ref[:, gs]
            u = jnp.dot(xb, w1b_sc[:, us], preferred_element_type=F32) + b1_ref[:, us]
            g = jnp.minimum(g, SWIGLU_LIMIT)
            u = jnp.clip(u, -SWIGLU_LIMIT, SWIGLU_LIMIT)
            act = (u + 1.0) * g * (1.0 / (1.0 + jnp.exp(-SWIGLU_ALPHA * g)))
            part = jnp.dot(act.astype(BF16), w2b_sc[gs, :], preferred_element_type=F32)
            acc = part if acc is None else acc + part
        y_ref[...] = _pack_bf16_pairs(acc + b2_ref[...])


def _moe_experts(x_pad, block_e, rows_valid, w1, b1, w2, b2, layer, bm):
    nb = x_pad.shape[0] // bm
    return pl.pallas_call(
        _moe_kernel,
        grid_spec=pltpu.PrefetchScalarGridSpec(
            num_scalar_prefetch=2,
            grid=(nb,),
            in_specs=[pl.BlockSpec((bm, D_PACK), lambda i, be, rv: (i, 0)),
                      pl.BlockSpec((None, None, D_MODEL, 2 * D_FF), lambda i, be, rv: (layer, be[i], 0, 0)),
                      pl.BlockSpec((None, None, 1, 2 * D_FF), lambda i, be, rv: (layer, be[i], 0, 0)),
                      pl.BlockSpec((None, None, D_FF, D_MODEL), lambda i, be, rv: (layer, be[i], 0, 0)),
                      pl.BlockSpec((None, None, 1, D_MODEL), lambda i, be, rv: (layer, be[i], 0, 0))],
            out_specs=pl.BlockSpec((bm, D_PACK), lambda i, be, rv: (i, 0)),
            scratch_shapes=[pltpu.VMEM((D_MODEL, 2 * D_FF), BF16), pltpu.VMEM((D_FF, D_MODEL), BF16)],
        ),
        out_shape=jax.ShapeDtypeStruct((nb * bm, D_PACK), F32),
        compiler_params=_cparams(("arbitrary",)),
        name="moe_experts",
    )(block_e, rows_valid, x_pad, w1, b1, w2, b2)


ROUTE_E, ROUTE_RANK, ROUTE_GATE = 0, TOP_K, 2 * TOP_K


def _router_kernel(lg_ref, route_ref, route_t_ref, cnt_ref, cnt_sc):
    @pl.when(pl.program_id(0) == 0)
    def _():
        cnt_sc[...] = jnp.zeros(cnt_sc.shape, F32)

    work = lg_ref[...]
    tm = work.shape[0]
    lane = _iota2(work.shape, 1)
    lane_f = lane.astype(F32)
    sels, vals, ids = [], [], []
    for _ in range(TOP_K):
        mx = jnp.max(work, axis=1, keepdims=True)
        idx = jnp.min(jnp.where(work == mx, lane_f, float(LANES)), axis=1, keepdims=True)
        sel = lane_f == idx
        sels.append(sel)
        vals.append(mx)
        ids.append(idx)
        work = jnp.where(sel, -jnp.inf, work)
    ex = [jnp.exp(v - vals[0]) for v in vals]
    den = ex[0]
    for x in ex[1:]:
        den = den + x
    picked = jnp.zeros(work.shape, F32)
    for sel in sels:
        picked = jnp.where(sel, 1.0, picked)
    r, c = _iota2((tm, tm), 0), _iota2((tm, tm), 1)
    before = jnp.dot(jnp.where(c < r, 1.0, 0.0).astype(BF16), picked.astype(BF16), preferred_element_type=F32)
    before = before + cnt_sc[...]
    out = jnp.zeros(work.shape, F32)
    for k in range(TOP_K):
        rank = jnp.sum(jnp.where(sels[k], before, 0.0), axis=1, keepdims=True)
        out = jnp.where(lane == ROUTE_E + k, ids[k], out)
        out = jnp.where(lane == ROUTE_RANK + k, rank, out)
        out = jnp.where(lane == ROUTE_GATE + k, ex[k] / den, out)
    route_ref[...] = out
    route_t_ref[...] = out.T[0:ROUTE_GATE, :]
    cnt_sc[...] = cnt_sc[...] + jnp.sum(picked, axis=0, keepdims=True)
    cnt_ref[...] = cnt_sc[...]


def _router(logits):
    t = logits.shape[0]
    tm = min(512, t)
    return pl.pallas_call(
        _router_kernel,
        grid=(t // tm,),
        in_specs=[pl.BlockSpec((tm, LANES), lambda i: (i, 0))],
        out_specs=[pl.BlockSpec((tm, LANES), lambda i: (i, 0)), pl.BlockSpec((ROUTE_GATE, tm), lambda i: (0, i)),
                   pl.BlockSpec((1, LANES), lambda i: (0, 0))],
        out_shape=[jax.ShapeDtypeStruct((t, LANES), F32), jax.ShapeDtypeStruct((ROUTE_GATE, t), F32),
                   jax.ShapeDtypeStruct((1, LANES), F32)],
        scratch_shapes=[pltpu.VMEM((1, LANES), F32)],
        compiler_params=_cparams(("arbitrary",)),
        name="router",
    )(logits)


def _route(logits, bm):
    t = logits.shape[0]
    m = t * TOP_K
    route, route_t, cnt = _router(logits)
    e_t = route_t[ROUTE_E:ROUTE_E + TOP_K].astype(jnp.int32)
    rank_t = route_t[ROUTE_RANK:ROUTE_RANK + TOP_K].astype(jnp.int32)
    counts = cnt[0, :N_EXPERTS].astype(jnp.int32)
    padded = ((counts + bm - 1) // bm) * bm
    pad_end = jnp.cumsum(padded)
    pad_start = pad_end - padded
    experts = jnp.arange(N_EXPERTS)[:, None, None]
    dest = jnp.sum(jnp.where(e_t[None] == experts, pad_start[:, None, None], 0), axis=0) + rank_t
    nb = -(-m // bm) + N_EXPERTS
    block_e = jnp.sum((pad_end[None, :] <= (jnp.arange(nb) * bm)[:, None]).astype(jnp.int32), axis=1)
    block_e = jnp.minimum(block_e, N_EXPERTS - 1)
    n_used = (pad_end[-1] // bm).astype(jnp.int32).reshape(1)
    block_e = jnp.where(jnp.arange(nb) < n_used[0], block_e, block_e[jnp.maximum(n_used[0] - 1, 0)])
    rows_valid = jnp.clip((pad_start + counts)[block_e] - jnp.arange(nb) * bm, 0, bm).astype(jnp.int32)
    return route, dest, block_e.astype(jnp.int32), rows_valid


def _ln2_kernel(h_ref, gate_ref, *rest, alpha):
    y_refs, (g_ref, b_ref, o_ref) = rest[:TOP_K], rest[TOP_K:]
    gate = gate_ref[...]
    x = alpha * h_ref[...]
    for k in range(TOP_K):
        x = x + gate[:, ROUTE_GATE + k:ROUTE_GATE + k + 1] * _unpack_bf16_pairs(y_refs[k][...])
    o_ref[...] = _ln_rows(x, g_ref[...], b_ref[...])


def _ln2(h, gate, yg, row0, g, b, alpha):
    t = h.shape[0]
    t_all = yg.shape[0] // TOP_K
    tm = min(512, t)
    assert row0 % tm == 0 and t_all % tm == 0
    off = row0 // tm
    nt_all = t_all // tm
    y_specs = [pl.BlockSpec((tm, D_PACK), functools.partial(lambda i, k: (k * nt_all + off + i, 0), k=k))
               for k in range(TOP_K)]
    return pl.pallas_call(
        functools.partial(_ln2_kernel, alpha=alpha),
        grid=(t // tm,),
        in_specs=[pl.BlockSpec((tm, D_MODEL), lambda i: (i, 0)),
                  pl.BlockSpec((tm, LANES), lambda i: (i + off, 0))] + y_specs +
                 [pl.BlockSpec((1, D_MODEL), lambda i: (0, 0)),
                  pl.BlockSpec((1, D_MODEL), lambda i: (0, 0))],
        out_specs=pl.BlockSpec((tm, D_MODEL), lambda i: (i, 0)),
        out_shape=jax.ShapeDtypeStruct((t, D_MODEL), F32),
        compiler_params=_cparams(("parallel",)),
        name="ln2",
    )(h, gate, *([yg] * TOP_K), g, b)


SC_CORES, SC_SUBCORES = 2, 16
SC_CHUNK = 64


def _sc_gather(table, idx):
    b, d = idx.shape[0], table.shape[1]
    workers = SC_CORES * SC_SUBCORES
    per_w = b // workers
    assert per_w * workers == b and per_w % SC_CHUNK == 0
    mesh = plsc.VectorSubcoreMesh(core_axis_name="c", subcore_axis_name="s")

    n_chunks = per_w // SC_CHUNK

    @functools.partial(
        pl.kernel, mesh=mesh, out_type=jax.ShapeDtypeStruct((b, d), table.dtype),
        scratch_types=[pltpu.VMEM((per_w,), jnp.int32), pltpu.VMEM((2, SC_CHUNK, d), table.dtype),
                       pltpu.SemaphoreType.DMA((2,))],
        name="sc_gather")
    def gather(table_hbm, idx_hbm, out_hbm, idx_v, rows_v, sems):
        wid = lax.axis_index("s") * SC_CORES + lax.axis_index("c")
        pltpu.sync_copy(idx_hbm.at[pl.ds(wid * per_w, per_w)], idx_v)

        def fetch(i, slot):
            return pltpu.make_async_copy(table_hbm.at[idx_v.at[pl.ds(i * SC_CHUNK, SC_CHUNK)]],
                                         rows_v.at[slot], sems.at[slot])

        fetch(0, 0).start()

        @pl.loop(0, n_chunks)
        def _(i):
            slot = lax.rem(i, 2)

            @pl.when(i + 1 < n_chunks)
            def _():
                fetch(i + 1, 1 - slot).start()

            fetch(i, slot).wait()
            pltpu.sync_copy(rows_v.at[slot], out_hbm.at[pl.ds(wid * per_w + i * SC_CHUNK, SC_CHUNK)])

    return gather(table, idx)


SC_SCATTER_CHUNK = 48


def _sc_scatter_rows(rows, dest, n_out):
    t, d = rows.shape
    workers = SC_CORES * SC_SUBCORES
    per_w = t // workers
    assert per_w * workers == t and per_w % SC_SCATTER_CHUNK == 0
    mesh = plsc.VectorSubcoreMesh(core_axis_name="c", subcore_axis_name="s")

    @functools.partial(
        pl.kernel, mesh=mesh, out_type=jax.ShapeDtypeStruct((n_out, d), rows.dtype),
        scratch_types=[pltpu.VMEM((TOP_K, SC_SCATTER_CHUNK), jnp.int32),
                       pltpu.VMEM((SC_SCATTER_CHUNK, d), rows.dtype)],
        name="sc_scatter")
    def scatter(rows_hbm, dest_hbm, out_hbm, idx_v, rows_v):
        wid = lax.axis_index("s") * SC_CORES + lax.axis_index("c")

        @pl.loop(0, per_w // SC_SCATTER_CHUNK)
        def _(i):
            base = wid * per_w + i * SC_SCATTER_CHUNK
            pltpu.sync_copy(rows_hbm.at[pl.ds(base, SC_SCATTER_CHUNK)], rows_v)
            for k in range(TOP_K):
                pltpu.sync_copy(dest_hbm.at[pl.ds(k * t + base, SC_SCATTER_CHUNK)], idx_v.at[k])
            for k in range(TOP_K):
                pltpu.sync_copy(rows_v, out_hbm.at[idx_v.at[k]])

    return scatter(rows, dest.reshape(-1))


def _rope_tables(pos):
    half = ROT_DIM // 2
    inv_freq = ROPE_THETA ** (-jnp.arange(0, ROT_DIM, 2, dtype=F32) / ROT_DIM)
    ang = pos.astype(F32)[:, None] * inv_freq[None, :]
    cos, sin = jnp.cos(ang), jnp.sin(ang)
    m = np.arange(LANES) % DQK_A
    idx = m % half
    cos_l = jnp.where(m < ROT_DIM, cos[:, idx], 1.0)
    sa = jnp.where(m < half, -sin[:, idx], 0.0)
    sb = jnp.where((m >= half) & (m < ROT_DIM), sin[:, idx], 0.0)
    return cos_l, sa, sb


def _prep_w_in(w):
    r0 = COL_C
    r1 = r0 + GATE_RANK
    pad = jnp.zeros((w.shape[0], LANES - GATE_RANK), w.dtype)
    return jnp.concatenate([w[:, :r0], w[:, r1:], w[:, r0:r1], pad], axis=1).astype(BF16)


def _tile_lanes(v, reps):
    return jnp.tile(v.reshape(1, -1), (1, reps)).astype(F32)


def _blockdiag_states(s):
    n = s.shape[0]
    eye = jnp.eye(HB, dtype=s.dtype)
    return jnp.einsum('nhde,hg->nhdge', s, eye).reshape(n, HB * DK_B, HB * DV_B)


def _diag_states(sbd):
    n = sbd.shape[0]
    s = sbd.reshape(n, HB, DK_B, HB, DV_B)
    return jnp.stack([s[:, h, :, h, :] for h in range(HB)], axis=1)


def kernel(x_prompt, x_sample, cache_k, cache_v, page_table, state_gla, w_in, lam_q1, lam_k1, lam_q2, lam_k2, attn_norm_g, gla_w_gate, gla_b_gate, gla_norm_g, cmlp_ln_g, cmlp_ln_b, cmlp_ws, cmlp_bs, w_o, ln1_g, ln1_b, router_w, router_b, exp_w1, exp_b1, exp_w2, exp_b2, ln2_g, ln2_b):
    depth = w_in.shape[0]
    bp, s_len, _ = x_prompt.shape
    db, l_new, _ = x_sample.shape
    n_phys, page = cache_k.shape[1], cache_k.shape[2]
    past_len = page_table.shape[1] * page
    alpha = (2 * depth) ** 0.25
    tp, ts = bp * s_len, db * l_new
    bm = MOE_BLOCK

    tabs_p = _rope_tables(jnp.arange(s_len))
    tabs_s = _rope_tables(past_len + (jnp.arange(ts) % l_new))
    page_table = page_table.astype(jnp.int32)

    hp = x_prompt.reshape(tp, D_MODEL)
    hs = x_sample.reshape(ts, D_MODEL)
    outs = {k: [] for k in ("gp", "cs")}
    gs_all = state_gla.astype(F32).reshape(depth * db, HB, DK_B, DV_B)
    kp_all, vp_all = (jnp.zeros((depth * tp, HA, DV_A), F32) for _ in range(2))
    ks_all, vs_all = (jnp.zeros((depth * ts, HA, DV_A), F32) for _ in range(2))
    for l in range(depth):
        lam_init = 0.8 - 0.6 * math.exp(-0.3 * l)
        w = _prep_w_in(w_in[l])
        lamv = jnp.pad(jnp.stack([lam_q1[l], lam_k1[l], lam_q2[l], lam_k2[l]]).astype(F32),
                       ((0, 0), (0, LANES - DQK_A)))
        g_attn = attn_norm_g[l].reshape(1, DV_A).astype(F32)
        wg = jnp.pad(gla_w_gate[l], ((0, LANES - GATE_RANK), (0, 0))).astype(BF16)
        wo = w_o[l].astype(BF16)
        rw = jnp.pad(router_w[l].astype(F32), ((0, 0), (0, LANES - N_EXPERTS)))
        rb = jnp.pad(router_b[l].astype(F32), (0, LANES - N_EXPERTS), constant_values=NEG_INF).reshape(1, LANES)
        ln1 = (ln1_g[l].reshape(1, D_MODEL), ln1_b[l].reshape(1, D_MODEL))

        def mixer_params(lc, n_rows):
            reps = n_rows // lc
            ws = jnp.tile(cmlp_ws[l][:, :lc, :lc], (1, reps, reps))
            bst = jnp.tile(jnp.repeat(cmlp_bs[l][:, :lc].T, DC, axis=1), (reps, 1))
            return (wg, gla_b_gate[l].reshape(1, W_B), _tile_lanes(gla_norm_g[l], HB),
                    _tile_lanes(cmlp_ln_g[l], HC), _tile_lanes(cmlp_ln_b[l], HC), ws, bst)

        q, kp_all, vp_all, kb, vb, g_in, c_in, br = _inproj(hp, w, tabs_p, l, kp_all, vp_all)
        o_a = _attn_prompt(q, kb, vb, lamv, g_attn, bp, lam_init)
        o_bc, st_p = _mixer_prompt(g_in, br, c_in, mixer_params(CMLP_CHUNK, CMLP_CHUNK), bp)
        rows_k, logits = jnp.zeros((tp + ts, D_PACK), F32), jnp.zeros((tp + ts, LANES), F32)
        hp1, rows_k, logits = _outproj(o_a, o_bc, hp, wo, *ln1, rw, rb, alpha, 0, rows_k, logits)
        outs["gp"].append(_diag_states(st_p))

        q, ks_all, vs_all, kb, vb, g_in, c_in, br = _inproj(hs, w, tabs_s, l, ks_all, vs_all)
        o_a = _attn_sample(q, kb, vb, cache_k, cache_v, l, page_table, lamv, g_attn, l_new, lam_init)
        rows_s = min(64, ts)
        lc = min(l_new, CMLP_CHUNK)
        o_bc, vn, gs_all = _mixer_sample(g_in, br, c_in, gs_all, l, mixer_params(lc, rows_s), l_new, rows_s)
        hs1, rows_k, logits = _outproj(o_a, o_bc, hs, wo, *ln1, rw, rb, alpha, tp, rows_k, logits)
        outs["cs"].append(vn.reshape(db, l_new, W_C))

        gate, dest, block_e, rows_valid = _route(logits, bm)
        x_pad = _sc_scatter_rows(rows_k, dest, block_e.shape[0] * bm)
        y_pad = _moe_experts(x_pad, block_e, rows_valid, exp_w1, exp_b1.reshape(depth, N_EXPERTS, 1, -1),
                             exp_w2, exp_b2.reshape(depth, N_EXPERTS, 1, -1), l, bm)
        yg = _sc_gather(y_pad, dest.reshape(-1))
        hp = _ln2(hp1, gate, yg, 0, ln2_g[l].reshape(1, -1), ln2_b[l].reshape(1, -1), alpha)
        hs = _ln2(hs1, gate, yg, tp, ln2_g[l].reshape(1, -1), ln2_b[l].reshape(1, -1), alpha)

    return (hp.reshape(bp, s_len, D_MODEL), hs.reshape(db, l_new, D_MODEL),
            kp_all.reshape(depth, bp, s_len, HA, DV_A), vp_all.reshape(depth, bp, s_len, HA, DV_A),
            jnp.stack(outs["gp"]),
            ks_all.reshape(depth, db, l_new, HA, DV_A), vs_all.reshape(depth, db, l_new, HA, DV_A),
            gs_all.reshape(depth, db, HB, DK_B, DV_B), jnp.stack(outs["cs"]))
```

```python
import functools
import math

import numpy as np
import jax
import jax.numpy as jnp
from jax import lax
from jax.experimental import pallas as pl
from jax.experimental.pallas import tpu as pltpu
from jax.experimental.pallas import tpu_sc as plsc

F32, BF16 = jnp.float32, jnp.bfloat16
LANES = 128
VMEM_LIMIT = 48 * 1024 * 1024

D_MODEL = 1024
HA, DQK_A, DV_A = 4, 64, 128
ROT_DIM = DQK_A // 4
ROPE_THETA = 500000.0
HB, DK_B, DV_B = 4, 64, 64
GATE_RANK = 16
GATE_NORM = 16.0
HC, DC = 4, 64
CMLP_CHUNK = 128
N_EXPERTS = 32
TOP_K = 4
D_FF = D_MODEL
SWIGLU_LIMIT = 7.0
SWIGLU_ALPHA = 1.702
LN_EPS = 1e-5
RMS_EPS = 1e-6
NEG_INF = -1e30
LOG2E = math.log2(math.e)

W_A = HA * 2 * DQK_A
W_B = HB * DK_B
W_C = HC * DC
COL_G = 3 * W_A
COL_C = COL_G + 4 * W_B
COL_R = COL_C + 2 * W_C
COL_END = COL_R + LANES
GLA_CHUNK_PROMPT = 32
ATTN_BQ, ATTN_BK = 512, 512
ATTN_ROWS = 2 * ATTN_BQ
D_PACK = D_MODEL // 2
MOE_BLOCK = 512
MOE_FF_CHUNK = 512


def _cparams(sem):
    return pltpu.CompilerParams(dimension_semantics=sem, vmem_limit_bytes=VMEM_LIMIT)


def _split3(x):
    hi = x.astype(BF16)
    r = x - hi.astype(F32)
    mid = r.astype(BF16)
    lo = (r - mid.astype(F32)).astype(BF16)
    return hi, mid, lo


def _dot_sel(sel_bf16, x):
    acc = None
    for p in _split3(x):
        d = jnp.dot(sel_bf16, p, preferred_element_type=F32)
        acc = d if acc is None else acc + d
    return acc


def _seg_sum(x, bd_bf16):
    acc = None
    for p in _split3(x):
        d = jnp.dot(p, bd_bf16, preferred_element_type=F32)
        acc = d if acc is None else acc + d
    return acc


def _pack_bf16_pairs(x):
    u = lax.bitcast_convert_type(x, jnp.uint32)
    r = u + (jnp.uint32(0x7FFF) + ((u >> 16) & jnp.uint32(1)))
    w = x.shape[1] // 2
    word = (r[:, :w] & jnp.uint32(0xFFFF0000)) | (r[:, w:] >> 16)
    return lax.bitcast_convert_type(word, F32)


def _unpack_bf16_pairs(words):
    u = lax.bitcast_convert_type(words, jnp.uint32)
    hi = lax.bitcast_convert_type(u & jnp.uint32(0xFFFF0000), F32)
    lo = lax.bitcast_convert_type(u << 16, F32)
    return jnp.concatenate([hi, lo], axis=1)


def _iota2(shape, dim):
    return lax.broadcasted_iota(jnp.int32, shape, dim)


def _idiv(x, n):
    shift = n.bit_length() - 1
    assert n == 1 << shift
    return x >> shift


def _head_blockdiag(n):
    r, c = _iota2((n, n), 0), _iota2((n, n), 1)
    return _idiv(r, DK_B) == _idiv(c, DK_B)


def _chunk_causal(n, chunk):
    r, c = _iota2((n, n), 0), _iota2((n, n), 1)
    return (_idiv(r, chunk) == _idiv(c, chunk)) & (c <= r)


def _ln_rows(x, g, b):
    mu = jnp.mean(x, axis=-1, keepdims=True)
    xc = x - mu
    var = jnp.mean(xc * xc, axis=-1, keepdims=True)
    return xc * lax.rsqrt(var + LN_EPS) * g + b


def _inproj_kernel(x_ref, w_ref, cos_ref, sa_ref, sb_ref, k_all_ref, v_all_ref,
                   q_ref, k_ref, v_ref, kb_ref, vb_ref, g_ref, c_ref, br_ref):
    del k_all_ref, v_all_ref
    xb = x_ref[...].astype(BF16)

    def proj(a, b):
        return jnp.dot(xb, w_ref[:, a:b], preferred_element_type=F32)

    cos, sa, sb = cos_ref[...], sa_ref[...], sb_ref[...]

    def rope(z):
        outs = []
        for i in range(z.shape[1] // LANES):
            zi = z[:, i * LANES:(i + 1) * LANES]
            outs.append(zi * cos + pltpu.roll(zi, LANES - ROT_DIM // 2, 1) * sa
                        + pltpu.roll(zi, ROT_DIM // 2, 1) * sb)
        return jnp.concatenate(outs, axis=1)

    q_ref[...] = (rope(proj(0, W_A)) * (DQK_A ** -0.5 * LOG2E)).astype(BF16)
    k = rope(proj(W_A, 2 * W_A))
    kb_ref[...] = k.astype(BF16)
    v = proj(2 * W_A, 3 * W_A)
    vb_ref[...] = v.astype(BF16)
    for h in range(HA):
        k_ref[:, h, :] = k[:, h * LANES:(h + 1) * LANES]
        v_ref[:, h, :] = v[:, h * LANES:(h + 1) * LANES]
    g_ref[...] = proj(COL_G, COL_C)
    c_ref[...] = proj(COL_C, COL_R)
    br_ref[...] = proj(COL_R, COL_END)


def _inproj(x, w, tabs, layer, k_all, v_all):
    t = x.shape[0]
    tm = min(512, t)
    nt = t // tm
    cos, sa, sb = tabs
    ntab = cos.shape[0] // tm
    row = lambda n: pl.BlockSpec((tm, n), lambda i: (i, 0))
    tab = pl.BlockSpec((tm, LANES), lambda i: (i % ntab, 0))
    heads = pl.BlockSpec((tm, HA, DV_A), lambda i: (layer * nt + i, 0, 0))
    anywhere = pl.BlockSpec(memory_space=pl.ANY)
    shapes = [((W_A,), BF16), None, None, ((W_A,), BF16), ((W_A,), BF16),
              ((4 * W_B,), F32), ((2 * W_C,), F32), ((LANES,), F32)]
    return pl.pallas_call(
        _inproj_kernel,
        grid=(nt,),
        in_specs=[row(D_MODEL), pl.BlockSpec((D_MODEL, COL_END), lambda i: (0, 0)), tab, tab, tab,
                  anywhere, anywhere],
        out_specs=[heads if s is None else row(s[0][0]) for s in shapes],
        out_shape=[jax.ShapeDtypeStruct(k_all.shape, F32) if s is None else jax.ShapeDtypeStruct((t,) + s[0], s[1])
                   for s in shapes],
        input_output_aliases={5: 1, 6: 2},
        compiler_params=_cparams(("parallel",)),
        name="inproj",
    )(x, w, cos, sa, sb, k_all, v_all)


def _diff_lambda(lamv, lam_init):
    a = jnp.sum(lamv[0:1] * lamv[1:2], axis=1, keepdims=True)
    b = jnp.sum(lamv[2:3] * lamv[3:4], axis=1, keepdims=True)
    return jnp.exp(a) - jnp.exp(b) + lam_init


def _diff_finish(o1, o2, lam, g, lam_init):
    o = o1 - lam * o2
    ms = jnp.mean(o * o, axis=-1, keepdims=True)
    return o * lax.rsqrt(ms + RMS_EPS) * g * (1.0 - lam_init)


def _split_maps(q):
    lane = _iota2(q.shape, 1)
    zero = jnp.zeros_like(q)
    return jnp.concatenate([jnp.where(lane < DQK_A, q, zero), jnp.where(lane >= DQK_A, q, zero)], axis=0)


def _attn_prompt_kernel(q_ref, k_ref, v_ref, lamv_ref, g_ref, o_ref, m_sc, l_sc, acc_sc, *, bq, bk, lam_init):
    qi = pl.program_id(2)
    qq = _split_maps(q_ref[...])
    m_sc[...] = jnp.full(m_sc.shape, NEG_INF, F32)
    l_sc[...] = jnp.zeros(l_sc.shape, F32)
    acc_sc[...] = jnp.zeros(acc_sc.shape, F32)

    def step(j, masked):
        start = pl.multiple_of(j * bk, bk)
        k = k_ref[pl.ds(start, bk), :]
        v = v_ref[pl.ds(start, bk), :]
        tiles = range(bk // LANES)
        for rc in range(2 * bq // ATTN_ROWS):
            rs = slice(rc * ATTN_ROWS, (rc + 1) * ATTN_ROWS)
            s = lax.dot_general(qq[rs], k, (((1,), (1,)), ((), ())), preferred_element_type=F32)
            if masked:
                r = (_iota2(s.shape, 0) + rc * ATTN_ROWS) & (bq - 1)
                c = _iota2(s.shape, 1)
                s = jnp.where(c <= r, s, NEG_INF)
            m_prev = m_sc[rs, :]
            s_max = functools.reduce(jnp.maximum, [s[:, i * LANES:(i + 1) * LANES] for i in tiles])
            m_new = jnp.maximum(m_prev, jnp.max(s_max, axis=1, keepdims=True))
            alpha = jnp.exp2(m_prev - m_new)
            p = jnp.exp2(s - jnp.tile(m_new, (1, bk // LANES)))
            p_sum = functools.reduce(jnp.add, [p[:, i * LANES:(i + 1) * LANES] for i in tiles])
            l_sc[rs, :] = alpha * l_sc[rs, :] + jnp.sum(p_sum, axis=1, keepdims=True)
            acc_sc[rs, :] = alpha * acc_sc[rs, :] + jnp.dot(p.astype(BF16), v, preferred_element_type=F32)
            m_sc[rs, :] = m_new

    def body(j, carry):
        step(j, False)
        return carry

    lax.fori_loop(0, qi, body, 0)
    step(qi, True)

    o = acc_sc[...] / l_sc[...]
    lam = _diff_lambda(lamv_ref[...], lam_init)
    o_ref[...] = _diff_finish(o[:bq], o[bq:], lam, g_ref[...], lam_init).astype(o_ref.dtype)


def _attn_prompt(q, kb, vb, lamv, g, nbatch, lam_init):
    t = q.shape[0]
    s = t // nbatch
    bq, bk = min(ATTN_BQ, s), min(ATTN_BK, s)
    assert bq == bk
    blk = bq
    nq = s // blk
    kern = functools.partial(_attn_prompt_kernel, bq=bq, bk=bk, lam_init=lam_init)
    return pl.pallas_call(
        kern,
        grid=(nbatch, HA, nq),
        in_specs=[pl.BlockSpec((blk, LANES), lambda b, h, i: (b * nq + i, h)),
                  pl.BlockSpec((s, LANES), lambda b, h, i: (b, h)),
                  pl.BlockSpec((s, LANES), lambda b, h, i: (b, h)),
                  pl.BlockSpec((4, LANES), lambda b, h, i: (0, 0)),
                  pl.BlockSpec((1, LANES), lambda b, h, i: (0, 0))],
        out_specs=pl.BlockSpec((blk, LANES), lambda b, h, i: (b * nq + i, h)),
        out_shape=jax.ShapeDtypeStruct((t, W_A), BF16),
        scratch_shapes=[pltpu.VMEM((2 * blk, LANES), F32)] * 3,
        compiler_params=_cparams(("parallel", "parallel", "arbitrary")),
        name="attn_prompt",
    )(q, kb, vb, lamv, g)


def _attn_sample_kernel(pt_ref, q_ref, kn_ref, vn_ref, lamv_ref, g_ref, *rest, n_pages, l_new, lam_init):
    del pt_ref
    kp = rest[:n_pages]
    vp = rest[n_pages:2 * n_pages]
    o_ref = rest[2 * n_pages]
    rows_pg = kp[0].shape[0]
    rpad = q_ref.shape[0]
    nq = 2 * rpad
    lam = _diff_lambda(lamv_ref[...], lam_init)
    q = q_ref[...]
    qq = jnp.concatenate([_split_maps(q[:, h * LANES:(h + 1) * LANES]) for h in range(HA)],
                         axis=0).astype(BF16)
    r = _iota2((HA * nq, rows_pg), 0)
    c = _iota2((HA * nq, rows_pg), 1)
    head_ok = (c & (HA - 1)) == _idiv(r, nq)
    new_ok = head_ok & (_idiv(c, HA) <= (r & (rpad - 1))) & (c < l_new * HA)
    nt = (((1,), (1,)), ((), ()))
    zpad = jnp.zeros((rows_pg - kn_ref.shape[0], LANES), F32)
    k_new = jnp.concatenate([kn_ref[...], zpad], axis=0).astype(BF16)
    v_new = jnp.concatenate([vn_ref[...], zpad], axis=0).astype(BF16)
    s_new = jnp.where(new_ok, lax.dot_general(qq, k_new, nt, preferred_element_type=F32), NEG_INF)
    s_past = [jnp.where(head_ok, lax.dot_general(qq, kp[j][...].astype(BF16), nt, preferred_element_type=F32),
                        NEG_INF) for j in range(n_pages)]
    m = jnp.max(s_new, axis=1, keepdims=True)
    for sj in s_past:
        m = jnp.maximum(m, jnp.max(sj, axis=1, keepdims=True))
    p_new = jnp.exp2(s_new - m)
    l = jnp.sum(p_new, axis=1, keepdims=True)
    acc = jnp.dot(p_new.astype(BF16), v_new, preferred_element_type=F32)
    for j in range(n_pages):
        pj = jnp.exp2(s_past[j] - m)
        l = l + jnp.sum(pj, axis=1, keepdims=True)
        acc = acc + jnp.dot(pj.astype(BF16), vp[j][...].astype(BF16), preferred_element_type=F32)
    o = acc / l
    outs = [_diff_finish(o[h * nq:h * nq + rpad], o[h * nq + rpad:(h + 1) * nq], lam, g_ref[...], lam_init)
            for h in range(HA)]
    o_ref[...] = jnp.concatenate(outs, axis=1)


def _attn_sample(q, k, v, cache_k, cache_v, layer, page_table, lamv, g, l_new, lam_init):
    t = q.shape[0]
    db = t // l_new
    n_pages = page_table.shape[1]
    page = cache_k.shape[2]
    rpad = 8

    def pad_rows(a):
        a = a.reshape(db, l_new, W_A).astype(F32)
        return jnp.concatenate([a, jnp.zeros((db, rpad - l_new, W_A), F32)], axis=1)

    def new_rows(a):
        return a.reshape(db, l_new * HA, DV_A).astype(F32)

    depth, n_phys = cache_k.shape[:2]
    cache_k = cache_k.reshape(depth, n_phys, page * HA, DV_A)
    cache_v = cache_v.reshape(depth, n_phys, page * HA, DV_A)
    q_spec = pl.BlockSpec((None, rpad, W_A), lambda b, pt: (b, 0, 0))
    new_spec = pl.BlockSpec((None, l_new * HA, DV_A), lambda b, pt: (b, 0, 0))
    page_specs = [pl.BlockSpec((None, None, page * HA, DV_A),
                               functools.partial(lambda b, pt, j: (layer, pt[b, j], 0, 0), j=j))
                  for j in range(n_pages)]
    kern = functools.partial(_attn_sample_kernel, n_pages=n_pages, l_new=l_new, lam_init=lam_init)
    out = pl.pallas_call(
        kern,
        grid_spec=pltpu.PrefetchScalarGridSpec(
            num_scalar_prefetch=1,
            grid=(db,),
            in_specs=[q_spec, new_spec, new_spec,
                      pl.BlockSpec((4, LANES), lambda b, pt: (0, 0)),
                      pl.BlockSpec((1, LANES), lambda b, pt: (0, 0))] + page_specs + page_specs,
            out_specs=q_spec,
        ),
        out_shape=jax.ShapeDtypeStruct((db, rpad, W_A), F32),
        compiler_params=_cparams(("arbitrary",)),
        name="attn_sample",
    )(page_table, pad_rows(q), new_rows(k), new_rows(v), lamv, g,
      *([cache_k] * n_pages), *([cache_v] * n_pages))
    return out[:, :l_new].reshape(t, W_A).astype(BF16)


def _log_sigmoid(x):
    return jnp.minimum(x, 0.0) - jnp.log(1.0 + jnp.exp(-jnp.abs(x)))


def _gla_gate(br, wg_ref, bgate_ref):
    x = jnp.dot(br.astype(BF16), wg_ref[...], preferred_element_type=F32) + bgate_ref[...]
    return _log_sigmoid(x) / GATE_NORM


def _gla_intra(q_att, k_in, v, chunk):
    keep = _chunk_causal(q_att.shape[0], chunk)
    lane = _idiv(_iota2(q_att.shape, 1), DK_B)
    kb = k_in.astype(BF16)
    atts, vs = [], []
    for h in range(HB):
        qh = jnp.where(lane == h, q_att, 0.0).astype(BF16)
        a = lax.dot_general(qh, kb, (((1,), (1,)), ((), ())), preferred_element_type=F32)
        atts.append(jnp.where(keep, a, 0.0).astype(BF16))
        vs.append(jnp.where(lane == h, v, 0.0).astype(BF16))
    return jnp.dot(jnp.concatenate(atts, axis=1), jnp.concatenate(vs, axis=0), preferred_element_type=F32)


def _gla_finish(o, gate_in, gng_ref, bd):
    ms = _seg_sum(o * o, bd) * (1.0 / DV_B)
    o = o * lax.rsqrt(ms + RMS_EPS) * gng_ref[...]
    return o * (gate_in * (1.0 / (1.0 + jnp.exp(-gate_in))))


def _chunk_mlp(c_in, lng_ref, lnb_ref, ws_ref, bst, chunk, bd):
    n = c_in.shape[0]
    cu, cv = c_in[:, :W_C], c_in[:, W_C:]
    mu = _seg_sum(cv, bd) * (1.0 / DC)
    xc = cv - mu
    var = _seg_sum(xc * xc, bd) * (1.0 / DC)
    vn = xc * lax.rsqrt(var + LN_EPS) * lng_ref[...] + lnb_ref[...]
    keep = _chunk_causal(n, chunk)
    lane = _idiv(_iota2(vn.shape, 1), DC)
    ws, vs = [], []
    for g in range(HC):
        ws.append(jnp.where(keep, ws_ref[g], 0.0).astype(BF16))
        vs.append(jnp.where(lane == g, vn, 0.0).astype(BF16))
    mixed = jnp.dot(jnp.concatenate(ws, axis=1), jnp.concatenate(vs, axis=0), preferred_element_type=F32) + bst
    return cu * mixed, vn


def _mixer_prompt_kernel(g_ref, br_ref, c_ref, wg_ref, bgate_ref, gng_ref, lng_ref, lnb_ref, ws_ref, bst_ref,
                         o_ref, st_ref, st_sc, *, ts, chunk):
    t = pl.program_id(1)

    @pl.when(t == 0)
    def _():
        st_sc[...] = jnp.zeros(st_sc.shape, F32)

    grp = CMLP_CHUNK
    bd = _head_blockdiag(W_B)
    bd_bf = jnp.where(bd, 1.0, 0.0).astype(BF16)
    csum_sel = jnp.where(_chunk_causal(grp, chunk), 1.0, 0.0).astype(BF16)
    rows = _iota2((grp, W_B), 0)
    half = chunk // 2
    for gi in range(ts // grp):
        rs = slice(gi * grp, (gi + 1) * grp)
        g = g_ref[rs, :]
        gq = g[:, 0:W_B] * (DK_B ** -0.5)
        gk, gv, gg = g[:, W_B:2 * W_B], g[:, 2 * W_B:3 * W_B], g[:, 3 * W_B:4 * W_B]
        la = _gla_gate(br_ref[rs, :], wg_ref, bgate_ref)
        bcum = _dot_sel(csum_sel, la)
        mids, lasts = [], []
        for ci in range(grp // chunk):
            mids.append(jnp.broadcast_to(bcum[ci * chunk + half - 1:ci * chunk + half, :], (chunk, W_B)))
            lasts.append(jnp.broadcast_to(bcum[(ci + 1) * chunk - 1:(ci + 1) * chunk, :], (chunk, W_B)))
        bmid = jnp.concatenate(mids, axis=0)
        blast = jnp.concatenate(lasts, axis=0)
        q_att = gq * jnp.exp(bcum - bmid)
        k_in = gk * jnp.exp(bmid - bcum)
        k_end = gk * jnp.exp(blast - bcum)
        q_dec = (gq * jnp.exp(bcum)).astype(BF16)
        o = _gla_intra(q_att, k_in, gv, chunk)
        v_t = gv.T.astype(BF16)
        o_inter = []
        for ci in range(grp // chunk):
            cs = slice(ci * chunk, (ci + 1) * chunk)
            st = st_sc[...]
            o_inter.append(lax.dot_general(q_dec[cs], st.astype(BF16), (((1,), (1,)), ((), ())),
                                           preferred_element_type=F32))
            kem = jnp.where(_idiv(rows, chunk) == ci, k_end, 0.0).astype(BF16)
            upd = jnp.dot(v_t, kem, preferred_element_type=F32)
            dl = jnp.exp(blast[ci * chunk:ci * chunk + 1, :])
            st_sc[...] = st * dl + jnp.where(bd, upd, 0.0)
        o = o + jnp.concatenate(o_inter, axis=0)
        o_b = _gla_finish(o, gg, gng_ref, bd_bf)
        o_c, _ = _chunk_mlp(c_ref[rs, :], lng_ref, lnb_ref, ws_ref, bst_ref[...], CMLP_CHUNK, bd_bf)
        o_ref[rs, :] = jnp.concatenate([o_b, o_c], axis=1).astype(o_ref.dtype)

    @pl.when(t == pl.num_programs(1) - 1)
    def _():
        st_ref[...] = st_sc[...].T


def _mixer_prompt(g_in, br, c_in, prm, nbatch, ts=1024):
    t = g_in.shape[0]
    nt = t // nbatch // ts
    row = lambda n: pl.BlockSpec((ts, n), lambda b, i: (b * nt + i, 0))
    full = lambda a: pl.BlockSpec(a.shape, lambda b, i: (0,) * a.ndim)
    kern = functools.partial(_mixer_prompt_kernel, ts=ts, chunk=GLA_CHUNK_PROMPT)
    return pl.pallas_call(
        kern,
        grid=(nbatch, nt),
        in_specs=[row(4 * W_B), row(LANES), row(2 * W_C)] + [full(a) for a in prm],
        out_specs=[row(W_B + W_C), pl.BlockSpec((None, W_B, W_B), lambda b, i: (b, 0, 0))],
        out_shape=[jax.ShapeDtypeStruct((t, W_B + W_C), BF16),
                   jax.ShapeDtypeStruct((nbatch, W_B, W_B), F32)],
        scratch_shapes=[pltpu.VMEM((W_B, W_B), F32)],
        compiler_params=_cparams(("parallel", "arbitrary")),
        name="mixer_prompt",
    )(g_in, br, c_in, *prm)


def _mixer_sample_kernel(g_ref, br_ref, c_ref, s0_ref, wg_ref, bgate_ref, gng_ref, lng_ref, lnb_ref, ws_ref,
                         bst_ref, o_ref, vn_ref, st_ref, *, l_new):
    n = g_ref.shape[0]
    bd = _head_blockdiag(W_B)
    bd_bf = jnp.where(bd, 1.0, 0.0).astype(BF16)
    r, c = _iota2((n, n), 0), _iota2((n, n), 1)
    csum_sel = jnp.where(_chunk_causal(n, l_new), 1.0, 0.0).astype(BF16)
    last_sel = jnp.where(_idiv(r, l_new) == _idiv(c, l_new), 1.0, 0.0).astype(BF16)
    g = g_ref[...]
    gq = g[:, 0:W_B] * (DK_B ** -0.5)
    gk, gv, gg = g[:, W_B:2 * W_B], g[:, 2 * W_B:3 * W_B], g[:, 3 * W_B:4 * W_B]
    la = _gla_gate(br_ref[...], wg_ref, bgate_ref)
    bcum = _dot_sel(csum_sel, la)
    blast = _dot_sel(last_sel, la)
    q_in = gq * jnp.exp(bcum)
    k_in = gk * jnp.exp(-bcum)
    k_end = gk * jnp.exp(blast - bcum)
    o = _gla_intra(q_in, k_in, gv, l_new)
    zrows = jnp.zeros((LANES - n, W_B), F32)
    ke_t = jnp.concatenate([k_end, zrows], axis=0).T
    bl_t = jnp.concatenate([blast, zrows], axis=0).T
    v_pad = jnp.concatenate([gv, zrows], axis=0).astype(BF16)
    rows = _iota2((n, W_B), 0)
    cols = _iota2((W_B, LANES), 1)
    zblk = jnp.zeros((DK_B, DV_B), F32)
    for s in range(n // l_new):
        s0 = jnp.concatenate(
            [jnp.concatenate([s0_ref[s, h] if g == h else zblk for g in range(HB)], axis=1) for h in range(HB)],
            axis=0)
        qs = jnp.where(_idiv(rows, l_new) == s, q_in, 0.0).astype(BF16)
        o = o + jnp.dot(qs, s0.astype(BF16), preferred_element_type=F32)
        kes = jnp.where(_idiv(cols, l_new) == s, ke_t, 0.0).astype(BF16)
        upd = jnp.dot(kes, v_pad, preferred_element_type=F32)
        dl = jnp.exp(bl_t[:, s * l_new:s * l_new + 1])
        fin = s0 * dl + upd
        for h in range(HB):
            st_ref[s, h] = fin[h * DK_B:(h + 1) * DK_B, h * DV_B:(h + 1) * DV_B]
    o_b = _gla_finish(o, gg, gng_ref, bd_bf)
    o_c, vn = _chunk_mlp(c_ref[...], lng_ref, lnb_ref, ws_ref, bst_ref[...], l_new, bd_bf)
    o_ref[...] = jnp.concatenate([o_b, o_c], axis=1).astype(o_ref.dtype)
    vn_ref[...] = vn


def _mixer_sample(g_in, br, c_in, states, layer, prm, l_new, ts=64):
    t = g_in.shape[0]
    ns = ts // l_new
    nblk = t // ts
    row = lambda n: pl.BlockSpec((ts, n), lambda i: (i, 0))
    full = lambda a: pl.BlockSpec(a.shape, lambda i: (0,) * a.ndim)
    st = pl.BlockSpec((ns, HB, DK_B, DV_B), lambda i: (layer * nblk + i, 0, 0, 0))
    kern = functools.partial(_mixer_sample_kernel, l_new=l_new)
    return pl.pallas_call(
        kern,
        grid=(nblk,),
        in_specs=[row(4 * W_B), row(LANES), row(2 * W_C), st] + [full(a) for a in prm],
        out_specs=[row(W_B + W_C), row(W_C), st],
        out_shape=[jax.ShapeDtypeStruct((t, W_B + W_C), BF16),
                   jax.ShapeDtypeStruct((t, W_C), F32),
                   jax.ShapeDtypeStruct(states.shape, F32)],
        input_output_aliases={3: 2},
        compiler_params=_cparams(("parallel",)),
        name="mixer_sample",
    )(g_in, br, c_in, states, *prm)


def _outproj_kernel(oa_ref, obc_ref, x_ref, wo_ref, g_ref, b_ref, rw_ref, rb_ref, *rest, alpha):
    h_ref, hp_ref, lg_ref = rest[-3:]
    y = jnp.dot(oa_ref[...], wo_ref[0:W_A, :], preferred_element_type=F32)
    y = y + jnp.dot(obc_ref[...], wo_ref[W_A:, :], preferred_element_type=F32)
    h = _ln_rows(alpha * x_ref[...] + y, g_ref[...], b_ref[...])
    h_ref[...] = h
    hp_ref[...] = _pack_bf16_pairs(h)
    h_hi = h.astype(BF16)
    h_lo = (h - h_hi.astype(F32)).astype(BF16)
    lg = jnp.dot(h_hi, rw_ref[0], preferred_element_type=F32)
    lg = lg + jnp.dot(h_lo, rw_ref[0], preferred_element_type=F32)
    lg = lg + jnp.dot(h_hi, rw_ref[1], preferred_element_type=F32)
    lg_ref[...] = lg + rb_ref[...]


def _outproj(o_a, o_bc, x, wo, g, b, rw, rb, alpha, row0, rows_all, logits_all):
    t = x.shape[0]
    tm = min(512, t)
    assert row0 % tm == 0
    off = row0 // tm
    row = lambda n: pl.BlockSpec((tm, n), lambda i: (i, 0))
    row_at = lambda n: pl.BlockSpec((tm, n), lambda i: (i + off, 0))
    full = lambda a: pl.BlockSpec(a.shape, lambda i: (0,) * a.ndim)
    anywhere = pl.BlockSpec(memory_space=pl.ANY)
    return pl.pallas_call(
        functools.partial(_outproj_kernel, alpha=alpha),
        grid=(t // tm,),
        in_specs=[row(W_A), row(W_B + W_C), row(D_MODEL)] + [full(a) for a in (wo, g, b, rw, rb)]
                 + [anywhere, anywhere],
        out_specs=[row(D_MODEL), row_at(D_PACK), row_at(LANES)],
        out_shape=[jax.ShapeDtypeStruct((t, D_MODEL), F32), jax.ShapeDtypeStruct(rows_all.shape, F32),
                   jax.ShapeDtypeStruct(logits_all.shape, F32)],
        input_output_aliases={8: 1, 9: 2},
        compiler_params=_cparams(("parallel",)),
        name="outproj",
    )(o_a, o_bc, x, wo, g, b, rw, rb, rows_all, logits_all)


def _moe_kernel(be_ref, rv_ref, x_ref, w1_ref, b1_ref, w2_ref, b2_ref, y_ref, w1b_sc, w2b_sc):
    i = pl.program_id(0)

    @pl.when((i == 0) | (be_ref[i] != be_ref[jnp.maximum(i - 1, 0)]))
    def _():
        w1b_sc[...] = w1_ref[...].astype(BF16)
        w2b_sc[...] = w2_ref[...].astype(BF16)

    @pl.when(rv_ref[i] == 0)
    def _():
        y_ref[...] = jnp.zeros(y_ref.shape, F32)

    @pl.when(rv_ref[i] > 0)
    def _():
        x = _unpack_bf16_pairs(x_ref[...])
        xb = jnp.where(_iota2(x.shape, 0) < rv_ref[i], x, 0.0).astype(BF16)
        acc = None
        for c in range(D_FF // MOE_FF_CHUNK):
            gs = slice(c * MOE_FF_CHUNK, (c + 1) * MOE_FF_CHUNK)
            us = slice(D_FF + c * MOE_FF_CHUNK, D_FF + (c + 1) * MOE_FF_CHUNK)
            g = jnp.dot(xb, w1b_sc[:, gs], preferred_element_type=F32) + b1_ref[:, gs]
            u = jnp.dot(xb, w1b_sc[:, us], preferred_element_type=F32) + b1_ref[:, us]
            g = jnp.minimum(g, SWIGLU_LIMIT)
            u = jnp.clip(u, -SWIGLU_LIMIT, SWIGLU_LIMIT)
            act = (u + 1.0) * g * (1.0 / (1.0 + jnp.exp(-SWIGLU_ALPHA * g)))
            part = jnp.dot(act.astype(BF16), w2b_sc[gs, :], preferred_element_type=F32)
            acc = part if acc is None else acc + part
        y_ref[...] = _pack_bf16_pairs(acc + b2_ref[...])


def _moe_experts(x_pad, block_e, rows_valid, w1, b1, w2, b2, layer, bm):
    nb = x_pad.shape[0] // bm
    return pl.pallas_call(
        _moe_kernel,
        grid_spec=pltpu.PrefetchScalarGridSpec(
            num_scalar_prefetch=2,
            grid=(nb,),
            in_specs=[pl.BlockSpec((bm, D_PACK), lambda i, be, rv: (i, 0)),
                      pl.BlockSpec((None, None, D_MODEL, 2 * D_FF), lambda i, be, rv: (layer, be[i], 0, 0)),
                      pl.BlockSpec((None, None, 1, 2 * D_FF), lambda i, be, rv: (layer, be[i], 0, 0)),
                      pl.BlockSpec((None, None, D_FF, D_MODEL), lambda i, be, rv: (layer, be[i], 0, 0)),
                      pl.BlockSpec((None, None, 1, D_MODEL), lambda i, be, rv: (layer, be[i], 0, 0))],
            out_specs=pl.BlockSpec((bm, D_PACK), lambda i, be, rv: (i, 0)),
            scratch_shapes=[pltpu.VMEM((D_MODEL, 2 * D_FF), BF16), pltpu.VMEM((D_FF, D_MODEL), BF16)],
        ),
        out_shape=jax.ShapeDtypeStruct((nb * bm, D_PACK), F32),
        compiler_params=_cparams(("arbitrary",)),
        name="moe_experts",
    )(block_e, rows_valid, x_pad, w1, b1, w2, b2)


ROUTE_E, ROUTE_RANK, ROUTE_GATE = 0, TOP_K, 2 * TOP_K


def _router_kernel(lg_ref, route_ref, route_t_ref, cnt_ref, cnt_sc):
    @pl.when(pl.program_id(0) == 0)
    def _():
        cnt_sc[...] = jnp.zeros(cnt_sc.shape, F32)

    work = lg_ref[...]
    tm = work.shape[0]
    lane = _iota2(work.shape, 1)
    lane_f = lane.astype(F32)
    sels, vals, ids = [], [], []
    for _ in range(TOP_K):
        mx = jnp.max(work, axis=1, keepdims=True)
        idx = jnp.min(jnp.where(work == mx, lane_f, float(LANES)), axis=1, keepdims=True)
        sel = lane_f == idx
        sels.append(sel)
        vals.append(mx)
        ids.append(idx)
        work = jnp.where(sel, -jnp.inf, work)
    ex = [jnp.exp(v - vals[0]) for v in vals]
    den = ex[0]
    for x in ex[1:]:
        den = den + x
    picked = jnp.zeros(work.shape, F32)
    for sel in sels:
        picked = jnp.where(sel, 1.0, picked)
    r, c = _iota2((tm, tm), 0), _iota2((tm, tm), 1)
    before = jnp.dot(jnp.where(c < r, 1.0, 0.0).astype(BF16), picked.astype(BF16), preferred_element_type=F32)
    before = before + cnt_sc[...]
    out = jnp.zeros(work.shape, F32)
    for k in range(TOP_K):
        rank = jnp.sum(jnp.where(sels[k], before, 0.0), axis=1, keepdims=True)
        out = jnp.where(lane == ROUTE_E + k, ids[k], out)
        out = jnp.where(lane == ROUTE_RANK + k, rank, out)
        out = jnp.where(lane == ROUTE_GATE + k, ex[k] / den, out)
    route_ref[...] = out
    route_t_ref[...] = out.T[0:ROUTE_GATE, :]
    cnt_sc[...] = cnt_sc[...] + jnp.sum(picked, axis=0, keepdims=True)
    cnt_ref[...] = cnt_sc[...]


def _router(logits):
    t = logits.shape[0]
    tm = min(512, t)
    return pl.pallas_call(
        _router_kernel,
        grid=(t // tm,),
        in_specs=[pl.BlockSpec((tm, LANES), lambda i: (i, 0))],
        out_specs=[pl.BlockSpec((tm, LANES), lambda i: (i, 0)), pl.BlockSpec((ROUTE_GATE, tm), lambda i: (0, i)),
                   pl.BlockSpec((1, LANES), lambda i: (0, 0))],
        out_shape=[jax.ShapeDtypeStruct((t, LANES), F32), jax.ShapeDtypeStruct((ROUTE_GATE, t), F32),
                   jax.ShapeDtypeStruct((1, LANES), F32)],
        scratch_shapes=[pltpu.VMEM((1, LANES), F32)],
        compiler_params=_cparams(("arbitrary",)),
        name="router",
    )(logits)


def _route(logits, bm):
    t = logits.shape[0]
    m = t * TOP_K
    route, route_t, cnt = _router(logits)
    e_t = route_t[ROUTE_E:ROUTE_E + TOP_K].astype(jnp.int32)
    rank_t = route_t[ROUTE_RANK:ROUTE_RANK + TOP_K].astype(jnp.int32)
    counts = cnt[0, :N_EXPERTS].astype(jnp.int32)
    padded = ((counts + bm - 1) // bm) * bm
    pad_end = jnp.cumsum(padded)
    pad_start = pad_end - padded
    experts = jnp.arange(N_EXPERTS)[:, None, None]
    dest = jnp.sum(jnp.where(e_t[None] == experts, pad_start[:, None, None], 0), axis=0) + rank_t
    nb = -(-m // bm) + N_EXPERTS
    block_e = jnp.sum((pad_end[None, :] <= (jnp.arange(nb) * bm)[:, None]).astype(jnp.int32), axis=1)
    block_e = jnp.minimum(block_e, N_EXPERTS - 1)
    n_used = (pad_end[-1] // bm).astype(jnp.int32).reshape(1)
    block_e = jnp.where(jnp.arange(nb) < n_used[0], block_e, block_e[jnp.maximum(n_used[0] - 1, 0)])
    rows_valid = jnp.clip((pad_start + counts)[block_e] - jnp.arange(nb) * bm, 0, bm).astype(jnp.int32)
    return route, dest, block_e.astype(jnp.int32), rows_valid


def _ln2_kernel(h_ref, gate_ref, *rest, alpha):
    y_refs, (g_ref, b_ref, o_ref) = rest[:TOP_K], rest[TOP_K:]
    gate = gate_ref[...]
    x = alpha * h_ref[...]
    for k in range(TOP_K):
        x = x + gate[:, ROUTE_GATE + k:ROUTE_GATE + k + 1] * _unpack_bf16_pairs(y_refs[k][...])
    o_ref[...] = _ln_rows(x, g_ref[...], b_ref[...])


def _ln2(h, gate, yg, row0, g, b, alpha):
    t = h.shape[0]
    t_all = yg.shape[0] // TOP_K
    tm = min(512, t)
    assert row0 % tm == 0 and t_all % tm == 0
    off = row0 // tm
    nt_all = t_all // tm
    y_specs = [pl.BlockSpec((tm, D_PACK), functools.partial(lambda i, k: (k * nt_all + off + i, 0), k=k))
               for k in range(TOP_K)]
    return pl.pallas_call(
        functools.partial(_ln2_kernel, alpha=alpha),
        grid=(t // tm,),
        in_specs=[pl.BlockSpec((tm, D_MODEL), lambda i: (i, 0)),
                  pl.BlockSpec((tm, LANES), lambda i: (i + off, 0))] + y_specs +
                 [pl.BlockSpec((1, D_MODEL), lambda i: (0, 0)),
                  pl.BlockSpec((1, D_MODEL), lambda i: (0, 0))],
        out_specs=pl.BlockSpec((tm, D_MODEL), lambda i: (i, 0)),
        out_shape=jax.ShapeDtypeStruct((t, D_MODEL), F32),
        compiler_params=_cparams(("parallel",)),
        name="ln2",
    )(h, gate, *([yg] * TOP_K), g, b)


SC_CORES, SC_SUBCORES = 2, 16
SC_CHUNK = 64


def _sc_gather(table, idx):
    b, d = idx.shape[0], table.shape[1]
    workers = SC_CORES * SC_SUBCORES
    per_w = b // workers
    assert per_w * workers == b and per_w % SC_CHUNK == 0
    mesh = plsc.VectorSubcoreMesh(core_axis_name="c", subcore_axis_name="s")

    n_chunks = per_w // SC_CHUNK

    @functools.partial(
        pl.kernel, mesh=mesh, out_type=jax.ShapeDtypeStruct((b, d), table.dtype),
        scratch_types=[pltpu.VMEM((per_w,), jnp.int32), pltpu.VMEM((2, SC_CHUNK, d), table.dtype),
                       pltpu.SemaphoreType.DMA((2,))],
        name="sc_gather")
    def gather(table_hbm, idx_hbm, out_hbm, idx_v, rows_v, sems):
        wid = lax.axis_index("s") * SC_CORES + lax.axis_index("c")
        pltpu.sync_copy(idx_hbm.at[pl.ds(wid * per_w, per_w)], idx_v)

        def fetch(i, slot):
            return pltpu.make_async_copy(table_hbm.at[idx_v.at[pl.ds(i * SC_CHUNK, SC_CHUNK)]],
                                         rows_v.at[slot], sems.at[slot])

        fetch(0, 0).start()

        @pl.loop(0, n_chunks)
        def _(i):
            slot = lax.rem(i, 2)

            @pl.when(i + 1 < n_chunks)
            def _():
                fetch(i + 1, 1 - slot).start()

            fetch(i, slot).wait()
            pltpu.sync_copy(rows_v.at[slot], out_hbm.at[pl.ds(wid * per_w + i * SC_CHUNK, SC_CHUNK)])

    return gather(table, idx)


SC_SCATTER_CHUNK = 48


def _sc_scatter_rows(rows, dest, n_out):
    t, d = rows.shape
    workers = SC_CORES * SC_SUBCORES
    per_w = t // workers
    assert per_w * workers == t and per_w % SC_SCATTER_CHUNK == 0
    mesh = plsc.VectorSubcoreMesh(core_axis_name="c", subcore_axis_name="s")

    @functools.partial(
        pl.kernel, mesh=mesh, out_type=jax.ShapeDtypeStruct((n_out, d), rows.dtype),
        scratch_types=[pltpu.VMEM((TOP_K, SC_SCATTER_CHUNK), jnp.int32),
                       pltpu.VMEM((SC_SCATTER_CHUNK, d), rows.dtype)],
        name="sc_scatter")
    def scatter(rows_hbm, dest_hbm, out_hbm, idx_v, rows_v):
        wid = lax.axis_index("s") * SC_CORES + lax.axis_index("c")

        @pl.loop(0, per_w // SC_SCATTER_CHUNK)
        def _(i):
            base = wid * per_w + i * SC_SCATTER_CHUNK
            pltpu.sync_copy(rows_hbm.at[pl.ds(base, SC_SCATTER_CHUNK)], rows_v)
            for k in range(TOP_K):
                pltpu.sync_copy(dest_hbm.at[pl.ds(k * t + base, SC_SCATTER_CHUNK)], idx_v.at[k])
            for k in range(TOP_K):
                pltpu.sync_copy(rows_v, out_hbm.at[idx_v.at[k]])

    return scatter(rows, dest.reshape(-1))


def _rope_tables(pos):
    half = ROT_DIM // 2
    inv_freq = ROPE_THETA ** (-jnp.arange(0, ROT_DIM, 2, dtype=F32) / ROT_DIM)
    ang = pos.astype(F32)[:, None] * inv_freq[None, :]
    cos, sin = jnp.cos(ang), jnp.sin(ang)
    m = np.arange(LANES) % DQK_A
    idx = m % half
    cos_l = jnp.where(m < ROT_DIM, cos[:, idx], 1.0)
    sa = jnp.where(m < half, -sin[:, idx], 0.0)
    sb = jnp.where((m >= half) & (m < ROT_DIM), sin[:, idx], 0.0)
    return cos_l, sa, sb


def _prep_w_in(w):
    r0 = COL_C
    r1 = r0 + GATE_RANK
    pad = jnp.zeros((w.shape[0], LANES - GATE_RANK), w.dtype)
    return jnp.concatenate([w[:, :r0], w[:, r1:], w[:, r0:r1], pad], axis=1).astype(BF16)


def _tile_lanes(v, reps):
    return jnp.tile(v.reshape(1, -1), (1, reps)).astype(F32)


def _blockdiag_states(s):
    n = s.shape[0]
    eye = jnp.eye(HB, dtype=s.dtype)
    return jnp.einsum('nhde,hg->nhdge', s, eye).reshape(n, HB * DK_B, HB * DV_B)


def _diag_states(sbd):
    n = sbd.shape[0]
    s = sbd.reshape(n, HB, DK_B, HB, DV_B)
    return jnp.stack([s[:, h, :, h, :] for h in range(HB)], axis=1)


def kernel(x_prompt, x_sample, cache_k, cache_v, page_table, state_gla, w_in, lam_q1, lam_k1, lam_q2, lam_k2, attn_norm_g, gla_w_gate, gla_b_gate, gla_norm_g, cmlp_ln_g, cmlp_ln_b, cmlp_ws, cmlp_bs, w_o, ln1_g, ln1_b, router_w, router_b, exp_w1, exp_b1, exp_w2, exp_b2, ln2_g, ln2_b):
    depth = w_in.shape[0]
    bp, s_len, _ = x_prompt.shape
    db, l_new, _ = x_sample.shape
    n_phys, page = cache_k.shape[1], cache_k.shape[2]
    past_len = page_table.shape[1] * page
    alpha = (2 * depth) ** 0.25
    tp, ts = bp * s_len, db * l_new
    bm = MOE_BLOCK

    tabs_p = _rope_tables(jnp.arange(s_len))
    tabs_s = _rope_tables(past_len + (jnp.arange(ts) % l_new))
    page_table = page_table.astype(jnp.int32)

    hp = x_prompt.reshape(tp, D_MODEL)
    hs = x_sample.reshape(ts, D_MODEL)
    outs = {k: [] for k in ("gp", "cs")}
    gs_all = state_gla.astype(F32).reshape(depth * db, HB, DK_B, DV_B)
    kp_all, vp_all = (jnp.zeros((depth * tp, HA, DV_A), F32) for _ in range(2))
    ks_all, vs_all = (jnp.zeros((depth * ts, HA, DV_A), F32) for _ in range(2))
    for l in range(depth):
        lam_init = 0.8 - 0.6 * math.exp(-0.3 * l)
        w = _prep_w_in(w_in[l])
        lamv = jnp.pad(jnp.stack([lam_q1[l], lam_k1[l], lam_q2[l], lam_k2[l]]).astype(F32),
                       ((0, 0), (0, LANES - DQK_A)))
        g_attn = attn_norm_g[l].reshape(1, DV_A).astype(F32)
        wg = jnp.pad(gla_w_gate[l], ((0, LANES - GATE_RANK), (0, 0))).astype(BF16)
        wo = w_o[l].astype(BF16)
        rw = jnp.pad(router_w[l].astype(F32), ((0, 0), (0, LANES - N_EXPERTS)))
        rw_hi = rw.astype(BF16)
        rw = jnp.stack([rw_hi, (rw - rw_hi.astype(F32)).astype(BF16)])
        rb = jnp.pad(router_b[l].astype(F32), (0, LANES - N_EXPERTS), constant_values=NEG_INF).reshape(1, LANES)
        ln1 = (ln1_g[l].reshape(1, D_MODEL), ln1_b[l].reshape(1, D_MODEL))

        def mixer_params(lc, n_rows):
            reps = n_rows // lc
            ws = jnp.tile(cmlp_ws[l][:, :lc, :lc], (1, reps, reps))
            bst = jnp.tile(jnp.repeat(cmlp_bs[l][:, :lc].T, DC, axis=1), (reps, 1))
            return (wg, gla_b_gate[l].reshape(1, W_B), _tile_lanes(gla_norm_g[l], HB),
                    _tile_lanes(cmlp_ln_g[l], HC), _tile_lanes(cmlp_ln_b[l], HC), ws, bst)

        q, kp_all, vp_all, kb, vb, g_in, c_in, br = _inproj(hp, w, tabs_p, l, kp_all, vp_all)
        o_a = _attn_prompt(q, kb, vb, lamv, g_attn, bp, lam_init)
        o_bc, st_p = _mixer_prompt(g_in, br, c_in, mixer_params(CMLP_CHUNK, CMLP_CHUNK), bp)
        rows_k, logits = jnp.zeros((tp + ts, D_PACK), F32), jnp.zeros((tp + ts, LANES), F32)
        hp1, rows_k, logits = _outproj(o_a, o_bc, hp, wo, *ln1, rw, rb, alpha, 0, rows_k, logits)
        outs["gp"].append(_diag_states(st_p))

        q, ks_all, vs_all, kb, vb, g_in, c_in, br = _inproj(hs, w, tabs_s, l, ks_all, vs_all)
        o_a = _attn_sample(q, kb, vb, cache_k, cache_v, l, page_table, lamv, g_attn, l_new, lam_init)
        rows_s = min(64, ts)
        lc = min(l_new, CMLP_CHUNK)
        o_bc, vn, gs_all = _mixer_sample(g_in, br, c_in, gs_all, l, mixer_params(lc, rows_s), l_new, rows_s)
        hs1, rows_k, logits = _outproj(o_a, o_bc, hs, wo, *ln1, rw, rb, alpha, tp, rows_k, logits)
        outs["cs"].append(vn.reshape(db, l_new, W_C))

        gate, dest, block_e, rows_valid = _route(logits, bm)
        x_pad = _sc_scatter_rows(rows_k, dest, block_e.shape[0] * bm)
        y_pad = _moe_experts(x_pad, block_e, rows_valid, exp_w1, exp_b1.reshape(depth, N_EXPERTS, 1, -1),
                             exp_w2, exp_b2.reshape(depth, N_EXPERTS, 1, -1), l, bm)
        yg = _sc_gather(y_pad, dest.reshape(-1))
        hp = _ln2(hp1, gate, yg, 0, ln2_g[l].reshape(1, -1), ln2_b[l].reshape(1, -1), alpha)
        hs = _ln2(hs1, gate, yg, tp, ln2_g[l].reshape(1, -1), ln2_b[l].reshape(1, -1), alpha)

    return (hp.reshape(bp, s_len, D_MODEL), hs.reshape(db, l_new, D_MODEL),
            kp_all.reshape(depth, bp, s_len, HA, DV_A), vp_all.reshape(depth, bp, s_len, HA, DV_A),
            jnp.stack(outs["gp"]),
            ks_all.reshape(depth, db, l_new, HA, DV_A), vs_all.reshape(depth, db, l_new, HA, DV_A),
            gs_all.reshape(depth, db, HB, DK_B, DV_B), jnp.stack(outs["cs"]))
```

```python
import functools
import math

import numpy as np
import jax
import jax.numpy as jnp
from jax import lax
from jax.experimental import pallas as pl
from jax.experimental.pallas import tpu as pltpu
from jax.experimental.pallas import tpu_sc as plsc

F32, BF16 = jnp.float32, jnp.bfloat16
LANES = 128
VMEM_LIMIT = 48 * 1024 * 1024

D_MODEL = 1024
HA, DQK_A, DV_A = 4, 64, 128
ROT_DIM = DQK_A // 4
ROPE_THETA = 500000.0
HB, DK_B, DV_B = 4, 64, 64
GATE_RANK = 16
GATE_NORM = 16.0
HC, DC = 4, 64
CMLP_CHUNK = 128
N_EXPERTS = 32
TOP_K = 4
D_FF = D_MODEL
SWIGLU_LIMIT = 7.0
SWIGLU_ALPHA = 1.702
LN_EPS = 1e-5
RMS_EPS = 1e-6
NEG_INF = -1e30
LOG2E = math.log2(math.e)

W_A = HA * 2 * DQK_A
W_B = HB * DK_B
W_C = HC * DC
COL_G = 3 * W_A
COL_C = COL_G + 4 * W_B
COL_R = COL_C + 2 * W_C
COL_END = COL_R + LANES
GLA_CHUNK_PROMPT = 32
ATTN_BQ, ATTN_BK = 512, 512
ATTN_ROWS = 2 * ATTN_BQ
D_PACK = D_MODEL // 2
MOE_BLOCK = 512
MOE_FF_CHUNK = 512


def _cparams(sem):
    return pltpu.CompilerParams(dimension_semantics=sem, vmem_limit_bytes=VMEM_LIMIT)


def _split3(x):
    hi = x.astype(BF16)
    r = x - hi.astype(F32)
    mid = r.astype(BF16)
    lo = (r - mid.astype(F32)).astype(BF16)
    return hi, mid, lo


def _dot_sel(sel_bf16, x):
    acc = None
    for p in _split3(x):
        d = jnp.dot(sel_bf16, p, preferred_element_type=F32)
        acc = d if acc is None else acc + d
    return acc


def _seg_sum(x, bd_bf16):
    acc = None
    for p in _split3(x):
        d = jnp.dot(p, bd_bf16, preferred_element_type=F32)
        acc = d if acc is None else acc + d
    return acc


def _pack_bf16_pairs(x):
    u = lax.bitcast_convert_type(x, jnp.uint32)
    r = u + (jnp.uint32(0x7FFF) + ((u >> 16) & jnp.uint32(1)))
    w = x.shape[1] // 2
    word = (r[:, :w] & jnp.uint32(0xFFFF0000)) | (r[:, w:] >> 16)
    return lax.bitcast_convert_type(word, F32)


def _unpack_bf16_pairs(words):
    u = lax.bitcast_convert_type(words, jnp.uint32)
    hi = lax.bitcast_convert_type(u & jnp.uint32(0xFFFF0000), F32)
    lo = lax.bitcast_convert_type(u << 16, F32)
    return jnp.concatenate([hi, lo], axis=1)


def _iota2(shape, dim):
    return lax.broadcasted_iota(jnp.int32, shape, dim)


def _idiv(x, n):
    shift = n.bit_length() - 1
    assert n == 1 << shift
    return x >> shift


def _head_blockdiag(n):
    r, c = _iota2((n, n), 0), _iota2((n, n), 1)
    return _idiv(r, DK_B) == _idiv(c, DK_B)


def _chunk_causal(n, chunk):
    r, c = _iota2((n, n), 0), _iota2((n, n), 1)
    return (_idiv(r, chunk) == _idiv(c, chunk)) & (c <= r)


def _ln_rows(x, g, b):
    mu = jnp.mean(x, axis=-1, keepdims=True)
    xc = x - mu
    var = jnp.mean(xc * xc, axis=-1, keepdims=True)
    return xc * lax.rsqrt(var + LN_EPS) * g + b


def _inproj_kernel(x_ref, w_ref, cos_ref, sa_ref, sb_ref, k_all_ref, v_all_ref,
                   q_ref, k_ref, v_ref, kb_ref, vb_ref, g_ref, c_ref, br_ref):
    del k_all_ref, v_all_ref
    xb = x_ref[...].astype(BF16)

    def proj(a, b):
        return jnp.dot(xb, w_ref[:, a:b], preferred_element_type=F32)

    cos, sa, sb = cos_ref[...], sa_ref[...], sb_ref[...]

    def rope(z):
        outs = []
        for i in range(z.shape[1] // LANES):
            zi = z[:, i * LANES:(i + 1) * LANES]
            outs.append(zi * cos + pltpu.roll(zi, LANES - ROT_DIM // 2, 1) * sa
                        + pltpu.roll(zi, ROT_DIM // 2, 1) * sb)
        return jnp.concatenate(outs, axis=1)

    q_ref[...] = (rope(proj(0, W_A)) * (DQK_A ** -0.5 * LOG2E)).astype(BF16)
    k = rope(proj(W_A, 2 * W_A))
    kb_ref[...] = k.astype(BF16)
    v = proj(2 * W_A, 3 * W_A)
    vb_ref[...] = v.astype(BF16)
    for h in range(HA):
        k_ref[:, h, :] = k[:, h * LANES:(h + 1) * LANES]
        v_ref[:, h, :] = v[:, h * LANES:(h + 1) * LANES]
    g_ref[...] = proj(COL_G, COL_C)
    c_ref[...] = proj(COL_C, COL_R)
    br_ref[...] = proj(COL_R, COL_END)


def _inproj(x, w, tabs, layer, k_all, v_all):
    t = x.shape[0]
    tm = min(512, t)
    nt = t // tm
    cos, sa, sb = tabs
    ntab = cos.shape[0] // tm
    row = lambda n: pl.BlockSpec((tm, n), lambda i: (i, 0))
    tab = pl.BlockSpec((tm, LANES), lambda i: (i % ntab, 0))
    heads = pl.BlockSpec((tm, HA, DV_A), lambda i: (layer * nt + i, 0, 0))
    anywhere = pl.BlockSpec(memory_space=pl.ANY)
    shapes = [((W_A,), BF16), None, None, ((W_A,), BF16), ((W_A,), BF16),
              ((4 * W_B,), F32), ((2 * W_C,), F32), ((LANES,), F32)]
    return pl.pallas_call(
        _inproj_kernel,
        grid=(nt,),
        in_specs=[row(D_MODEL), pl.BlockSpec((D_MODEL, COL_END), lambda i: (0, 0)), tab, tab, tab,
                  anywhere, anywhere],
        out_specs=[heads if s is None else row(s[0][0]) for s in shapes],
        out_shape=[jax.ShapeDtypeStruct(k_all.shape, F32) if s is None else jax.ShapeDtypeStruct((t,) + s[0], s[1])
                   for s in shapes],
        input_output_aliases={5: 1, 6: 2},
        compiler_params=_cparams(("parallel",)),
        name="inproj",
    )(x, w, cos, sa, sb, k_all, v_all)


def _diff_lambda(lamv, lam_init):
    a = jnp.sum(lamv[0:1] * lamv[1:2], axis=1, keepdims=True)
    b = jnp.sum(lamv[2:3] * lamv[3:4], axis=1, keepdims=True)
    return jnp.exp(a) - jnp.exp(b) + lam_init


def _diff_finish(o1, o2, lam, g, lam_init):
    o = o1 - lam * o2
    ms = jnp.mean(o * o, axis=-1, keepdims=True)
    return o * lax.rsqrt(ms + RMS_EPS) * g * (1.0 - lam_init)


def _split_maps(q):
    lane = _iota2(q.shape, 1)
    zero = jnp.zeros_like(q)
    return jnp.concatenate([jnp.where(lane < DQK_A, q, zero), jnp.where(lane >= DQK_A, q, zero)], axis=0)


def _attn_prompt_kernel(q_ref, k_ref, v_ref, lamv_ref, g_ref, o_ref, m_sc, l_sc, acc_sc, *, bq, bk, lam_init):
    qi = pl.program_id(2)
    qq = _split_maps(q_ref[...])
    m_sc[...] = jnp.full(m_sc.shape, NEG_INF, F32)
    l_sc[...] = jnp.zeros(l_sc.shape, F32)
    acc_sc[...] = jnp.zeros(acc_sc.shape, F32)

    def step(j, masked):
        start = pl.multiple_of(j * bk, bk)
        k = k_ref[pl.ds(start, bk), :]
        v = v_ref[pl.ds(start, bk), :]
        tiles = range(bk // LANES)
        for rc in range(2 * bq // ATTN_ROWS):
            rs = slice(rc * ATTN_ROWS, (rc + 1) * ATTN_ROWS)
            s = lax.dot_general(qq[rs], k, (((1,), (1,)), ((), ())), preferred_element_type=F32)
            if masked:
                r = (_iota2(s.shape, 0) + rc * ATTN_ROWS) & (bq - 1)
                c = _iota2(s.shape, 1)
                s = jnp.where(c <= r, s, NEG_INF)
            m_prev = m_sc[rs, :]
            s_max = functools.reduce(jnp.maximum, [s[:, i * LANES:(i + 1) * LANES] for i in tiles])
            m_new = jnp.maximum(m_prev, jnp.max(s_max, axis=1, keepdims=True))
            alpha = jnp.exp2(m_prev - m_new)
            p = jnp.exp2((s - jnp.tile(m_new, (1, bk // LANES))).astype(BF16))
            p_sum = functools.reduce(jnp.add, [p[:, i * LANES:(i + 1) * LANES].astype(F32) for i in tiles])
            l_sc[rs, :] = alpha * l_sc[rs, :] + jnp.sum(p_sum, axis=1, keepdims=True)
            acc_sc[rs, :] = alpha * acc_sc[rs, :] + jnp.dot(p, v, preferred_element_type=F32)
            m_sc[rs, :] = m_new

    def body(j, carry):
        step(j, False)
        return carry

    lax.fori_loop(0, qi, body, 0)
    step(qi, True)

    o = acc_sc[...] / l_sc[...]
    lam = _diff_lambda(lamv_ref[...], lam_init)
    o_ref[...] = _diff_finish(o[:bq], o[bq:], lam, g_ref[...], lam_init).astype(o_ref.dtype)


def _attn_prompt(q, kb, vb, lamv, g, nbatch, lam_init):
    t = q.shape[0]
    s = t // nbatch
    bq, bk = min(ATTN_BQ, s), min(ATTN_BK, s)
    assert bq == bk
    blk = bq
    nq = s // blk
    kern = functools.partial(_attn_prompt_kernel, bq=bq, bk=bk, lam_init=lam_init)
    return pl.pallas_call(
        kern,
        grid=(nbatch, HA, nq),
        in_specs=[pl.BlockSpec((blk, LANES), lambda b, h, i: (b * nq + i, h)),
                  pl.BlockSpec((s, LANES), lambda b, h, i: (b, h)),
                  pl.BlockSpec((s, LANES), lambda b, h, i: (b, h)),
                  pl.BlockSpec((4, LANES), lambda b, h, i: (0, 0)),
                  pl.BlockSpec((1, LANES), lambda b, h, i: (0, 0))],
        out_specs=pl.BlockSpec((blk, LANES), lambda b, h, i: (b * nq + i, h)),
        out_shape=jax.ShapeDtypeStruct((t, W_A), BF16),
        scratch_shapes=[pltpu.VMEM((2 * blk, LANES), F32)] * 3,
        compiler_params=_cparams(("parallel", "parallel", "arbitrary")),
        name="attn_prompt",
    )(q, kb, vb, lamv, g)


def _attn_sample_kernel(pt_ref, q_ref, kn_ref, vn_ref, lamv_ref, g_ref, *rest, n_pages, l_new, lam_init):
    del pt_ref
    kp = rest[:n_pages]
    vp = rest[n_pages:2 * n_pages]
    o_ref = rest[2 * n_pages]
    rows_pg = kp[0].shape[0]
    rpad = q_ref.shape[0]
    nq = 2 * rpad
    lam = _diff_lambda(lamv_ref[...], lam_init)
    q = q_ref[...]
    qq = jnp.concatenate([_split_maps(q[:, h * LANES:(h + 1) * LANES]) for h in range(HA)],
                         axis=0).astype(BF16)
    r = _iota2((HA * nq, rows_pg), 0)
    c = _iota2((HA * nq, rows_pg), 1)
    head_ok = (c & (HA - 1)) == _idiv(r, nq)
    new_ok = head_ok & (_idiv(c, HA) <= (r & (rpad - 1))) & (c < l_new * HA)
    nt = (((1,), (1,)), ((), ()))
    zpad = jnp.zeros((rows_pg - kn_ref.shape[0], LANES), F32)
    k_new = jnp.concatenate([kn_ref[...], zpad], axis=0).astype(BF16)
    v_new = jnp.concatenate([vn_ref[...], zpad], axis=0).astype(BF16)
    s_new = jnp.where(new_ok, lax.dot_general(qq, k_new, nt, preferred_element_type=F32), NEG_INF)
    s_past = [jnp.where(head_ok, lax.dot_general(qq, kp[j][...].astype(BF16), nt, preferred_element_type=F32),
                        NEG_INF) for j in range(n_pages)]
    m = jnp.max(s_new, axis=1, keepdims=True)
    for sj in s_past:
        m = jnp.maximum(m, jnp.max(sj, axis=1, keepdims=True))
    p_new = jnp.exp2(s_new - m)
    l = jnp.sum(p_new, axis=1, keepdims=True)
    acc = jnp.dot(p_new.astype(BF16), v_new, preferred_element_type=F32)
    for j in range(n_pages):
        pj = jnp.exp2(s_past[j] - m)
        l = l + jnp.sum(pj, axis=1, keepdims=True)
        acc = acc + jnp.dot(pj.astype(BF16), vp[j][...].astype(BF16), preferred_element_type=F32)
    o = acc / l
    outs = [_diff_finish(o[h * nq:h * nq + rpad], o[h * nq + rpad:(h + 1) * nq], lam, g_ref[...], lam_init)
            for h in range(HA)]
    o_ref[...] = jnp.concatenate(outs, axis=1)


def _attn_sample(q, k, v, cache_k, cache_v, layer, page_table, lamv, g, l_new, lam_init):
    t = q.shape[0]
    db = t // l_new
    n_pages = page_table.shape[1]
    page = cache_k.shape[2]
    rpad = 8

    def pad_rows(a):
        a = a.reshape(db, l_new, W_A).astype(F32)
        return jnp.concatenate([a, jnp.zeros((db, rpad - l_new, W_A), F32)], axis=1)

    def new_rows(a):
        return a.reshape(db, l_new * HA, DV_A).astype(F32)

    depth, n_phys = cache_k.shape[:2]
    cache_k = cache_k.reshape(depth, n_phys, page * HA, DV_A)
    cache_v = cache_v.reshape(depth, n_phys, page * HA, DV_A)
    q_spec = pl.BlockSpec((None, rpad, W_A), lambda b, pt: (b, 0, 0))
    new_spec = pl.BlockSpec((None, l_new * HA, DV_A), lambda b, pt: (b, 0, 0))
    page_specs = [pl.BlockSpec((None, None, page * HA, DV_A),
                               functools.partial(lambda b, pt, j: (layer, pt[b, j], 0, 0), j=j))
                  for j in range(n_pages)]
    kern = functools.partial(_attn_sample_kernel, n_pages=n_pages, l_new=l_new, lam_init=lam_init)
    out = pl.pallas_call(
        kern,
        grid_spec=pltpu.PrefetchScalarGridSpec(
            num_scalar_prefetch=1,
            grid=(db,),
            in_specs=[q_spec, new_spec, new_spec,
                      pl.BlockSpec((4, LANES), lambda b, pt: (0, 0)),
                      pl.BlockSpec((1, LANES), lambda b, pt: (0, 0))] + page_specs + page_specs,
            out_specs=q_spec,
        ),
        out_shape=jax.ShapeDtypeStruct((db, rpad, W_A), F32),
        compiler_params=_cparams(("arbitrary",)),
        name="attn_sample",
    )(page_table, pad_rows(q), new_rows(k), new_rows(v), lamv, g,
      *([cache_k] * n_pages), *([cache_v] * n_pages))
    return out[:, :l_new].reshape(t, W_A).astype(BF16)


def _log_sigmoid(x):
    return jnp.minimum(x, 0.0) - jnp.log(1.0 + jnp.exp(-jnp.abs(x)))


def _gla_gate(br, wg_ref, bgate_ref):
    x = jnp.dot(br.astype(BF16), wg_ref[...], preferred_element_type=F32) + bgate_ref[...]
    return _log_sigmoid(x) / GATE_NORM


def _gla_intra(q_att, k_in, v, chunk):
    keep = _chunk_causal(q_att.shape[0], chunk)
    lane = _idiv(_iota2(q_att.shape, 1), DK_B)
    kb = k_in.astype(BF16)
    atts, vs = [], []
    for h in range(HB):
        qh = jnp.where(lane == h, q_att, 0.0).astype(BF16)
        a = lax.dot_general(qh, kb, (((1,), (1,)), ((), ())), preferred_element_type=F32)
        atts.append(jnp.where(keep, a, 0.0).astype(BF16))
        vs.append(jnp.where(lane == h, v, 0.0).astype(BF16))
    return jnp.dot(jnp.concatenate(atts, axis=1), jnp.concatenate(vs, axis=0), preferred_element_type=F32)


def _gla_finish(o, gate_in, gng_ref, bd):
    ms = _seg_sum(o * o, bd) * (1.0 / DV_B)
    o = o * lax.rsqrt(ms + RMS_EPS) * gng_ref[...]
    return o * (gate_in * (1.0 / (1.0 + jnp.exp(-gate_in))))


def _chunk_mlp(c_in, lng_ref, lnb_ref, ws_ref, bst, chunk, bd):
    n = c_in.shape[0]
    cu, cv = c_in[:, :W_C], c_in[:, W_C:]
    mu = _seg_sum(cv, bd) * (1.0 / DC)
    xc = cv - mu
    var = _seg_sum(xc * xc, bd) * (1.0 / DC)
    vn = xc * lax.rsqrt(var + LN_EPS) * lng_ref[...] + lnb_ref[...]
    keep = _chunk_causal(n, chunk)
    lane = _idiv(_iota2(vn.shape, 1), DC)
    ws, vs = [], []
    for g in range(HC):
        ws.append(jnp.where(keep, ws_ref[g], 0.0).astype(BF16))
        vs.append(jnp.where(lane == g, vn, 0.0).astype(BF16))
    mixed = jnp.dot(jnp.concatenate(ws, axis=1), jnp.concatenate(vs, axis=0), preferred_element_type=F32) + bst
    return cu * mixed, vn


def _mixer_prompt_kernel(g_ref, br_ref, c_ref, wg_ref, bgate_ref, gng_ref, lng_ref, lnb_ref, ws_ref, bst_ref,
                         o_ref, st_ref, st_sc, *, ts, chunk):
    t = pl.program_id(1)

    @pl.when(t == 0)
    def _():
        st_sc[...] = jnp.zeros(st_sc.shape, F32)

    grp = CMLP_CHUNK
    bd = _head_blockdiag(W_B)
    bd_bf = jnp.where(bd, 1.0, 0.0).astype(BF16)
    csum_sel = jnp.where(_chunk_causal(grp, chunk), 1.0, 0.0).astype(BF16)
    rows = _iota2((grp, W_B), 0)
    half = chunk // 2
    for gi in range(ts // grp):
        rs = slice(gi * grp, (gi + 1) * grp)
        g = g_ref[rs, :]
        gq = g[:, 0:W_B] * (DK_B ** -0.5)
        gk, gv, gg = g[:, W_B:2 * W_B], g[:, 2 * W_B:3 * W_B], g[:, 3 * W_B:4 * W_B]
        la = _gla_gate(br_ref[rs, :], wg_ref, bgate_ref)
        bcum = _dot_sel(csum_sel, la)
        mids, lasts = [], []
        for ci in range(grp // chunk):
            mids.append(jnp.broadcast_to(bcum[ci * chunk + half - 1:ci * chunk + half, :], (chunk, W_B)))
            lasts.append(jnp.broadcast_to(bcum[(ci + 1) * chunk - 1:(ci + 1) * chunk, :], (chunk, W_B)))
        bmid = jnp.concatenate(mids, axis=0)
        blast = jnp.concatenate(lasts, axis=0)
        q_att = gq * jnp.exp(bcum - bmid)
        k_in = gk * jnp.exp(bmid - bcum)
        k_end = gk * jnp.exp(blast - bcum)
        q_dec = (gq * jnp.exp(bcum)).astype(BF16)
        o = _gla_intra(q_att, k_in, gv, chunk)
        v_t = gv.T.astype(BF16)
        o_inter = []
        for ci in range(grp // chunk):
            cs = slice(ci * chunk, (ci + 1) * chunk)
            st = st_sc[...]
            o_inter.append(lax.dot_general(q_dec[cs], st.astype(BF16), (((1,), (1,)), ((), ())),
                                           preferred_element_type=F32))
            kem = jnp.where(_idiv(rows, chunk) == ci, k_end, 0.0).astype(BF16)
            upd = jnp.dot(v_t, kem, preferred_element_type=F32)
            dl = jnp.exp(blast[ci * chunk:ci * chunk + 1, :])
            st_sc[...] = st * dl + jnp.where(bd, upd, 0.0)
        o = o + jnp.concatenate(o_inter, axis=0)
        o_b = _gla_finish(o, gg, gng_ref, bd_bf)
        o_c, _ = _chunk_mlp(c_ref[rs, :], lng_ref, lnb_ref, ws_ref, bst_ref[...], CMLP_CHUNK, bd_bf)
        o_ref[rs, :] = jnp.concatenate([o_b, o_c], axis=1).astype(o_ref.dtype)

    @pl.when(t == pl.num_programs(1) - 1)
    def _():
        st_ref[...] = st_sc[...].T


def _mixer_prompt(g_in, br, c_in, prm, nbatch, ts=1024):
    t = g_in.shape[0]
    nt = t // nbatch // ts
    row = lambda n: pl.BlockSpec((ts, n), lambda b, i: (b * nt + i, 0))
    full = lambda a: pl.BlockSpec(a.shape, lambda b, i: (0,) * a.ndim)
    kern = functools.partial(_mixer_prompt_kernel, ts=ts, chunk=GLA_CHUNK_PROMPT)
    return pl.pallas_call(
        kern,
        grid=(nbatch, nt),
        in_specs=[row(4 * W_B), row(LANES), row(2 * W_C)] + [full(a) for a in prm],
        out_specs=[row(W_B + W_C), pl.BlockSpec((None, W_B, W_B), lambda b, i: (b, 0, 0))],
        out_shape=[jax.ShapeDtypeStruct((t, W_B + W_C), BF16),
                   jax.ShapeDtypeStruct((nbatch, W_B, W_B), F32)],
        scratch_shapes=[pltpu.VMEM((W_B, W_B), F32)],
        compiler_params=_cparams(("parallel", "arbitrary")),
        name="mixer_prompt",
    )(g_in, br, c_in, *prm)


def _mixer_sample_kernel(g_ref, br_ref, c_ref, s0_ref, wg_ref, bgate_ref, gng_ref, lng_ref, lnb_ref, ws_ref,
                         bst_ref, o_ref, vn_ref, st_ref, *, l_new):
    n = g_ref.shape[0]
    bd = _head_blockdiag(W_B)
    bd_bf = jnp.where(bd, 1.0, 0.0).astype(BF16)
    r, c = _iota2((n, n), 0), _iota2((n, n), 1)
    csum_sel = jnp.where(_chunk_causal(n, l_new), 1.0, 0.0).astype(BF16)
    last_sel = jnp.where(_idiv(r, l_new) == _idiv(c, l_new), 1.0, 0.0).astype(BF16)
    g = g_ref[...]
    gq = g[:, 0:W_B] * (DK_B ** -0.5)
    gk, gv, gg = g[:, W_B:2 * W_B], g[:, 2 * W_B:3 * W_B], g[:, 3 * W_B:4 * W_B]
    la = _gla_gate(br_ref[...], wg_ref, bgate_ref)
    bcum = _dot_sel(csum_sel, la)
    blast = _dot_sel(last_sel, la)
    q_in = gq * jnp.exp(bcum)
    k_in = gk * jnp.exp(-bcum)
    k_end = gk * jnp.exp(blast - bcum)
    o = _gla_intra(q_in, k_in, gv, l_new)
    zrows = jnp.zeros((LANES - n, W_B), F32)
    ke_t = jnp.concatenate([k_end, zrows], axis=0).T
    bl_t = jnp.concatenate([blast, zrows], axis=0).T
    v_pad = jnp.concatenate([gv, zrows], axis=0).astype(BF16)
    rows = _iota2((n, W_B), 0)
    cols = _iota2((W_B, LANES), 1)
    zblk = jnp.zeros((DK_B, DV_B), F32)
    for s in range(n // l_new):
        s0 = jnp.concatenate(
            [jnp.concatenate([s0_ref[s, h] if g == h else zblk for g in range(HB)], axis=1) for h in range(HB)],
            axis=0)
        qs = jnp.where(_idiv(rows, l_new) == s, q_in, 0.0).astype(BF16)
        o = o + jnp.dot(qs, s0.astype(BF16), preferred_element_type=F32)
        kes = jnp.where(_idiv(cols, l_new) == s, ke_t, 0.0).astype(BF16)
        upd = jnp.dot(kes, v_pad, preferred_element_type=F32)
        dl = jnp.exp(bl_t[:, s * l_new:s * l_new + 1])
        fin = s0 * dl + upd
        for h in range(HB):
            st_ref[s, h] = fin[h * DK_B:(h + 1) * DK_B, h * DV_B:(h + 1) * DV_B]
    o_b = _gla_finish(o, gg, gng_ref, bd_bf)
    o_c, vn = _chunk_mlp(c_ref[...], lng_ref, lnb_ref, ws_ref, bst_ref[...], l_new, bd_bf)
    o_ref[...] = jnp.concatenate([o_b, o_c], axis=1).astype(o_ref.dtype)
    vn_ref[...] = vn


def _mixer_sample(g_in, br, c_in, states, layer, prm, l_new, ts=64):
    t = g_in.shape[0]
    ns = ts // l_new
    nblk = t // ts
    row = lambda n: pl.BlockSpec((ts, n), lambda i: (i, 0))
    full = lambda a: pl.BlockSpec(a.shape, lambda i: (0,) * a.ndim)
    st = pl.BlockSpec((ns, HB, DK_B, DV_B), lambda i: (layer * nblk + i, 0, 0, 0))
    kern = functools.partial(_mixer_sample_kernel, l_new=l_new)
    return pl.pallas_call(
        kern,
        grid=(nblk,),
        in_specs=[row(4 * W_B), row(LANES), row(2 * W_C), st] + [full(a) for a in prm],
        out_specs=[row(W_B + W_C), row(W_C), st],
        out_shape=[jax.ShapeDtypeStruct((t, W_B + W_C), BF16),
                   jax.ShapeDtypeStruct((t, W_C), F32),
                   jax.ShapeDtypeStruct(states.shape, F32)],
        input_output_aliases={3: 2},
        compiler_params=_cparams(("parallel",)),
        name="mixer_sample",
    )(g_in, br, c_in, states, *prm)


def _outproj_kernel(oa_ref, obc_ref, x_ref, wo_ref, g_ref, b_ref, rw_ref, rb_ref, *rest, alpha):
    h_ref, hp_ref, lg_ref = rest[-3:]
    y = jnp.dot(oa_ref[...], wo_ref[0:W_A, :], preferred_element_type=F32)
    y = y + jnp.dot(obc_ref[...], wo_ref[W_A:, :], preferred_element_type=F32)
    h = _ln_rows(alpha * x_ref[...] + y, g_ref[...], b_ref[...])
    h_ref[...] = h
    hp_ref[...] = _pack_bf16_pairs(h)
    h_hi = h.astype(BF16)
    h_lo = (h - h_hi.astype(F32)).astype(BF16)
    lg = jnp.dot(h_hi, rw_ref[0], preferred_element_type=F32)
    lg = lg + jnp.dot(h_lo, rw_ref[0], preferred_element_type=F32)
    lg = lg + jnp.dot(h_hi, rw_ref[1], preferred_element_type=F32)
    lg_ref[...] = lg + rb_ref[...]


def _outproj(o_a, o_bc, x, wo, g, b, rw, rb, alpha, row0, rows_all, logits_all):
    t = x.shape[0]
    tm = min(512, t)
    assert row0 % tm == 0
    off = row0 // tm
    row = lambda n: pl.BlockSpec((tm, n), lambda i: (i, 0))
    row_at = lambda n: pl.BlockSpec((tm, n), lambda i: (i + off, 0))
    full = lambda a: pl.BlockSpec(a.shape, lambda i: (0,) * a.ndim)
    anywhere = pl.BlockSpec(memory_space=pl.ANY)
    return pl.pallas_call(
        functools.partial(_outproj_kernel, alpha=alpha),
        grid=(t // tm,),
        in_specs=[row(W_A), row(W_B + W_C), row(D_MODEL)] + [full(a) for a in (wo, g, b, rw, rb)]
                 + [anywhere, anywhere],
        out_specs=[row(D_MODEL), row_at(D_PACK), row_at(LANES)],
        out_shape=[jax.ShapeDtypeStruct((t, D_MODEL), F32), jax.ShapeDtypeStruct(rows_all.shape, F32),
                   jax.ShapeDtypeStruct(logits_all.shape, F32)],
        input_output_aliases={8: 1, 9: 2},
        compiler_params=_cparams(("parallel",)),
        name="outproj",
    )(o_a, o_bc, x, wo, g, b, rw, rb, rows_all, logits_all)


def _moe_kernel(be_ref, rv_ref, x_ref, w1_ref, b1_ref, w2_ref, b2_ref, y_ref, w1b_sc, w2b_sc):
    i = pl.program_id(0)

    @pl.when((i == 0) | (be_ref[i] != be_ref[jnp.maximum(i - 1, 0)]))
    def _():
        w1b_sc[...] = w1_ref[...].astype(BF16)
        w2b_sc[...] = w2_ref[...].astype(BF16)

    @pl.when(rv_ref[i] == 0)
    def _():
        y_ref[...] = jnp.zeros(y_ref.shape, F32)

    @pl.when(rv_ref[i] > 0)
    def _():
        x = _unpack_bf16_pairs(x_ref[...])
        xb = jnp.where(_iota2(x.shape, 0) < rv_ref[i], x, 0.0).astype(BF16)
        acc = None
        for c in range(D_FF // MOE_FF_CHUNK):
            gs = slice(c * MOE_FF_CHUNK, (c + 1) * MOE_FF_CHUNK)
            us = slice(D_FF + c * MOE_FF_CHUNK, D_FF + (c + 1) * MOE_FF_CHUNK)
            g = jnp.dot(xb, w1b_sc[:, gs], preferred_element_type=F32) + b1_ref[:, gs]
            u = jnp.dot(xb, w1b_sc[:, us], preferred_element_type=F32) + b1_ref[:, us]
            g = jnp.minimum(g, SWIGLU_LIMIT)
            u = jnp.clip(u, -SWIGLU_LIMIT, SWIGLU_LIMIT)
            act = (u + 1.0) * g * (1.0 / (1.0 + jnp.exp(-SWIGLU_ALPHA * g)))
            part = jnp.dot(act.astype(BF16), w2b_sc[gs, :], preferred_element_type=F32)
            acc = part if acc is None else acc + part
        y_ref[...] = _pack_bf16_pairs(acc + b2_ref[...])


def _moe_experts(x_pad, block_e, rows_valid, w1, b1, w2, b2, layer, bm):
    nb = x_pad.shape[0] // bm
    return pl.pallas_call(
        _moe_kernel,
        grid_spec=pltpu.PrefetchScalarGridSpec(
            num_scalar_prefetch=2,
            grid=(nb,),
            in_specs=[pl.BlockSpec((bm, D_PACK), lambda i, be, rv: (i, 0)),
                      pl.BlockSpec((None, None, D_MODEL, 2 * D_FF), lambda i, be, rv: (layer, be[i], 0, 0)),
                      pl.BlockSpec((None, None, 1, 2 * D_FF), lambda i, be, rv: (layer, be[i], 0, 0)),
                      pl.BlockSpec((None, None, D_FF, D_MODEL), lambda i, be, rv: (layer, be[i], 0, 0)),
                      pl.BlockSpec((None, None, 1, D_MODEL), lambda i, be, rv: (layer, be[i], 0, 0))],
            out_specs=pl.BlockSpec((bm, D_PACK), lambda i, be, rv: (i, 0)),
            scratch_shapes=[pltpu.VMEM((D_MODEL, 2 * D_FF), BF16), pltpu.VMEM((D_FF, D_MODEL), BF16)],
        ),
        out_shape=jax.ShapeDtypeStruct((nb * bm, D_PACK), F32),
        compiler_params=_cparams(("arbitrary",)),
        name="moe_experts",
    )(block_e, rows_valid, x_pad, w1, b1, w2, b2)


ROUTE_E, ROUTE_RANK, ROUTE_GATE = 0, TOP_K, 2 * TOP_K


def _router_kernel(lg_ref, route_ref, route_t_ref, cnt_ref, cnt_sc):
    @pl.when(pl.program_id(0) == 0)
    def _():
        cnt_sc[...] = jnp.zeros(cnt_sc.shape, F32)

    work = lg_ref[...]
    tm = work.shape[0]
    lane = _iota2(work.shape, 1)
    lane_f = lane.astype(F32)
    sels, vals, ids = [], [], []
    for _ in range(TOP_K):
        mx = jnp.max(work, axis=1, keepdims=True)
        idx = jnp.min(jnp.where(work == mx, lane_f, float(LANES)), axis=1, keepdims=True)
        sel = lane_f == idx
        sels.append(sel)
        vals.append(mx)
        ids.append(idx)
        work = jnp.where(sel, -jnp.inf, work)
    ex = [jnp.exp(v - vals[0]) for v in vals]
    den = ex[0]
    for x in ex[1:]:
        den = den + x
    picked = jnp.zeros(work.shape, F32)
    for sel in sels:
        picked = jnp.where(sel, 1.0, picked)
    r, c = _iota2((tm, tm), 0), _iota2((tm, tm), 1)
    before = jnp.dot(jnp.where(c < r, 1.0, 0.0).astype(BF16), picked.astype(BF16), preferred_element_type=F32)
    before = before + cnt_sc[...]
    out = jnp.zeros(work.shape, F32)
    for k in range(TOP_K):
        rank = jnp.sum(jnp.where(sels[k], before, 0.0), axis=1, keepdims=True)
        out = jnp.where(lane == ROUTE_E + k, ids[k], out)
        out = jnp.where(lane == ROUTE_RANK + k, rank, out)
        out = jnp.where(lane == ROUTE_GATE + k, ex[k] / den, out)
    route_ref[...] = out
    route_t_ref[...] = out.T[0:ROUTE_GATE, :]
    cnt_sc[...] = cnt_sc[...] + jnp.sum(picked, axis=0, keepdims=True)
    cnt_ref[...] = cnt_sc[...]


def _router(logits):
    t = logits.shape[0]
    tm = min(512, t)
    return pl.pallas_call(
        _router_kernel,
        grid=(t // tm,),
        in_specs=[pl.BlockSpec((tm, LANES), lambda i: (i, 0))],
        out_specs=[pl.BlockSpec((tm, LANES), lambda i: (i, 0)), pl.BlockSpec((ROUTE_GATE, tm), lambda i: (0, i)),
                   pl.BlockSpec((1, LANES), lambda i: (0, 0))],
        out_shape=[jax.ShapeDtypeStruct((t, LANES), F32), jax.ShapeDtypeStruct((ROUTE_GATE, t), F32),
                   jax.ShapeDtypeStruct((1, LANES), F32)],
        scratch_shapes=[pltpu.VMEM((1, LANES), F32)],
        compiler_params=_cparams(("arbitrary",)),
        name="router",
    )(logits)


def _route(logits, bm):
    t = logits.shape[0]
    m = t * TOP_K
    route, route_t, cnt = _router(logits)
    e_t = route_t[ROUTE_E:ROUTE_E + TOP_K].astype(jnp.int32)
    rank_t = route_t[ROUTE_RANK:ROUTE_RANK + TOP_K].astype(jnp.int32)
    counts = cnt[0, :N_EXPERTS].astype(jnp.int32)
    padded = ((counts + bm - 1) // bm) * bm
    pad_end = jnp.cumsum(padded)
    pad_start = pad_end - padded
    experts = jnp.arange(N_EXPERTS)[:, None, None]
    dest = jnp.sum(jnp.where(e_t[None] == experts, pad_start[:, None, None], 0), axis=0) + rank_t
    nb = -(-m // bm) + N_EXPERTS
    block_e = jnp.sum((pad_end[None, :] <= (jnp.arange(nb) * bm)[:, None]).astype(jnp.int32), axis=1)
    block_e = jnp.minimum(block_e, N_EXPERTS - 1)
    n_used = (pad_end[-1] // bm).astype(jnp.int32).reshape(1)
    block_e = jnp.where(jnp.arange(nb) < n_used[0], block_e, block_e[jnp.maximum(n_used[0] - 1, 0)])
    rows_valid = jnp.clip((pad_start + counts)[block_e] - jnp.arange(nb) * bm, 0, bm).astype(jnp.int32)
    return route, dest, block_e.astype(jnp.int32), rows_valid


def _ln2_kernel(h_ref, gate_ref, *rest, alpha):
    y_refs, (g_ref, b_ref, o_ref) = rest[:TOP_K], rest[TOP_K:]
    gate = gate_ref[...]
    x = alpha * h_ref[...]
    for k in range(TOP_K):
        x = x + gate[:, ROUTE_GATE + k:ROUTE_GATE + k + 1] * _unpack_bf16_pairs(y_refs[k][...])
    o_ref[...] = _ln_rows(x, g_ref[...], b_ref[...])


def _ln2(h, gate, yg, row0, g, b, alpha):
    t = h.shape[0]
    t_all = yg.shape[0] // TOP_K
    tm = min(512, t)
    assert row0 % tm == 0 and t_all % tm == 0
    off = row0 // tm
    nt_all = t_all // tm
    y_specs = [pl.BlockSpec((tm, D_PACK), functools.partial(lambda i, k: (k * nt_all + off + i, 0), k=k))
               for k in range(TOP_K)]
    return pl.pallas_call(
        functools.partial(_ln2_kernel, alpha=alpha),
        grid=(t // tm,),
        in_specs=[pl.BlockSpec((tm, D_MODEL), lambda i: (i, 0)),
                  pl.BlockSpec((tm, LANES), lambda i: (i + off, 0))] + y_specs +
                 [pl.BlockSpec((1, D_MODEL), lambda i: (0, 0)),
                  pl.BlockSpec((1, D_MODEL), lambda i: (0, 0))],
        out_specs=pl.BlockSpec((tm, D_MODEL), lambda i: (i, 0)),
        out_shape=jax.ShapeDtypeStruct((t, D_MODEL), F32),
        compiler_params=_cparams(("parallel",)),
        name="ln2",
    )(h, gate, *([yg] * TOP_K), g, b)


SC_CORES, SC_SUBCORES = 2, 16
SC_CHUNK = 64


def _sc_gather(table, idx):
    b, d = idx.shape[0], table.shape[1]
    workers = SC_CORES * SC_SUBCORES
    per_w = b // workers
    assert per_w * workers == b and per_w % SC_CHUNK == 0
    mesh = plsc.VectorSubcoreMesh(core_axis_name="c", subcore_axis_name="s")

    n_chunks = per_w // SC_CHUNK

    @functools.partial(
        pl.kernel, mesh=mesh, out_type=jax.ShapeDtypeStruct((b, d), table.dtype),
        scratch_types=[pltpu.VMEM((per_w,), jnp.int32), pltpu.VMEM((2, SC_CHUNK, d), table.dtype),
                       pltpu.SemaphoreType.DMA((2,))],
        name="sc_gather")
    def gather(table_hbm, idx_hbm, out_hbm, idx_v, rows_v, sems):
        wid = lax.axis_index("s") * SC_CORES + lax.axis_index("c")
        pltpu.sync_copy(idx_hbm.at[pl.ds(wid * per_w, per_w)], idx_v)

        def fetch(i, slot):
            return pltpu.make_async_copy(table_hbm.at[idx_v.at[pl.ds(i * SC_CHUNK, SC_CHUNK)]],
                                         rows_v.at[slot], sems.at[slot])

        fetch(0, 0).start()

        @pl.loop(0, n_chunks)
        def _(i):
            slot = lax.rem(i, 2)

            @pl.when(i + 1 < n_chunks)
            def _():
                fetch(i + 1, 1 - slot).start()

            fetch(i, slot).wait()
            pltpu.sync_copy(rows_v.at[slot], out_hbm.at[pl.ds(wid * per_w + i * SC_CHUNK, SC_CHUNK)])

    return gather(table, idx)


SC_SCATTER_CHUNK = 48


def _sc_scatter_rows(rows, dest, n_out):
    t, d = rows.shape
    workers = SC_CORES * SC_SUBCORES
    per_w = t // workers
    assert per_w * workers == t and per_w % SC_SCATTER_CHUNK == 0
    mesh = plsc.VectorSubcoreMesh(core_axis_name="c", subcore_axis_name="s")

    @functools.partial(
        pl.kernel, mesh=mesh, out_type=jax.ShapeDtypeStruct((n_out, d), rows.dtype),
        scratch_types=[pltpu.VMEM((TOP_K, SC_SCATTER_CHUNK), jnp.int32),
                       pltpu.VMEM((SC_SCATTER_CHUNK, d), rows.dtype)],
        name="sc_scatter")
    def scatter(rows_hbm, dest_hbm, out_hbm, idx_v, rows_v):
        wid = lax.axis_index("s") * SC_CORES + lax.axis_index("c")

        @pl.loop(0, per_w // SC_SCATTER_CHUNK)
        def _(i):
            base = wid * per_w + i * SC_SCATTER_CHUNK
            pltpu.sync_copy(rows_hbm.at[pl.ds(base, SC_SCATTER_CHUNK)], rows_v)
            for k in range(TOP_K):
                pltpu.sync_copy(dest_hbm.at[pl.ds(k * t + base, SC_SCATTER_CHUNK)], idx_v.at[k])
            for k in range(TOP_K):
                pltpu.sync_copy(rows_v, out_hbm.at[idx_v.at[k]])

    return scatter(rows, dest.reshape(-1))


def _rope_tables(pos):
    half = ROT_DIM // 2
    inv_freq = ROPE_THETA ** (-jnp.arange(0, ROT_DIM, 2, dtype=F32) / ROT_DIM)
    ang = pos.astype(F32)[:, None] * inv_freq[None, :]
    cos, sin = jnp.cos(ang), jnp.sin(ang)
    m = np.arange(LANES) % DQK_A
    idx = m % half
    cos_l = jnp.where(m < ROT_DIM, cos[:, idx], 1.0)
    sa = jnp.where(m < half, -sin[:, idx], 0.0)
    sb = jnp.where((m >= half) & (m < ROT_DIM), sin[:, idx], 0.0)
    return cos_l, sa, sb


def _prep_w_in(w):
    r0 = COL_C
    r1 = r0 + GATE_RANK
    pad = jnp.zeros((w.shape[0], LANES - GATE_RANK), w.dtype)
    return jnp.concatenate([w[:, :r0], w[:, r1:], w[:, r0:r1], pad], axis=1).astype(BF16)


def _tile_lanes(v, reps):
    return jnp.tile(v.reshape(1, -1), (1, reps)).astype(F32)


def _blockdiag_states(s):
    n = s.shape[0]
    eye = jnp.eye(HB, dtype=s.dtype)
    return jnp.einsum('nhde,hg->nhdge', s, eye).reshape(n, HB * DK_B, HB * DV_B)


def _diag_states(sbd):
    n = sbd.shape[0]
    s = sbd.reshape(n, HB, DK_B, HB, DV_B)
    return jnp.stack([s[:, h, :, h, :] for h in range(HB)], axis=1)


def kernel(x_prompt, x_sample, cache_k, cache_v, page_table, state_gla, w_in, lam_q1, lam_k1, lam_q2, lam_k2, attn_norm_g, gla_w_gate, gla_b_gate, gla_norm_g, cmlp_ln_g, cmlp_ln_b, cmlp_ws, cmlp_bs, w_o, ln1_g, ln1_b, router_w, router_b, exp_w1, exp_b1, exp_w2, exp_b2, ln2_g, ln2_b):
    depth = w_in.shape[0]
    bp, s_len, _ = x_prompt.shape
    db, l_new, _ = x_sample.shape
    n_phys, page = cache_k.shape[1], cache_k.shape[2]
    past_len = page_table.shape[1] * page
    alpha = (2 * depth) ** 0.25
    tp, ts = bp * s_len, db * l_new
    bm = MOE_BLOCK

    tabs_p = _rope_tables(jnp.arange(s_len))
    tabs_s = _rope_tables(past_len + (jnp.arange(ts) % l_new))
    page_table = page_table.astype(jnp.int32)

    hp = x_prompt.reshape(tp, D_MODEL)
    hs = x_sample.reshape(ts, D_MODEL)
    outs = {k: [] for k in ("gp", "cs")}
    gs_all = state_gla.astype(F32).reshape(depth * db, HB, DK_B, DV_B)
    kp_all, vp_all = (jnp.zeros((depth * tp, HA, DV_A), F32) for _ in range(2))
    ks_all, vs_all = (jnp.zeros((depth * ts, HA, DV_A), F32) for _ in range(2))
    for l in range(depth):
        lam_init = 0.8 - 0.6 * math.exp(-0.3 * l)
        w = _prep_w_in(w_in[l])
        lamv = jnp.pad(jnp.stack([lam_q1[l], lam_k1[l], lam_q2[l], lam_k2[l]]).astype(F32),
                       ((0, 0), (0, LANES - DQK_A)))
        g_attn = attn_norm_g[l].reshape(1, DV_A).astype(F32)
        wg = jnp.pad(gla_w_gate[l], ((0, LANES - GATE_RANK), (0, 0))).astype(BF16)
        wo = w_o[l].astype(BF16)
        rw = jnp.pad(router_w[l].astype(F32), ((0, 0), (0, LANES - N_EXPERTS)))
        rw_hi = rw.astype(BF16)
        rw = jnp.stack([rw_hi, (rw - rw_hi.astype(F32)).astype(BF16)])
        rb = jnp.pad(router_b[l].astype(F32), (0, LANES - N_EXPERTS), constant_values=NEG_INF).reshape(1, LANES)
        ln1 = (ln1_g[l].reshape(1, D_MODEL), ln1_b[l].reshape(1, D_MODEL))

        def mixer_params(lc, n_rows):
            reps = n_rows // lc
            ws = jnp.tile(cmlp_ws[l][:, :lc, :lc], (1, reps, reps))
            bst = jnp.tile(jnp.repeat(cmlp_bs[l][:, :lc].T, DC, axis=1), (reps, 1))
            return (wg, gla_b_gate[l].reshape(1, W_B), _tile_lanes(gla_norm_g[l], HB),
                    _tile_lanes(cmlp_ln_g[l], HC), _tile_lanes(cmlp_ln_b[l], HC), ws, bst)

        q, kp_all, vp_all, kb, vb, g_in, c_in, br = _inproj(hp, w, tabs_p, l, kp_all, vp_all)
        o_a = _attn_prompt(q, kb, vb, lamv, g_attn, bp, lam_init)
        o_bc, st_p = _mixer_prompt(g_in, br, c_in, mixer_params(CMLP_CHUNK, CMLP_CHUNK), bp)
        rows_k, logits = jnp.zeros((tp + ts, D_PACK), F32), jnp.zeros((tp + ts, LANES), F32)
        hp1, rows_k, logits = _outproj(o_a, o_bc, hp, wo, *ln1, rw, rb, alpha, 0, rows_k, logits)
        outs["gp"].append(_diag_states(st_p))

        q, ks_all, vs_all, kb, vb, g_in, c_in, br = _inproj(hs, w, tabs_s, l, ks_all, vs_all)
        o_a = _attn_sample(q, kb, vb, cache_k, cache_v, l, page_table, lamv, g_attn, l_new, lam_init)
        rows_s = min(64, ts)
        lc = min(l_new, CMLP_CHUNK)
        o_bc, vn, gs_all = _mixer_sample(g_in, br, c_in, gs_all, l, mixer_params(lc, rows_s), l_new, rows_s)
        hs1, rows_k, logits = _outproj(o_a, o_bc, hs, wo, *ln1, rw, rb, alpha, tp, rows_k, logits)
        outs["cs"].append(vn.reshape(db, l_new, W_C))

        gate, dest, block_e, rows_valid = _route(logits, bm)
        x_pad = _sc_scatter_rows(rows_k, dest, block_e.shape[0] * bm)
        y_pad = _moe_experts(x_pad, block_e, rows_valid, exp_w1, exp_b1.reshape(depth, N_EXPERTS, 1, -1),
                             exp_w2, exp_b2.reshape(depth, N_EXPERTS, 1, -1), l, bm)
        yg = _sc_gather(y_pad, dest.reshape(-1))
        hp = _ln2(hp1, gate, yg, 0, ln2_g[l].reshape(1, -1), ln2_b[l].reshape(1, -1), alpha)
        hs = _ln2(hs1, gate, yg, tp, ln2_g[l].reshape(1, -1), ln2_b[l].reshape(1, -1), alpha)

    return (hp.reshape(bp, s_len, D_MODEL), hs.reshape(db, l_new, D_MODEL),
            kp_all.reshape(depth, bp, s_len, HA, DV_A), vp_all.reshape(depth, bp, s_len, HA, DV_A),
            jnp.stack(outs["gp"]),
            ks_all.reshape(depth, db, l_new, HA, DV_A), vs_all.reshape(depth, db, l_new, HA, DV_A),
            gs_all.reshape(depth, db, HB, DK_B, DV_B), jnp.stack(outs["cs"]))
```

```python
import functools
import math

import numpy as np
import jax
import jax.numpy as jnp
from jax import lax
from jax.experimental import pallas as pl
from jax.experimental.pallas import tpu as pltpu
from jax.experimental.pallas import tpu_sc as plsc

F32, BF16 = jnp.float32, jnp.bfloat16
LANES = 128
VMEM_LIMIT = 48 * 1024 * 1024

D_MODEL = 1024
HA, DQK_A, DV_A = 4, 64, 128
ROT_DIM = DQK_A // 4
ROPE_THETA = 500000.0
HB, DK_B, DV_B = 4, 64, 64
GATE_RANK = 16
GATE_NORM = 16.0
HC, DC = 4, 64
CMLP_CHUNK = 128
N_EXPERTS = 32
TOP_K = 4
D_FF = D_MODEL
SWIGLU_LIMIT = 7.0
SWIGLU_ALPHA = 1.702
LN_EPS = 1e-5
RMS_EPS = 1e-6
NEG_INF = -1e30
LOG2E = math.log2(math.e)

W_A = HA * 2 * DQK_A
W_B = HB * DK_B
W_C = HC * DC
COL_G = 3 * W_A
COL_C = COL_G + 4 * W_B
COL_R = COL_C + 2 * W_C
COL_END = COL_R + LANES
GLA_CHUNK_PROMPT = 32
ATTN_BQ, ATTN_BK = 512, 512
ATTN_ROWS = 2 * ATTN_BQ
D_PACK = D_MODEL // 2
MOE_BLOCK = 512
MOE_FF_CHUNK = 512


def _cparams(sem):
    return pltpu.CompilerParams(dimension_semantics=sem, vmem_limit_bytes=VMEM_LIMIT)


def _split3(x):
    hi = x.astype(BF16)
    r = x - hi.astype(F32)
    mid = r.astype(BF16)
    lo = (r - mid.astype(F32)).astype(BF16)
    return hi, mid, lo


def _dot_sel(sel_bf16, x):
    acc = None
    for p in _split3(x):
        d = jnp.dot(sel_bf16, p, preferred_element_type=F32)
        acc = d if acc is None else acc + d
    return acc


def _seg_sum(x, bd_bf16):
    acc = None
    for p in _split3(x):
        d = jnp.dot(p, bd_bf16, preferred_element_type=F32)
        acc = d if acc is None else acc + d
    return acc


def _pack_bf16_pairs(x):
    u = lax.bitcast_convert_type(x, jnp.uint32)
    r = u + (jnp.uint32(0x7FFF) + ((u >> 16) & jnp.uint32(1)))
    w = x.shape[1] // 2
    word = (r[:, :w] & jnp.uint32(0xFFFF0000)) | (r[:, w:] >> 16)
    return lax.bitcast_convert_type(word, F32)


def _unpack_bf16_pairs(words):
    u = lax.bitcast_convert_type(words, jnp.uint32)
    hi = lax.bitcast_convert_type(u & jnp.uint32(0xFFFF0000), F32)
    lo = lax.bitcast_convert_type(u << 16, F32)
    return jnp.concatenate([hi, lo], axis=1)


def _iota2(shape, dim):
    return lax.broadcasted_iota(jnp.int32, shape, dim)


def _idiv(x, n):
    shift = n.bit_length() - 1
    assert n == 1 << shift
    return x >> shift


def _head_blockdiag(n):
    r, c = _iota2((n, n), 0), _iota2((n, n), 1)
    return _idiv(r, DK_B) == _idiv(c, DK_B)


def _chunk_causal(n, chunk):
    r, c = _iota2((n, n), 0), _iota2((n, n), 1)
    return (_idiv(r, chunk) == _idiv(c, chunk)) & (c <= r)


def _ln_rows(x, g, b):
    mu = jnp.mean(x, axis=-1, keepdims=True)
    xc = x - mu
    var = jnp.mean(xc * xc, axis=-1, keepdims=True)
    return xc * lax.rsqrt(var + LN_EPS) * g + b


def _inproj_kernel(x_ref, w_ref, cos_ref, sa_ref, sb_ref, k_all_ref, v_all_ref,
                   q_ref, k_ref, v_ref, kb_ref, vb_ref, g_ref, c_ref, br_ref):
    del k_all_ref, v_all_ref
    xb = x_ref[...].astype(BF16)

    def proj(a, b):
        return jnp.dot(xb, w_ref[:, a:b], preferred_element_type=F32)

    cos, sa, sb = cos_ref[...], sa_ref[...], sb_ref[...]

    def rope(z):
        outs = []
        for i in range(z.shape[1] // LANES):
            zi = z[:, i * LANES:(i + 1) * LANES]
            outs.append(zi * cos + pltpu.roll(zi, LANES - ROT_DIM // 2, 1) * sa
                        + pltpu.roll(zi, ROT_DIM // 2, 1) * sb)
        return jnp.concatenate(outs, axis=1)

    q_ref[...] = (rope(proj(0, W_A)) * (DQK_A ** -0.5 * LOG2E)).astype(BF16)
    k = rope(proj(W_A, 2 * W_A))
    kb_ref[...] = k.astype(BF16)
    v = proj(2 * W_A, 3 * W_A)
    vb_ref[...] = v.astype(BF16)
    for h in range(HA):
        k_ref[:, h, :] = k[:, h * LANES:(h + 1) * LANES]
        v_ref[:, h, :] = v[:, h * LANES:(h + 1) * LANES]
    g_ref[...] = proj(COL_G, COL_C)
    c_ref[...] = proj(COL_C, COL_R)
    br_ref[...] = proj(COL_R, COL_END)


def _inproj(x, w, tabs, layer, k_all, v_all):
    t = x.shape[0]
    tm = min(512, t)
    nt = t // tm
    cos, sa, sb = tabs
    ntab = cos.shape[0] // tm
    row = lambda n: pl.BlockSpec((tm, n), lambda i: (i, 0))
    tab = pl.BlockSpec((tm, LANES), lambda i: (i % ntab, 0))
    heads = pl.BlockSpec((tm, HA, DV_A), lambda i: (layer * nt + i, 0, 0))
    anywhere = pl.BlockSpec(memory_space=pl.ANY)
    shapes = [((W_A,), BF16), None, None, ((W_A,), BF16), ((W_A,), BF16),
              ((4 * W_B,), F32), ((2 * W_C,), F32), ((LANES,), F32)]
    return pl.pallas_call(
        _inproj_kernel,
        grid=(nt,),
        in_specs=[row(D_MODEL), pl.BlockSpec((D_MODEL, COL_END), lambda i: (0, 0)), tab, tab, tab,
                  anywhere, anywhere],
        out_specs=[heads if s is None else row(s[0][0]) for s in shapes],
        out_shape=[jax.ShapeDtypeStruct(k_all.shape, F32) if s is None else jax.ShapeDtypeStruct((t,) + s[0], s[1])
                   for s in shapes],
        input_output_aliases={5: 1, 6: 2},
        compiler_params=_cparams(("parallel",)),
        name="inproj",
    )(x, w, cos, sa, sb, k_all, v_all)


def _diff_lambda(lamv, lam_init):
    a = jnp.sum(lamv[0:1] * lamv[1:2], axis=1, keepdims=True)
    b = jnp.sum(lamv[2:3] * lamv[3:4], axis=1, keepdims=True)
    return jnp.exp(a) - jnp.exp(b) + lam_init


def _diff_finish(o1, o2, lam, g, lam_init):
    o = o1 - lam * o2
    ms = jnp.mean(o * o, axis=-1, keepdims=True)
    return o * lax.rsqrt(ms + RMS_EPS) * g * (1.0 - lam_init)


def _split_maps(q):
    lane = _iota2(q.shape, 1)
    zero = jnp.zeros_like(q)
    return jnp.concatenate([jnp.where(lane < DQK_A, q, zero), jnp.where(lane >= DQK_A, q, zero)], axis=0)


def _attn_prompt_kernel(q_ref, k_ref, v_ref, lamv_ref, g_ref, o_ref, m_sc, l_sc, acc_sc, *, bq, bk, lam_init):
    qi = pl.program_id(2)
    qq = _split_maps(q_ref[...])
    m_sc[...] = jnp.full(m_sc.shape, NEG_INF, F32)
    l_sc[...] = jnp.zeros(l_sc.shape, F32)
    acc_sc[...] = jnp.zeros(acc_sc.shape, F32)

    def step(j, masked):
        start = pl.multiple_of(j * bk, bk)
        k = k_ref[pl.ds(start, bk), :]
        v = v_ref[pl.ds(start, bk), :]
        tiles = range(bk // LANES)
        for rc in range(2 * bq // ATTN_ROWS):
            rs = slice(rc * ATTN_ROWS, (rc + 1) * ATTN_ROWS)
            s = lax.dot_general(qq[rs], k, (((1,), (1,)), ((), ())), preferred_element_type=F32)
            if masked:
                r = (_iota2(s.shape, 0) + rc * ATTN_ROWS) & (bq - 1)
                c = _iota2(s.shape, 1)
                s = jnp.where(c <= r, s, NEG_INF)
            m_prev = m_sc[rs, :]
            s_max = functools.reduce(jnp.maximum, [s[:, i * LANES:(i + 1) * LANES] for i in tiles])
            m_new = jnp.maximum(m_prev, jnp.max(s_max, axis=1, keepdims=True))
            alpha = jnp.exp2(m_prev - m_new)
            p = jnp.exp2((s - jnp.tile(m_new, (1, bk // LANES))).astype(BF16))
            p_sum = functools.reduce(jnp.add, [p[:, i * LANES:(i + 1) * LANES].astype(F32) for i in tiles])
            l_sc[rs, :] = alpha * l_sc[rs, :] + jnp.sum(p_sum, axis=1, keepdims=True)
            acc_sc[rs, :] = alpha * acc_sc[rs, :] + jnp.dot(p, v, preferred_element_type=F32)
            m_sc[rs, :] = m_new

    def body(j, carry):
        step(j, False)
        return carry

    lax.fori_loop(0, qi, body, 0)
    step(qi, True)

    o = acc_sc[...] / l_sc[...]
    lam = _diff_lambda(lamv_ref[...], lam_init)
    o_ref[...] = _diff_finish(o[:bq], o[bq:], lam, g_ref[...], lam_init).astype(o_ref.dtype)


def _attn_prompt(q, kb, vb, lamv, g, nbatch, lam_init):
    t = q.shape[0]
    s = t // nbatch
    bq, bk = min(ATTN_BQ, s), min(ATTN_BK, s)
    assert bq == bk
    blk = bq
    nq = s // blk
    kern = functools.partial(_attn_prompt_kernel, bq=bq, bk=bk, lam_init=lam_init)
    return pl.pallas_call(
        kern,
        grid=(nbatch, HA, nq),
        in_specs=[pl.BlockSpec((blk, LANES), lambda b, h, i: (b * nq + i, h)),
                  pl.BlockSpec((s, LANES), lambda b, h, i: (b, h)),
                  pl.BlockSpec((s, LANES), lambda b, h, i: (b, h)),
                  pl.BlockSpec((4, LANES), lambda b, h, i: (0, 0)),
                  pl.BlockSpec((1, LANES), lambda b, h, i: (0, 0))],
        out_specs=pl.BlockSpec((blk, LANES), lambda b, h, i: (b * nq + i, h)),
        out_shape=jax.ShapeDtypeStruct((t, W_A), BF16),
        scratch_shapes=[pltpu.VMEM((2 * blk, LANES), F32)] * 3,
        compiler_params=_cparams(("parallel", "parallel", "arbitrary")),
        name="attn_prompt",
    )(q, kb, vb, lamv, g)


def _attn_sample_kernel(pt_ref, q_ref, kn_ref, vn_ref, lamv_ref, g_ref, *rest, n_pages, l_new, lam_init):
    del pt_ref
    kp = rest[:n_pages]
    vp = rest[n_pages:2 * n_pages]
    o_ref = rest[2 * n_pages]
    rows_pg = kp[0].shape[0]
    rpad = q_ref.shape[0]
    nq = 2 * rpad
    lam = _diff_lambda(lamv_ref[...], lam_init)
    q = q_ref[...]
    qq = jnp.concatenate([_split_maps(q[:, h * LANES:(h + 1) * LANES]) for h in range(HA)],
                         axis=0).astype(BF16)
    r = _iota2((HA * nq, rows_pg), 0)
    c = _iota2((HA * nq, rows_pg), 1)
    head_ok = (c & (HA - 1)) == _idiv(r, nq)
    new_ok = head_ok & (_idiv(c, HA) <= (r & (rpad - 1))) & (c < l_new * HA)
    nt = (((1,), (1,)), ((), ()))
    zpad = jnp.zeros((rows_pg - kn_ref.shape[0], LANES), F32)
    k_new = jnp.concatenate([kn_ref[...], zpad], axis=0).astype(BF16)
    v_new = jnp.concatenate([vn_ref[...], zpad], axis=0).astype(BF16)
    s_new = jnp.where(new_ok, lax.dot_general(qq, k_new, nt, preferred_element_type=F32), NEG_INF)
    s_past = [jnp.where(head_ok, lax.dot_general(qq, kp[j][...].astype(BF16), nt, preferred_element_type=F32),
                        NEG_INF) for j in range(n_pages)]
    m = jnp.max(s_new, axis=1, keepdims=True)
    for sj in s_past:
        m = jnp.maximum(m, jnp.max(sj, axis=1, keepdims=True))
    p_new = jnp.exp2(s_new - m)
    l = jnp.sum(p_new, axis=1, keepdims=True)
    acc = jnp.dot(p_new.astype(BF16), v_new, preferred_element_type=F32)
    for j in range(n_pages):
        pj = jnp.exp2(s_past[j] - m)
        l = l + jnp.sum(pj, axis=1, keepdims=True)
        acc = acc + jnp.dot(pj.astype(BF16), vp[j][...].astype(BF16), preferred_element_type=F32)
    o = acc / l
    outs = [_diff_finish(o[h * nq:h * nq + rpad], o[h * nq + rpad:(h + 1) * nq], lam, g_ref[...], lam_init)
            for h in range(HA)]
    o_ref[...] = jnp.concatenate(outs, axis=1)


def _attn_sample(q, k, v, cache_k, cache_v, layer, page_table, lamv, g, l_new, lam_init):
    t = q.shape[0]
    db = t // l_new
    n_pages = page_table.shape[1]
    page = cache_k.shape[2]
    rpad = 8

    def pad_rows(a):
        a = a.reshape(db, l_new, W_A).astype(F32)
        return jnp.concatenate([a, jnp.zeros((db, rpad - l_new, W_A), F32)], axis=1)

    def new_rows(a):
        return a.reshape(db, l_new * HA, DV_A).astype(F32)

    depth, n_phys = cache_k.shape[:2]
    cache_k = cache_k.reshape(depth, n_phys, page * HA, DV_A)
    cache_v = cache_v.reshape(depth, n_phys, page * HA, DV_A)
    q_spec = pl.BlockSpec((None, rpad, W_A), lambda b, pt: (b, 0, 0))
    new_spec = pl.BlockSpec((None, l_new * HA, DV_A), lambda b, pt: (b, 0, 0))
    page_specs = [pl.BlockSpec((None, None, page * HA, DV_A),
                               functools.partial(lambda b, pt, j: (layer, pt[b, j], 0, 0), j=j))
                  for j in range(n_pages)]
    kern = functools.partial(_attn_sample_kernel, n_pages=n_pages, l_new=l_new, lam_init=lam_init)
    out = pl.pallas_call(
        kern,
        grid_spec=pltpu.PrefetchScalarGridSpec(
            num_scalar_prefetch=1,
            grid=(db,),
            in_specs=[q_spec, new_spec, new_spec,
                      pl.BlockSpec((4, LANES), lambda b, pt: (0, 0)),
                      pl.BlockSpec((1, LANES), lambda b, pt: (0, 0))] + page_specs + page_specs,
            out_specs=q_spec,
        ),
        out_shape=jax.ShapeDtypeStruct((db, rpad, W_A), F32),
        compiler_params=_cparams(("arbitrary",)),
        name="attn_sample",
    )(page_table, pad_rows(q), new_rows(k), new_rows(v), lamv, g,
      *([cache_k] * n_pages), *([cache_v] * n_pages))
    return out[:, :l_new].reshape(t, W_A).astype(BF16)


def _log_sigmoid(x):
    return jnp.minimum(x, 0.0) - jnp.log(1.0 + jnp.exp(-jnp.abs(x)))


def _gla_gate(br, wg_ref, bgate_ref):
    x = jnp.dot(br.astype(BF16), wg_ref[...], preferred_element_type=F32) + bgate_ref[...]
    return _log_sigmoid(x) / GATE_NORM


def _gla_intra(q_att, k_in, v, chunk):
    keep = _chunk_causal(q_att.shape[0], chunk)
    lane = _idiv(_iota2(q_att.shape, 1), DK_B)
    kb = k_in.astype(BF16)
    atts, vs = [], []
    for h in range(HB):
        qh = jnp.where(lane == h, q_att, 0.0).astype(BF16)
        a = lax.dot_general(qh, kb, (((1,), (1,)), ((), ())), preferred_element_type=F32)
        atts.append(jnp.where(keep, a, 0.0).astype(BF16))
        vs.append(jnp.where(lane == h, v, 0.0).astype(BF16))
    return jnp.dot(jnp.concatenate(atts, axis=1), jnp.concatenate(vs, axis=0), preferred_element_type=F32)


def _gla_finish(o, gate_in, gng_ref, bd):
    ms = _seg_sum(o * o, bd) * (1.0 / DV_B)
    o = o * lax.rsqrt(ms + RMS_EPS) * gng_ref[...]
    return o * (gate_in * (1.0 / (1.0 + jnp.exp(-gate_in))))


def _chunk_mlp(c_in, lng_ref, lnb_ref, ws_ref, bst, chunk, bd):
    n = c_in.shape[0]
    cu, cv = c_in[:, :W_C], c_in[:, W_C:]
    mu = _seg_sum(cv, bd) * (1.0 / DC)
    xc = cv - mu
    var = _seg_sum(xc * xc, bd) * (1.0 / DC)
    vn = xc * lax.rsqrt(var + LN_EPS) * lng_ref[...] + lnb_ref[...]
    keep = _chunk_causal(n, chunk)
    lane = _idiv(_iota2(vn.shape, 1), DC)
    ws, vs = [], []
    for g in range(HC):
        ws.append(jnp.where(keep, ws_ref[g], 0.0).astype(BF16))
        vs.append(jnp.where(lane == g, vn, 0.0).astype(BF16))
    mixed = jnp.dot(jnp.concatenate(ws, axis=1), jnp.concatenate(vs, axis=0), preferred_element_type=F32) + bst
    return cu * mixed, vn


def _mixer_prompt_kernel(g_ref, br_ref, c_ref, wg_ref, bgate_ref, gng_ref, lng_ref, lnb_ref, ws_ref, bst_ref,
                         o_ref, st_ref, st_sc, *, ts, chunk):
    t = pl.program_id(1)

    @pl.when(t == 0)
    def _():
        st_sc[...] = jnp.zeros(st_sc.shape, F32)

    grp = CMLP_CHUNK
    bd = _head_blockdiag(W_B)
    bd_bf = jnp.where(bd, 1.0, 0.0).astype(BF16)
    csum_sel = jnp.where(_chunk_causal(grp, chunk), 1.0, 0.0).astype(BF16)
    rows = _iota2((grp, W_B), 0)
    half = chunk // 2
    for gi in range(ts // grp):
        rs = slice(gi * grp, (gi + 1) * grp)
        g = g_ref[rs, :]
        gq = g[:, 0:W_B] * (DK_B ** -0.5)
        gk, gv, gg = g[:, W_B:2 * W_B], g[:, 2 * W_B:3 * W_B], g[:, 3 * W_B:4 * W_B]
        la = _gla_gate(br_ref[rs, :], wg_ref, bgate_ref)
        bcum = _dot_sel(csum_sel, la)
        mids, lasts = [], []
        for ci in range(grp // chunk):
            mids.append(jnp.broadcast_to(bcum[ci * chunk + half - 1:ci * chunk + half, :], (chunk, W_B)))
            lasts.append(jnp.broadcast_to(bcum[(ci + 1) * chunk - 1:(ci + 1) * chunk, :], (chunk, W_B)))
        bmid = jnp.concatenate(mids, axis=0)
        blast = jnp.concatenate(lasts, axis=0)
        q_att = gq * jnp.exp(bcum - bmid)
        k_in = gk * jnp.exp(bmid - bcum)
        k_end = gk * jnp.exp(blast - bcum)
        q_dec = (gq * jnp.exp(bcum)).astype(BF16)
        o = _gla_intra(q_att, k_in, gv, chunk)
        v_t = gv.T.astype(BF16)
        o_inter = []
        for ci in range(grp // chunk):
            cs = slice(ci * chunk, (ci + 1) * chunk)
            st = st_sc[...]
            o_inter.append(lax.dot_general(q_dec[cs], st.astype(BF16), (((1,), (1,)), ((), ())),
                                           preferred_element_type=F32))
            kem = jnp.where(_idiv(rows, chunk) == ci, k_end, 0.0).astype(BF16)
            upd = jnp.dot(v_t, kem, preferred_element_type=F32)
            dl = jnp.exp(blast[ci * chunk:ci * chunk + 1, :])
            st_sc[...] = st * dl + jnp.where(bd, upd, 0.0)
        o = o + jnp.concatenate(o_inter, axis=0)
        o_b = _gla_finish(o, gg, gng_ref, bd_bf)
        o_c, _ = _chunk_mlp(c_ref[rs, :], lng_ref, lnb_ref, ws_ref, bst_ref[...], CMLP_CHUNK, bd_bf)
        o_ref[rs, :] = jnp.concatenate([o_b, o_c], axis=1).astype(o_ref.dtype)

    @pl.when(t == pl.num_programs(1) - 1)
    def _():
        st_ref[...] = st_sc[...].T


def _mixer_prompt(g_in, br, c_in, prm, nbatch, ts=1024):
    t = g_in.shape[0]
    nt = t // nbatch // ts
    row = lambda n: pl.BlockSpec((ts, n), lambda b, i: (b * nt + i, 0))
    full = lambda a: pl.BlockSpec(a.shape, lambda b, i: (0,) * a.ndim)
    kern = functools.partial(_mixer_prompt_kernel, ts=ts, chunk=GLA_CHUNK_PROMPT)
    return pl.pallas_call(
        kern,
        grid=(nbatch, nt),
        in_specs=[row(4 * W_B), row(LANES), row(2 * W_C)] + [full(a) for a in prm],
        out_specs=[row(W_B + W_C), pl.BlockSpec((None, W_B, W_B), lambda b, i: (b, 0, 0))],
        out_shape=[jax.ShapeDtypeStruct((t, W_B + W_C), BF16),
                   jax.ShapeDtypeStruct((nbatch, W_B, W_B), F32)],
        scratch_shapes=[pltpu.VMEM((W_B, W_B), F32)],
        compiler_params=_cparams(("parallel", "arbitrary")),
        name="mixer_prompt",
    )(g_in, br, c_in, *prm)


def _mixer_sample_kernel(g_ref, br_ref, c_ref, s0_ref, wg_ref, bgate_ref, gng_ref, lng_ref, lnb_ref, ws_ref,
                         bst_ref, o_ref, vn_ref, st_ref, *, l_new):
    n = g_ref.shape[0]
    bd = _head_blockdiag(W_B)
    bd_bf = jnp.where(bd, 1.0, 0.0).astype(BF16)
    r, c = _iota2((n, n), 0), _iota2((n, n), 1)
    csum_sel = jnp.where(_chunk_causal(n, l_new), 1.0, 0.0).astype(BF16)
    last_sel = jnp.where(_idiv(r, l_new) == _idiv(c, l_new), 1.0, 0.0).astype(BF16)
    g = g_ref[...]
    gq = g[:, 0:W_B] * (DK_B ** -0.5)
    gk, gv, gg = g[:, W_B:2 * W_B], g[:, 2 * W_B:3 * W_B], g[:, 3 * W_B:4 * W_B]
    la = _gla_gate(br_ref[...], wg_ref, bgate_ref)
    bcum = _dot_sel(csum_sel, la)
    blast = _dot_sel(last_sel, la)
    q_in = gq * jnp.exp(bcum)
    k_in = gk * jnp.exp(-bcum)
    k_end = gk * jnp.exp(blast - bcum)
    o = _gla_intra(q_in, k_in, gv, l_new)
    zrows = jnp.zeros((LANES - n, W_B), F32)
    ke_t = jnp.concatenate([k_end, zrows], axis=0).T
    bl_t = jnp.concatenate([blast, zrows], axis=0).T
    v_pad = jnp.concatenate([gv, zrows], axis=0).astype(BF16)
    rows = _iota2((n, W_B), 0)
    cols = _iota2((W_B, LANES), 1)
    zblk = jnp.zeros((DK_B, DV_B), F32)
    for s in range(n // l_new):
        s0 = jnp.concatenate(
            [jnp.concatenate([s0_ref[s, h] if g == h else zblk for g in range(HB)], axis=1) for h in range(HB)],
            axis=0)
        qs = jnp.where(_idiv(rows, l_new) == s, q_in, 0.0).astype(BF16)
        o = o + jnp.dot(qs, s0.astype(BF16), preferred_element_type=F32)
        kes = jnp.where(_idiv(cols, l_new) == s, ke_t, 0.0).astype(BF16)
        upd = jnp.dot(kes, v_pad, preferred_element_type=F32)
        dl = jnp.exp(bl_t[:, s * l_new:s * l_new + 1])
        fin = s0 * dl + upd
        for h in range(HB):
            st_ref[s, h] = fin[h * DK_B:(h + 1) * DK_B, h * DV_B:(h + 1) * DV_B]
    o_b = _gla_finish(o, gg, gng_ref, bd_bf)
    o_c, vn = _chunk_mlp(c_ref[...], lng_ref, lnb_ref, ws_ref, bst_ref[...], l_new, bd_bf)
    o_ref[...] = jnp.concatenate([o_b, o_c], axis=1).astype(o_ref.dtype)
    vn_ref[...] = vn


def _mixer_sample(g_in, br, c_in, states, layer, prm, l_new, ts=64):
    t = g_in.shape[0]
    ns = ts // l_new
    nblk = t // ts
    row = lambda n: pl.BlockSpec((ts, n), lambda i: (i, 0))
    full = lambda a: pl.BlockSpec(a.shape, lambda i: (0,) * a.ndim)
    st = pl.BlockSpec((ns, HB, DK_B, DV_B), lambda i: (layer * nblk + i, 0, 0, 0))
    kern = functools.partial(_mixer_sample_kernel, l_new=l_new)
    return pl.pallas_call(
        kern,
        grid=(nblk,),
        in_specs=[row(4 * W_B), row(LANES), row(2 * W_C), st] + [full(a) for a in prm],
        out_specs=[row(W_B + W_C), row(W_C), st],
        out_shape=[jax.ShapeDtypeStruct((t, W_B + W_C), BF16),
                   jax.ShapeDtypeStruct((t, W_C), F32),
                   jax.ShapeDtypeStruct(states.shape, F32)],
        input_output_aliases={3: 2},
        compiler_params=_cparams(("parallel",)),
        name="mixer_sample",
    )(g_in, br, c_in, states, *prm)


def _outproj_kernel(oa_ref, obc_ref, x_ref, wo_ref, g_ref, b_ref, rw_ref, rb_ref, *rest, alpha):
    h_ref, hp_ref, lg_ref = rest[-3:]
    y = jnp.dot(oa_ref[...], wo_ref[0:W_A, :], preferred_element_type=F32)
    y = y + jnp.dot(obc_ref[...], wo_ref[W_A:, :], preferred_element_type=F32)
    h = _ln_rows(alpha * x_ref[...] + y, g_ref[...], b_ref[...])
    h_ref[...] = h
    hp_ref[...] = _pack_bf16_pairs(h)
    h_hi = h.astype(BF16)
    h_lo = (h - h_hi.astype(F32)).astype(BF16)
    lg = jnp.dot(h_hi, rw_ref[0], preferred_element_type=F32)
    lg = lg + jnp.dot(h_lo, rw_ref[0], preferred_element_type=F32)
    lg = lg + jnp.dot(h_hi, rw_ref[1], preferred_element_type=F32)
    lg_ref[...] = lg + rb_ref[...]


def _outproj(o_a, o_bc, x, wo, g, b, rw, rb, alpha, row0, rows_all, logits_all):
    t = x.shape[0]
    tm = min(512, t)
    assert row0 % tm == 0
    off = row0 // tm
    row = lambda n: pl.BlockSpec((tm, n), lambda i: (i, 0))
    row_at = lambda n: pl.BlockSpec((tm, n), lambda i: (i + off, 0))
    full = lambda a: pl.BlockSpec(a.shape, lambda i: (0,) * a.ndim)
    anywhere = pl.BlockSpec(memory_space=pl.ANY)
    return pl.pallas_call(
        functools.partial(_outproj_kernel, alpha=alpha),
        grid=(t // tm,),
        in_specs=[row(W_A), row(W_B + W_C), row(D_MODEL)] + [full(a) for a in (wo, g, b, rw, rb)]
                 + [anywhere, anywhere],
        out_specs=[row(D_MODEL), row_at(D_PACK), row_at(LANES)],
        out_shape=[jax.ShapeDtypeStruct((t, D_MODEL), F32), jax.ShapeDtypeStruct(rows_all.shape, F32),
                   jax.ShapeDtypeStruct(logits_all.shape, F32)],
        input_output_aliases={8: 1, 9: 2},
        compiler_params=_cparams(("parallel",)),
        name="outproj",
    )(o_a, o_bc, x, wo, g, b, rw, rb, rows_all, logits_all)


def _moe_kernel(be_ref, rv_ref, x_ref, w1_ref, b1_ref, w2_ref, b2_ref, y_ref, w1b_sc, w2b_sc):
    i = pl.program_id(0)

    @pl.when((i == 0) | (be_ref[i] != be_ref[jnp.maximum(i - 1, 0)]))
    def _():
        w1b_sc[...] = w1_ref[...].astype(BF16)
        w2b_sc[...] = w2_ref[...].astype(BF16)

    @pl.when(rv_ref[i] == 0)
    def _():
        y_ref[...] = jnp.zeros(y_ref.shape, F32)

    def expert_rows(n):
        x = _unpack_bf16_pairs(x_ref[0:n, :])
        xb = jnp.where(_iota2(x.shape, 0) < rv_ref[i], x, 0.0).astype(BF16)
        acc = None
        for c in range(D_FF // MOE_FF_CHUNK):
            gs = slice(c * MOE_FF_CHUNK, (c + 1) * MOE_FF_CHUNK)
            us = slice(D_FF + c * MOE_FF_CHUNK, D_FF + (c + 1) * MOE_FF_CHUNK)
            g = jnp.dot(xb, w1b_sc[:, gs], preferred_element_type=F32) + b1_ref[:, gs]
            u = jnp.dot(xb, w1b_sc[:, us], preferred_element_type=F32) + b1_ref[:, us]
            g = jnp.minimum(g, SWIGLU_LIMIT)
            u = jnp.clip(u, -SWIGLU_LIMIT, SWIGLU_LIMIT)
            act = (u + 1.0) * g * (1.0 / (1.0 + jnp.exp(-SWIGLU_ALPHA * g)))
            part = jnp.dot(act.astype(BF16), w2b_sc[gs, :], preferred_element_type=F32)
            acc = part if acc is None else acc + part
        y_ref[0:n, :] = _pack_bf16_pairs(acc + b2_ref[...])

    bm = x_ref.shape[0]
    half = bm // 2

    @pl.when(rv_ref[i] > half)
    def _():
        expert_rows(bm)

    @pl.when((rv_ref[i] > 0) & (rv_ref[i] <= half))
    def _():
        expert_rows(half)
        y_ref[half:, :] = jnp.zeros((bm - half, y_ref.shape[1]), F32)


def _moe_experts(x_pad, block_e, rows_valid, w1, b1, w2, b2, layer, bm):
    nb = x_pad.shape[0] // bm
    return pl.pallas_call(
        _moe_kernel,
        grid_spec=pltpu.PrefetchScalarGridSpec(
            num_scalar_prefetch=2,
            grid=(nb,),
            in_specs=[pl.BlockSpec((bm, D_PACK), lambda i, be, rv: (i, 0)),
                      pl.BlockSpec((None, None, D_MODEL, 2 * D_FF), lambda i, be, rv: (layer, be[i], 0, 0)),
                      pl.BlockSpec((None, None, 1, 2 * D_FF), lambda i, be, rv: (layer, be[i], 0, 0)),
                      pl.BlockSpec((None, None, D_FF, D_MODEL), lambda i, be, rv: (layer, be[i], 0, 0)),
                      pl.BlockSpec((None, None, 1, D_MODEL), lambda i, be, rv: (layer, be[i], 0, 0))],
            out_specs=pl.BlockSpec((bm, D_PACK), lambda i, be, rv: (i, 0)),
            scratch_shapes=[pltpu.VMEM((D_MODEL, 2 * D_FF), BF16), pltpu.VMEM((D_FF, D_MODEL), BF16)],
        ),
        out_shape=jax.ShapeDtypeStruct((nb * bm, D_PACK), F32),
        compiler_params=_cparams(("arbitrary",)),
        name="moe_experts",
    )(block_e, rows_valid, x_pad, w1, b1, w2, b2)


ROUTE_E, ROUTE_RANK, ROUTE_GATE = 0, TOP_K, 2 * TOP_K


def _router_kernel(lg_ref, route_ref, route_t_ref, cnt_ref, cnt_sc):
    @pl.when(pl.program_id(0) == 0)
    def _():
        cnt_sc[...] = jnp.zeros(cnt_sc.shape, F32)

    work = lg_ref[...]
    tm = work.shape[0]
    lane = _iota2(work.shape, 1)
    lane_f = lane.astype(F32)
    sels, vals, ids = [], [], []
    for _ in range(TOP_K):
        mx = jnp.max(work, axis=1, keepdims=True)
        idx = jnp.min(jnp.where(work == mx, lane_f, float(LANES)), axis=1, keepdims=True)
        sel = lane_f == idx
        sels.append(sel)
        vals.append(mx)
        ids.append(idx)
        work = jnp.where(sel, -jnp.inf, work)
    ex = [jnp.exp(v - vals[0]) for v in vals]
    den = ex[0]
    for x in ex[1:]:
        den = den + x
    picked = jnp.zeros(work.shape, F32)
    for sel in sels:
        picked = jnp.where(sel, 1.0, picked)
    r, c = _iota2((tm, tm), 0), _iota2((tm, tm), 1)
    before = jnp.dot(jnp.where(c < r, 1.0, 0.0).astype(BF16), picked.astype(BF16), preferred_element_type=F32)
    before = before + cnt_sc[...]
    out = jnp.zeros(work.shape, F32)
    for k in range(TOP_K):
        rank = jnp.sum(jnp.where(sels[k], before, 0.0), axis=1, keepdims=True)
        out = jnp.where(lane == ROUTE_E + k, ids[k], out)
        out = jnp.where(lane == ROUTE_RANK + k, rank, out)
        out = jnp.where(lane == ROUTE_GATE + k, ex[k] / den, out)
    route_ref[...] = out
    route_t_ref[...] = out.T[0:ROUTE_GATE, :]
    cnt_sc[...] = cnt_sc[...] + jnp.sum(picked, axis=0, keepdims=True)
    cnt_ref[...] = cnt_sc[...]


def _router(logits):
    t = logits.shape[0]
    tm = min(512, t)
    return pl.pallas_call(
        _router_kernel,
        grid=(t // tm,),
        in_specs=[pl.BlockSpec((tm, LANES), lambda i: (i, 0))],
        out_specs=[pl.BlockSpec((tm, LANES), lambda i: (i, 0)), pl.BlockSpec((ROUTE_GATE, tm), lambda i: (0, i)),
                   pl.BlockSpec((1, LANES), lambda i: (0, 0))],
        out_shape=[jax.ShapeDtypeStruct((t, LANES), F32), jax.ShapeDtypeStruct((ROUTE_GATE, t), F32),
                   jax.ShapeDtypeStruct((1, LANES), F32)],
        scratch_shapes=[pltpu.VMEM((1, LANES), F32)],
        compiler_params=_cparams(("arbitrary",)),
        name="router",
    )(logits)


def _route(logits, bm):
    t = logits.shape[0]
    m = t * TOP_K
    route, route_t, cnt = _router(logits)
    e_t = route_t[ROUTE_E:ROUTE_E + TOP_K].astype(jnp.int32)
    rank_t = route_t[ROUTE_RANK:ROUTE_RANK + TOP_K].astype(jnp.int32)
    counts = cnt[0, :N_EXPERTS].astype(jnp.int32)
    padded = ((counts + bm - 1) // bm) * bm
    pad_end = jnp.cumsum(padded)
    pad_start = pad_end - padded
    experts = jnp.arange(N_EXPERTS)[:, None, None]
    dest = jnp.sum(jnp.where(e_t[None] == experts, pad_start[:, None, None], 0), axis=0) + rank_t
    nb = -(-m // bm) + N_EXPERTS
    block_e = jnp.sum((pad_end[None, :] <= (jnp.arange(nb) * bm)[:, None]).astype(jnp.int32), axis=1)
    block_e = jnp.minimum(block_e, N_EXPERTS - 1)
    n_used = (pad_end[-1] // bm).astype(jnp.int32).reshape(1)
    block_e = jnp.where(jnp.arange(nb) < n_used[0], block_e, block_e[jnp.maximum(n_used[0] - 1, 0)])
    rows_valid = jnp.clip((pad_start + counts)[block_e] - jnp.arange(nb) * bm, 0, bm).astype(jnp.int32)
    return route, dest, block_e.astype(jnp.int32), rows_valid


def _ln2_kernel(h_ref, gate_ref, *rest, alpha):
    y_refs, (g_ref, b_ref, o_ref) = rest[:TOP_K], rest[TOP_K:]
    gate = gate_ref[...]
    x = alpha * h_ref[...]
    for k in range(TOP_K):
        x = x + gate[:, ROUTE_GATE + k:ROUTE_GATE + k + 1] * _unpack_bf16_pairs(y_refs[k][...])
    o_ref[...] = _ln_rows(x, g_ref[...], b_ref[...])


def _ln2(h, gate, yg, row0, g, b, alpha):
    t = h.shape[0]
    t_all = yg.shape[0] // TOP_K
    tm = min(512, t)
    assert row0 % tm == 0 and t_all % tm == 0
    off = row0 // tm
    nt_all = t_all // tm
    y_specs = [pl.BlockSpec((tm, D_PACK), functools.partial(lambda i, k: (k * nt_all + off + i, 0), k=k))
               for k in range(TOP_K)]
    return pl.pallas_call(
        functools.partial(_ln2_kernel, alpha=alpha),
        grid=(t // tm,),
        in_specs=[pl.BlockSpec((tm, D_MODEL), lambda i: (i, 0)),
                  pl.BlockSpec((tm, LANES), lambda i: (i + off, 0))] + y_specs +
                 [pl.BlockSpec((1, D_MODEL), lambda i: (0, 0)),
                  pl.BlockSpec((1, D_MODEL), lambda i: (0, 0))],
        out_specs=pl.BlockSpec((tm, D_MODEL), lambda i: (i, 0)),
        out_shape=jax.ShapeDtypeStruct((t, D_MODEL), F32),
        compiler_params=_cparams(("parallel",)),
        name="ln2",
    )(h, gate, *([yg] * TOP_K), g, b)


SC_CORES, SC_SUBCORES = 2, 16
SC_CHUNK = 64


def _sc_gather(table, idx):
    b, d = idx.shape[0], table.shape[1]
    workers = SC_CORES * SC_SUBCORES
    per_w = b // workers
    assert per_w * workers == b and per_w % SC_CHUNK == 0
    mesh = plsc.VectorSubcoreMesh(core_axis_name="c", subcore_axis_name="s")

    n_chunks = per_w // SC_CHUNK

    @functools.partial(
        pl.kernel, mesh=mesh, out_type=jax.ShapeDtypeStruct((b, d), table.dtype),
        scratch_types=[pltpu.VMEM((per_w,), jnp.int32), pltpu.VMEM((2, SC_CHUNK, d), table.dtype),
                       pltpu.SemaphoreType.DMA((2,))],
        name="sc_gather")
    def gather(table_hbm, idx_hbm, out_hbm, idx_v, rows_v, sems):
        wid = lax.axis_index("s") * SC_CORES + lax.axis_index("c")
        pltpu.sync_copy(idx_hbm.at[pl.ds(wid * per_w, per_w)], idx_v)

        def fetch(i, slot):
            return pltpu.make_async_copy(table_hbm.at[idx_v.at[pl.ds(i * SC_CHUNK, SC_CHUNK)]],
                                         rows_v.at[slot], sems.at[slot])

        fetch(0, 0).start()

        @pl.loop(0, n_chunks)
        def _(i):
            slot = lax.rem(i, 2)

            @pl.when(i + 1 < n_chunks)
            def _():
                fetch(i + 1, 1 - slot).start()

            fetch(i, slot).wait()
            pltpu.sync_copy(rows_v.at[slot], out_hbm.at[pl.ds(wid * per_w + i * SC_CHUNK, SC_CHUNK)])

    return gather(table, idx)


SC_SCATTER_CHUNK = 48


def _sc_scatter_rows(rows, dest, n_out):
    t, d = rows.shape
    workers = SC_CORES * SC_SUBCORES
    per_w = t // workers
    assert per_w * workers == t and per_w % SC_SCATTER_CHUNK == 0
    mesh = plsc.VectorSubcoreMesh(core_axis_name="c", subcore_axis_name="s")

    @functools.partial(
        pl.kernel, mesh=mesh, out_type=jax.ShapeDtypeStruct((n_out, d), rows.dtype),
        scratch_types=[pltpu.VMEM((TOP_K, SC_SCATTER_CHUNK), jnp.int32),
                       pltpu.VMEM((SC_SCATTER_CHUNK, d), rows.dtype)],
        name="sc_scatter")
    def scatter(rows_hbm, dest_hbm, out_hbm, idx_v, rows_v):
        wid = lax.axis_index("s") * SC_CORES + lax.axis_index("c")

        @pl.loop(0, per_w // SC_SCATTER_CHUNK)
        def _(i):
            base = wid * per_w + i * SC_SCATTER_CHUNK
            pltpu.sync_copy(rows_hbm.at[pl.ds(base, SC_SCATTER_CHUNK)], rows_v)
            for k in range(TOP_K):
                pltpu.sync_copy(dest_hbm.at[pl.ds(k * t + base, SC_SCATTER_CHUNK)], idx_v.at[k])
            for k in range(TOP_K):
                pltpu.sync_copy(rows_v, out_hbm.at[idx_v.at[k]])

    return scatter(rows, dest.reshape(-1))


def _rope_tables(pos):
    half = ROT_DIM // 2
    inv_freq = ROPE_THETA ** (-jnp.arange(0, ROT_DIM, 2, dtype=F32) / ROT_DIM)
    ang = pos.astype(F32)[:, None] * inv_freq[None, :]
    cos, sin = jnp.cos(ang), jnp.sin(ang)
    m = np.arange(LANES) % DQK_A
    idx = m % half
    cos_l = jnp.where(m < ROT_DIM, cos[:, idx], 1.0)
    sa = jnp.where(m < half, -sin[:, idx], 0.0)
    sb = jnp.where((m >= half) & (m < ROT_DIM), sin[:, idx], 0.0)
    return cos_l, sa, sb


def _prep_w_in(w):
    r0 = COL_C
    r1 = r0 + GATE_RANK
    pad = jnp.zeros((w.shape[0], LANES - GATE_RANK), w.dtype)
    return jnp.concatenate([w[:, :r0], w[:, r1:], w[:, r0:r1], pad], axis=1).astype(BF16)


def _tile_lanes(v, reps):
    return jnp.tile(v.reshape(1, -1), (1, reps)).astype(F32)


def _blockdiag_states(s):
    n = s.shape[0]
    eye = jnp.eye(HB, dtype=s.dtype)
    return jnp.einsum('nhde,hg->nhdge', s, eye).reshape(n, HB * DK_B, HB * DV_B)


def _diag_states(sbd):
    n = sbd.shape[0]
    s = sbd.reshape(n, HB, DK_B, HB, DV_B)
    return jnp.stack([s[:, h, :, h, :] for h in range(HB)], axis=1)


def kernel(x_prompt, x_sample, cache_k, cache_v, page_table, state_gla, w_in, lam_q1, lam_k1, lam_q2, lam_k2, attn_norm_g, gla_w_gate, gla_b_gate, gla_norm_g, cmlp_ln_g, cmlp_ln_b, cmlp_ws, cmlp_bs, w_o, ln1_g, ln1_b, router_w, router_b, exp_w1, exp_b1, exp_w2, exp_b2, ln2_g, ln2_b):
    depth = w_in.shape[0]
    bp, s_len, _ = x_prompt.shape
    db, l_new, _ = x_sample.shape
    n_phys, page = cache_k.shape[1], cache_k.shape[2]
    past_len = page_table.shape[1] * page
    alpha = (2 * depth) ** 0.25
    tp, ts = bp * s_len, db * l_new
    bm = MOE_BLOCK

    tabs_p = _rope_tables(jnp.arange(s_len))
    tabs_s = _rope_tables(past_len + (jnp.arange(ts) % l_new))
    page_table = page_table.astype(jnp.int32)

    hp = x_prompt.reshape(tp, D_MODEL)
    hs = x_sample.reshape(ts, D_MODEL)
    outs = {k: [] for k in ("gp", "cs")}
    gs_all = state_gla.astype(F32).reshape(depth * db, HB, DK_B, DV_B)
    kp_all, vp_all = (jnp.zeros((depth * tp, HA, DV_A), F32) for _ in range(2))
    ks_all, vs_all = (jnp.zeros((depth * ts, HA, DV_A), F32) for _ in range(2))
    for l in range(depth):
        lam_init = 0.8 - 0.6 * math.exp(-0.3 * l)
        w = _prep_w_in(w_in[l])
        lamv = jnp.pad(jnp.stack([lam_q1[l], lam_k1[l], lam_q2[l], lam_k2[l]]).astype(F32),
                       ((0, 0), (0, LANES - DQK_A)))
        g_attn = attn_norm_g[l].reshape(1, DV_A).astype(F32)
        wg = jnp.pad(gla_w_gate[l], ((0, LANES - GATE_RANK), (0, 0))).astype(BF16)
        wo = w_o[l].astype(BF16)
        rw = jnp.pad(router_w[l].astype(F32), ((0, 0), (0, LANES - N_EXPERTS)))
        rw_hi = rw.astype(BF16)
        rw = jnp.stack([rw_hi, (rw - rw_hi.astype(F32)).astype(BF16)])
        rb = jnp.pad(router_b[l].astype(F32), (0, LANES - N_EXPERTS), constant_values=NEG_INF).reshape(1, LANES)
        ln1 = (ln1_g[l].reshape(1, D_MODEL), ln1_b[l].reshape(1, D_MODEL))

        def mixer_params(lc, n_rows):
            reps = n_rows // lc
            ws = jnp.tile(cmlp_ws[l][:, :lc, :lc], (1, reps, reps))
            bst = jnp.tile(jnp.repeat(cmlp_bs[l][:, :lc].T, DC, axis=1), (reps, 1))
            return (wg, gla_b_gate[l].reshape(1, W_B), _tile_lanes(gla_norm_g[l], HB),
                    _tile_lanes(cmlp_ln_g[l], HC), _tile_lanes(cmlp_ln_b[l], HC), ws, bst)

        q, kp_all, vp_all, kb, vb, g_in, c_in, br = _inproj(hp, w, tabs_p, l, kp_all, vp_all)
        o_a = _attn_prompt(q, kb, vb, lamv, g_attn, bp, lam_init)
        o_bc, st_p = _mixer_prompt(g_in, br, c_in, mixer_params(CMLP_CHUNK, CMLP_CHUNK), bp)
        rows_k, logits = jnp.zeros((tp + ts, D_PACK), F32), jnp.zeros((tp + ts, LANES), F32)
        hp1, rows_k, logits = _outproj(o_a, o_bc, hp, wo, *ln1, rw, rb, alpha, 0, rows_k, logits)
        outs["gp"].append(_diag_states(st_p))

        q, ks_all, vs_all, kb, vb, g_in, c_in, br = _inproj(hs, w, tabs_s, l, ks_all, vs_all)
        o_a = _attn_sample(q, kb, vb, cache_k, cache_v, l, page_table, lamv, g_attn, l_new, lam_init)
        rows_s = min(64, ts)
        lc = min(l_new, CMLP_CHUNK)
        o_bc, vn, gs_all = _mixer_sample(g_in, br, c_in, gs_all, l, mixer_params(lc, rows_s), l_new, rows_s)
        hs1, rows_k, logits = _outproj(o_a, o_bc, hs, wo, *ln1, rw, rb, alpha, tp, rows_k, logits)
        outs["cs"].append(vn.reshape(db, l_new, W_C))

        gate, dest, block_e, rows_valid = _route(logits, bm)
        x_pad = _sc_scatter_rows(rows_k, dest, block_e.shape[0] * bm)
        y_pad = _moe_experts(x_pad, block_e, rows_valid, exp_w1, exp_b1.reshape(depth, N_EXPERTS, 1, -1),
                             exp_w2, exp_b2.reshape(depth, N_EXPERTS, 1, -1), l, bm)
        yg = _sc_gather(y_pad, dest.reshape(-1))
        hp = _ln2(hp1, gate, yg, 0, ln2_g[l].reshape(1, -1), ln2_b[l].reshape(1, -1), alpha)
        hs = _ln2(hs1, gate, yg, tp, ln2_g[l].reshape(1, -1), ln2_b[l].reshape(1, -1), alpha)

    return (hp.reshape(bp, s_len, D_MODEL), hs.reshape(db, l_new, D_MODEL),
            kp_all.reshape(depth, bp, s_len, HA, DV_A), vp_all.reshape(depth, bp, s_len, HA, DV_A),
            jnp.stack(outs["gp"]),
            ks_all.reshape(depth, db, l_new, HA, DV_A), vs_all.reshape(depth, db, l_new, HA, DV_A),
            gs_all.reshape(depth, db, HB, DK_B, DV_B), jnp.stack(outs["cs"]))
```

```python
import functools
import math

import numpy as np
import jax
import jax.numpy as jnp
from jax import lax
from jax.experimental import pallas as pl
from jax.experimental.pallas import tpu as pltpu
from jax.experimental.pallas import tpu_sc as plsc

F32, BF16 = jnp.float32, jnp.bfloat16
LANES = 128
VMEM_LIMIT = 48 * 1024 * 1024

D_MODEL = 1024
HA, DQK_A, DV_A = 4, 64, 128
ROT_DIM = DQK_A // 4
ROPE_THETA = 500000.0
HB, DK_B, DV_B = 4, 64, 64
GATE_RANK = 16
GATE_NORM = 16.0
HC, DC = 4, 64
CMLP_CHUNK = 128
N_EXPERTS = 32
TOP_K = 4
D_FF = D_MODEL
SWIGLU_LIMIT = 7.0
SWIGLU_ALPHA = 1.702
LN_EPS = 1e-5
RMS_EPS = 1e-6
NEG_INF = -1e30
LOG2E = math.log2(math.e)

W_A = HA * 2 * DQK_A
W_B = HB * DK_B
W_C = HC * DC
COL_G = 3 * W_A
COL_C = COL_G + 4 * W_B
COL_R = COL_C + 2 * W_C
COL_END = COL_R + LANES
GLA_CHUNK_PROMPT = 32
ATTN_BQ, ATTN_BK = 512, 512
ATTN_SAMPLES_PER_STEP = 2
ATTN_ROWS = 2 * ATTN_BQ
D_PACK = D_MODEL // 2
MOE_BLOCK = 512
MOE_FF_CHUNK = 512


def _cparams(sem):
    return pltpu.CompilerParams(dimension_semantics=sem, vmem_limit_bytes=VMEM_LIMIT)


def _split3(x):
    hi = x.astype(BF16)
    r = x - hi.astype(F32)
    mid = r.astype(BF16)
    lo = (r - mid.astype(F32)).astype(BF16)
    return hi, mid, lo


def _dot_sel(sel_bf16, x):
    acc = None
    for p in _split3(x):
        d = jnp.dot(sel_bf16, p, preferred_element_type=F32)
        acc = d if acc is None else acc + d
    return acc


def _seg_sum(x, bd_bf16):
    acc = None
    for p in _split3(x):
        d = jnp.dot(p, bd_bf16, preferred_element_type=F32)
        acc = d if acc is None else acc + d
    return acc


def _pack_bf16_pairs(x):
    u = lax.bitcast_convert_type(x, jnp.uint32)
    r = u + (jnp.uint32(0x7FFF) + ((u >> 16) & jnp.uint32(1)))
    w = x.shape[1] // 2
    word = (r[:, :w] & jnp.uint32(0xFFFF0000)) | (r[:, w:] >> 16)
    return lax.bitcast_convert_type(word, F32)


def _unpack_bf16_pairs(words):
    u = lax.bitcast_convert_type(words, jnp.uint32)
    hi = lax.bitcast_convert_type(u & jnp.uint32(0xFFFF0000), F32)
    lo = lax.bitcast_convert_type(u << 16, F32)
    return jnp.concatenate([hi, lo], axis=1)


def _iota2(shape, dim):
    return lax.broadcasted_iota(jnp.int32, shape, dim)


def _idiv(x, n):
    shift = n.bit_length() - 1
    assert n == 1 << shift
    return x >> shift


def _head_blockdiag(n):
    r, c = _iota2((n, n), 0), _iota2((n, n), 1)
    return _idiv(r, DK_B) == _idiv(c, DK_B)


def _chunk_causal(n, chunk):
    r, c = _iota2((n, n), 0), _iota2((n, n), 1)
    return (_idiv(r, chunk) == _idiv(c, chunk)) & (c <= r)


def _ln_rows(x, g, b):
    mu = jnp.mean(x, axis=-1, keepdims=True)
    xc = x - mu
    var = jnp.mean(xc * xc, axis=-1, keepdims=True)
    return xc * lax.rsqrt(var + LN_EPS) * g + b


def _inproj_kernel(x_ref, w_ref, cos_ref, sa_ref, sb_ref, k_all_ref, v_all_ref,
                   q_ref, k_ref, v_ref, kb_ref, vb_ref, g_ref, c_ref, br_ref):
    del k_all_ref, v_all_ref
    xb = x_ref[...].astype(BF16)

    def proj(a, b):
        return jnp.dot(xb, w_ref[:, a:b], preferred_element_type=F32)

    cos, sa, sb = cos_ref[...], sa_ref[...], sb_ref[...]

    def rope(z):
        outs = []
        for i in range(z.shape[1] // LANES):
            zi = z[:, i * LANES:(i + 1) * LANES]
            outs.append(zi * cos + pltpu.roll(zi, LANES - ROT_DIM // 2, 1) * sa
                        + pltpu.roll(zi, ROT_DIM // 2, 1) * sb)
        return jnp.concatenate(outs, axis=1)

    q_ref[...] = (rope(proj(0, W_A)) * (DQK_A ** -0.5 * LOG2E)).astype(BF16)
    k = rope(proj(W_A, 2 * W_A))
    kb_ref[...] = k.astype(BF16)
    v = proj(2 * W_A, 3 * W_A)
    vb_ref[...] = v.astype(BF16)
    for h in range(HA):
        k_ref[:, h, :] = k[:, h * LANES:(h + 1) * LANES]
        v_ref[:, h, :] = v[:, h * LANES:(h + 1) * LANES]
    g_ref[...] = proj(COL_G, COL_C)
    c_ref[...] = proj(COL_C, COL_R)
    br_ref[...] = proj(COL_R, COL_END)


def _inproj(x, w, tabs, layer, k_all, v_all):
    t = x.shape[0]
    tm = min(512, t)
    nt = t // tm
    cos, sa, sb = tabs
    ntab = cos.shape[0] // tm
    row = lambda n: pl.BlockSpec((tm, n), lambda i: (i, 0))
    tab = pl.BlockSpec((tm, LANES), lambda i: (i % ntab, 0))
    heads = pl.BlockSpec((tm, HA, DV_A), lambda i: (layer * nt + i, 0, 0))
    anywhere = pl.BlockSpec(memory_space=pl.ANY)
    shapes = [((W_A,), BF16), None, None, ((W_A,), BF16), ((W_A,), BF16),
              ((4 * W_B,), F32), ((2 * W_C,), F32), ((LANES,), F32)]
    return pl.pallas_call(
        _inproj_kernel,
        grid=(nt,),
        in_specs=[row(D_MODEL), pl.BlockSpec((D_MODEL, COL_END), lambda i: (0, 0)), tab, tab, tab,
                  anywhere, anywhere],
        out_specs=[heads if s is None else row(s[0][0]) for s in shapes],
        out_shape=[jax.ShapeDtypeStruct(k_all.shape, F32) if s is None else jax.ShapeDtypeStruct((t,) + s[0], s[1])
                   for s in shapes],
        input_output_aliases={5: 1, 6: 2},
        compiler_params=_cparams(("parallel",)),
        name="inproj",
    )(x, w, cos, sa, sb, k_all, v_all)


def _diff_lambda(lamv, lam_init):
    a = jnp.sum(lamv[0:1] * lamv[1:2], axis=1, keepdims=True)
    b = jnp.sum(lamv[2:3] * lamv[3:4], axis=1, keepdims=True)
    return jnp.exp(a) - jnp.exp(b) + lam_init


def _diff_finish(o1, o2, lam, g, lam_init):
    o = o1 - lam * o2
    ms = jnp.mean(o * o, axis=-1, keepdims=True)
    return o * lax.rsqrt(ms + RMS_EPS) * g * (1.0 - lam_init)


def _split_maps(q):
    lane = _iota2(q.shape, 1)
    zero = jnp.zeros_like(q)
    return jnp.concatenate([jnp.where(lane < DQK_A, q, zero), jnp.where(lane >= DQK_A, q, zero)], axis=0)


def _attn_prompt_kernel(q_ref, k_ref, v_ref, lamv_ref, g_ref, o_ref, m_sc, l_sc, acc_sc, *, bq, bk, lam_init):
    qi = pl.program_id(2)
    qq = _split_maps(q_ref[...])
    m_sc[...] = jnp.full(m_sc.shape, NEG_INF, F32)
    l_sc[...] = jnp.zeros(l_sc.shape, F32)
    acc_sc[...] = jnp.zeros(acc_sc.shape, F32)

    def step(j, masked):
        start = pl.multiple_of(j * bk, bk)
        k = k_ref[pl.ds(start, bk), :]
        v = v_ref[pl.ds(start, bk), :]
        tiles = range(bk // LANES)
        for rc in range(2 * bq // ATTN_ROWS):
            rs = slice(rc * ATTN_ROWS, (rc + 1) * ATTN_ROWS)
            s = lax.dot_general(qq[rs], k, (((1,), (1,)), ((), ())), preferred_element_type=F32)
            if masked:
                r = (_iota2(s.shape, 0) + rc * ATTN_ROWS) & (bq - 1)
                c = _iota2(s.shape, 1)
                s = jnp.where(c <= r, s, NEG_INF)
            m_prev = m_sc[rs, :]
            s_max = functools.reduce(jnp.maximum, [s[:, i * LANES:(i + 1) * LANES] for i in tiles])
            m_new = jnp.maximum(m_prev, jnp.max(s_max, axis=1, keepdims=True))
            alpha = jnp.exp2(m_prev - m_new)
            p = jnp.exp2((s - jnp.tile(m_new, (1, bk // LANES))).astype(BF16))
            p_sum = functools.reduce(jnp.add, [p[:, i * LANES:(i + 1) * LANES].astype(F32) for i in tiles])
            l_sc[rs, :] = alpha * l_sc[rs, :] + jnp.sum(p_sum, axis=1, keepdims=True)
            acc_sc[rs, :] = alpha * acc_sc[rs, :] + jnp.dot(p, v, preferred_element_type=F32)
            m_sc[rs, :] = m_new

    def body(j, carry):
        step(j, False)
        return carry

    lax.fori_loop(0, qi, body, 0)
    step(qi, True)

    o = acc_sc[...] / l_sc[...]
    lam = _diff_lambda(lamv_ref[...], lam_init)
    o_ref[...] = _diff_finish(o[:bq], o[bq:], lam, g_ref[...], lam_init).astype(o_ref.dtype)


def _attn_prompt(q, kb, vb, lamv, g, nbatch, lam_init):
    t = q.shape[0]
    s = t // nbatch
    bq, bk = min(ATTN_BQ, s), min(ATTN_BK, s)
    assert bq == bk
    blk = bq
    nq = s // blk
    kern = functools.partial(_attn_prompt_kernel, bq=bq, bk=bk, lam_init=lam_init)
    return pl.pallas_call(
        kern,
        grid=(nbatch, HA, nq),
        in_specs=[pl.BlockSpec((blk, LANES), lambda b, h, i: (b * nq + i, h)),
                  pl.BlockSpec((s, LANES), lambda b, h, i: (b, h)),
                  pl.BlockSpec((s, LANES), lambda b, h, i: (b, h)),
                  pl.BlockSpec((4, LANES), lambda b, h, i: (0, 0)),
                  pl.BlockSpec((1, LANES), lambda b, h, i: (0, 0))],
        out_specs=pl.BlockSpec((blk, LANES), lambda b, h, i: (b * nq + i, h)),
        out_shape=jax.ShapeDtypeStruct((t, W_A), BF16),
        scratch_shapes=[pltpu.VMEM((2 * blk, LANES), F32)] * 3,
        compiler_params=_cparams(("parallel", "parallel", "arbitrary")),
        name="attn_prompt",
    )(q, kb, vb, lamv, g)


def _attn_sample_kernel(pt_ref, q_ref, kn_ref, vn_ref, lamv_ref, g_ref, *rest, n_pages, l_new, lam_init):
    del pt_ref
    spb = q_ref.shape[0]
    o_ref = rest[2 * n_pages * spb]
    for i in range(spb):
        kp = rest[i * n_pages:(i + 1) * n_pages]
        vp = rest[(spb + i) * n_pages:(spb + i + 1) * n_pages]
        _attn_one_sample(q_ref.at[i], kn_ref.at[i], vn_ref.at[i], lamv_ref, g_ref, kp, vp, o_ref.at[i],
                         n_pages=n_pages, l_new=l_new, lam_init=lam_init)


def _attn_one_sample(q_ref, kn_ref, vn_ref, lamv_ref, g_ref, kp, vp, o_ref, *, n_pages, l_new, lam_init):
    rows_pg = kp[0].shape[0]
    rpad = q_ref.shape[0]
    nq = 2 * rpad
    lam = _diff_lambda(lamv_ref[...], lam_init)
    q = q_ref[...]
    qq = jnp.concatenate([_split_maps(q[:, h * LANES:(h + 1) * LANES]) for h in range(HA)],
                         axis=0).astype(BF16)
    r = _iota2((HA * nq, rows_pg), 0)
    c = _iota2((HA * nq, rows_pg), 1)
    head_ok = (c & (HA - 1)) == _idiv(r, nq)
    new_ok = head_ok & (_idiv(c, HA) <= (r & (rpad - 1))) & (c < l_new * HA)
    nt = (((1,), (1,)), ((), ()))
    zpad = jnp.zeros((rows_pg - kn_ref.shape[0], LANES), F32)
    k_new = jnp.concatenate([kn_ref[...], zpad], axis=0).astype(BF16)
    v_new = jnp.concatenate([vn_ref[...], zpad], axis=0).astype(BF16)
    s_new = jnp.where(new_ok, lax.dot_general(qq, k_new, nt, preferred_element_type=F32), NEG_INF)
    s_past = [jnp.where(head_ok, lax.dot_general(qq, kp[j][...].astype(BF16), nt, preferred_element_type=F32),
                        NEG_INF) for j in range(n_pages)]
    m = jnp.max(s_new, axis=1, keepdims=True)
    for sj in s_past:
        m = jnp.maximum(m, jnp.max(sj, axis=1, keepdims=True))
    p_new = jnp.exp2(s_new - m)
    l = jnp.sum(p_new, axis=1, keepdims=True)
    acc = jnp.dot(p_new.astype(BF16), v_new, preferred_element_type=F32)
    for j in range(n_pages):
        pj = jnp.exp2(s_past[j] - m)
        l = l + jnp.sum(pj, axis=1, keepdims=True)
        acc = acc + jnp.dot(pj.astype(BF16), vp[j][...].astype(BF16), preferred_element_type=F32)
    o = acc / l
    outs = [_diff_finish(o[h * nq:h * nq + rpad], o[h * nq + rpad:(h + 1) * nq], lam, g_ref[...], lam_init)
            for h in range(HA)]
    o_ref[...] = jnp.concatenate(outs, axis=1)


def _attn_sample(q, k, v, cache_k, cache_v, layer, page_table, lamv, g, l_new, lam_init):
    t = q.shape[0]
    db = t // l_new
    n_pages = page_table.shape[1]
    page = cache_k.shape[2]
    rpad = 8

    def pad_rows(a):
        a = a.reshape(db, l_new, W_A).astype(F32)
        return jnp.concatenate([a, jnp.zeros((db, rpad - l_new, W_A), F32)], axis=1)

    def new_rows(a):
        return a.reshape(db, l_new * HA, DV_A).astype(F32)

    depth, n_phys = cache_k.shape[:2]
    cache_k = cache_k.reshape(depth, n_phys, page * HA, DV_A)
    cache_v = cache_v.reshape(depth, n_phys, page * HA, DV_A)
    spb = ATTN_SAMPLES_PER_STEP if db % ATTN_SAMPLES_PER_STEP == 0 else 1
    q_spec = pl.BlockSpec((spb, rpad, W_A), lambda b, pt: (b, 0, 0))
    new_spec = pl.BlockSpec((spb, l_new * HA, DV_A), lambda b, pt: (b, 0, 0))
    page_specs = [pl.BlockSpec((None, None, page * HA, DV_A),
                               functools.partial(lambda b, pt, i, j: (layer, pt[b * spb + i, j], 0, 0), i=i, j=j))
                  for i in range(spb) for j in range(n_pages)]
    kern = functools.partial(_attn_sample_kernel, n_pages=n_pages, l_new=l_new, lam_init=lam_init)
    out = pl.pallas_call(
        kern,
        grid_spec=pltpu.PrefetchScalarGridSpec(
            num_scalar_prefetch=1,
            grid=(db // spb,),
            in_specs=[q_spec, new_spec, new_spec,
                      pl.BlockSpec((4, LANES), lambda b, pt: (0, 0)),
                      pl.BlockSpec((1, LANES), lambda b, pt: (0, 0))] + page_specs + page_specs,
            out_specs=q_spec,
        ),
        out_shape=jax.ShapeDtypeStruct((db, rpad, W_A), F32),
        compiler_params=_cparams(("arbitrary",)),
        name="attn_sample",
    )(page_table, pad_rows(q), new_rows(k), new_rows(v), lamv, g,
      *([cache_k] * (n_pages * spb)), *([cache_v] * (n_pages * spb)))
    return out[:, :l_new].reshape(t, W_A).astype(BF16)


def _log_sigmoid(x):
    return jnp.minimum(x, 0.0) - jnp.log(1.0 + jnp.exp(-jnp.abs(x)))


def _gla_gate(br, wg_ref, bgate_ref):
    x = jnp.dot(br.astype(BF16), wg_ref[...], preferred_element_type=F32) + bgate_ref[...]
    return _log_sigmoid(x) / GATE_NORM


def _gla_intra(q_att, k_in, v, chunk):
    keep = _chunk_causal(q_att.shape[0], chunk)
    lane = _idiv(_iota2(q_att.shape, 1), DK_B)
    kb = k_in.astype(BF16)
    atts, vs = [], []
    for h in range(HB):
        qh = jnp.where(lane == h, q_att, 0.0).astype(BF16)
        a = lax.dot_general(qh, kb, (((1,), (1,)), ((), ())), preferred_element_type=F32)
        atts.append(jnp.where(keep, a, 0.0).astype(BF16))
        vs.append(jnp.where(lane == h, v, 0.0).astype(BF16))
    return jnp.dot(jnp.concatenate(atts, axis=1), jnp.concatenate(vs, axis=0), preferred_element_type=F32)


def _gla_finish(o, gate_in, gng_ref, bd):
    ms = _seg_sum(o * o, bd) * (1.0 / DV_B)
    o = o * lax.rsqrt(ms + RMS_EPS) * gng_ref[...]
    return o * (gate_in * (1.0 / (1.0 + jnp.exp(-gate_in))))


def _chunk_mlp(c_in, lng_ref, lnb_ref, ws_ref, bst, chunk, bd):
    n = c_in.shape[0]
    cu, cv = c_in[:, :W_C], c_in[:, W_C:]
    mu = _seg_sum(cv, bd) * (1.0 / DC)
    xc = cv - mu
    var = _seg_sum(xc * xc, bd) * (1.0 / DC)
    vn = xc * lax.rsqrt(var + LN_EPS) * lng_ref[...] + lnb_ref[...]
    keep = _chunk_causal(n, chunk)
    lane = _idiv(_iota2(vn.shape, 1), DC)
    ws, vs = [], []
    for g in range(HC):
        ws.append(jnp.where(keep, ws_ref[g], 0.0).astype(BF16))
        vs.append(jnp.where(lane == g, vn, 0.0).astype(BF16))
    mixed = jnp.dot(jnp.concatenate(ws, axis=1), jnp.concatenate(vs, axis=0), preferred_element_type=F32) + bst
    return cu * mixed, vn


def _mixer_prompt_kernel(g_ref, br_ref, c_ref, wg_ref, bgate_ref, gng_ref, lng_ref, lnb_ref, ws_ref, bst_ref,
                         o_ref, st_ref, st_sc, *, ts, chunk):
    t = pl.program_id(1)

    @pl.when(t == 0)
    def _():
        st_sc[...] = jnp.zeros(st_sc.shape, F32)

    grp = CMLP_CHUNK
    bd = _head_blockdiag(W_B)
    bd_bf = jnp.where(bd, 1.0, 0.0).astype(BF16)
    csum_sel = jnp.where(_chunk_causal(grp, chunk), 1.0, 0.0).astype(BF16)
    rows = _iota2((grp, W_B), 0)
    half = chunk // 2
    st = st_sc[...]
    for gi in range(ts // grp):
        rs = slice(gi * grp, (gi + 1) * grp)
        g = g_ref[rs, :]
        gq = g[:, 0:W_B] * (DK_B ** -0.5)
        gk, gv, gg = g[:, W_B:2 * W_B], g[:, 2 * W_B:3 * W_B], g[:, 3 * W_B:4 * W_B]
        la = _gla_gate(br_ref[rs, :], wg_ref, bgate_ref)
        bcum = _dot_sel(csum_sel, la)
        mids, lasts = [], []
        for ci in range(grp // chunk):
            mids.append(jnp.broadcast_to(bcum[ci * chunk + half - 1:ci * chunk + half, :], (chunk, W_B)))
            lasts.append(jnp.broadcast_to(bcum[(ci + 1) * chunk - 1:(ci + 1) * chunk, :], (chunk, W_B)))
        bmid = jnp.concatenate(mids, axis=0)
        blast = jnp.concatenate(lasts, axis=0)
        q_att = gq * jnp.exp(bcum - bmid)
        k_in = gk * jnp.exp(bmid - bcum)
        k_end = gk * jnp.exp(blast - bcum)
        q_dec = (gq * jnp.exp(bcum)).astype(BF16)
        o = _gla_intra(q_att, k_in, gv, chunk)
        v_t = gv.T.astype(BF16)
        o_inter = []
        for ci in range(grp // chunk):
            cs = slice(ci * chunk, (ci + 1) * chunk)
            o_inter.append(lax.dot_general(q_dec[cs], st.astype(BF16), (((1,), (1,)), ((), ())),
                                           preferred_element_type=F32))
            kem = jnp.where(_idiv(rows, chunk) == ci, k_end, 0.0).astype(BF16)
            upd = jnp.dot(v_t, kem, preferred_element_type=F32)
            dl = jnp.exp(blast[ci * chunk:ci * chunk + 1, :])
            st = st * dl + jnp.where(bd, upd, 0.0)
        o = o + jnp.concatenate(o_inter, axis=0)
        o_b = _gla_finish(o, gg, gng_ref, bd_bf)
        o_c, _ = _chunk_mlp(c_ref[rs, :], lng_ref, lnb_ref, ws_ref, bst_ref[...], CMLP_CHUNK, bd_bf)
        o_ref[rs, :] = jnp.concatenate([o_b, o_c], axis=1).astype(o_ref.dtype)
    st_sc[...] = st

    @pl.when(t == pl.num_programs(1) - 1)
    def _():
        st_ref[...] = st_sc[...].T


def _mixer_prompt(g_in, br, c_in, prm, nbatch, ts=1024):
    t = g_in.shape[0]
    nt = t // nbatch // ts
    row = lambda n: pl.BlockSpec((ts, n), lambda b, i: (b * nt + i, 0))
    full = lambda a: pl.BlockSpec(a.shape, lambda b, i: (0,) * a.ndim)
    kern = functools.partial(_mixer_prompt_kernel, ts=ts, chunk=GLA_CHUNK_PROMPT)
    return pl.pallas_call(
        kern,
        grid=(nbatch, nt),
        in_specs=[row(4 * W_B), row(LANES), row(2 * W_C)] + [full(a) for a in prm],
        out_specs=[row(W_B + W_C), pl.BlockSpec((None, W_B, W_B), lambda b, i: (b, 0, 0))],
        out_shape=[jax.ShapeDtypeStruct((t, W_B + W_C), BF16),
                   jax.ShapeDtypeStruct((nbatch, W_B, W_B), F32)],
        scratch_shapes=[pltpu.VMEM((W_B, W_B), F32)],
        compiler_params=_cparams(("parallel", "arbitrary")),
        name="mixer_prompt",
    )(g_in, br, c_in, *prm)


def _mixer_sample_kernel(g_ref, br_ref, c_ref, s0_ref, wg_ref, bgate_ref, gng_ref, lng_ref, lnb_ref, ws_ref,
                         bst_ref, o_ref, vn_ref, st_ref, *, l_new):
    n = g_ref.shape[0]
    bd = _head_blockdiag(W_B)
    bd_bf = jnp.where(bd, 1.0, 0.0).astype(BF16)
    r, c = _iota2((n, n), 0), _iota2((n, n), 1)
    csum_sel = jnp.where(_chunk_causal(n, l_new), 1.0, 0.0).astype(BF16)
    last_sel = jnp.where(_idiv(r, l_new) == _idiv(c, l_new), 1.0, 0.0).astype(BF16)
    g = g_ref[...]
    gq = g[:, 0:W_B] * (DK_B ** -0.5)
    gk, gv, gg = g[:, W_B:2 * W_B], g[:, 2 * W_B:3 * W_B], g[:, 3 * W_B:4 * W_B]
    la = _gla_gate(br_ref[...], wg_ref, bgate_ref)
    bcum = _dot_sel(csum_sel, la)
    blast = _dot_sel(last_sel, la)
    q_in = gq * jnp.exp(bcum)
    k_in = gk * jnp.exp(-bcum)
    k_end = gk * jnp.exp(blast - bcum)
    o = _gla_intra(q_in, k_in, gv, l_new)
    zrows = jnp.zeros((LANES - n, W_B), F32)
    ke_t = jnp.concatenate([k_end, zrows], axis=0).T
    bl_t = jnp.concatenate([blast, zrows], axis=0).T
    v_pad = jnp.concatenate([gv, zrows], axis=0).astype(BF16)
    rows = _iota2((n, W_B), 0)
    cols = _iota2((W_B, LANES), 1)
    zblk = jnp.zeros((DK_B, DV_B), F32)
    for s in range(n // l_new):
        s0 = jnp.concatenate(
            [jnp.concatenate([s0_ref[s, h] if g == h else zblk for g in range(HB)], axis=1) for h in range(HB)],
            axis=0)
        qs = jnp.where(_idiv(rows, l_new) == s, q_in, 0.0).astype(BF16)
        o = o + jnp.dot(qs, s0.astype(BF16), preferred_element_type=F32)
        kes = jnp.where(_idiv(cols, l_new) == s, ke_t, 0.0).astype(BF16)
        upd = jnp.dot(kes, v_pad, preferred_element_type=F32)
        dl = jnp.exp(bl_t[:, s * l_new:s * l_new + 1])
        fin = s0 * dl + upd
        for h in range(HB):
            st_ref[s, h] = fin[h * DK_B:(h + 1) * DK_B, h * DV_B:(h + 1) * DV_B]
    o_b = _gla_finish(o, gg, gng_ref, bd_bf)
    o_c, vn = _chunk_mlp(c_ref[...], lng_ref, lnb_ref, ws_ref, bst_ref[...], l_new, bd_bf)
    o_ref[...] = jnp.concatenate([o_b, o_c], axis=1).astype(o_ref.dtype)
    vn_ref[...] = vn


def _mixer_sample(g_in, br, c_in, states, layer, prm, l_new, ts=64):
    t = g_in.shape[0]
    ns = ts // l_new
    nblk = t // ts
    row = lambda n: pl.BlockSpec((ts, n), lambda i: (i, 0))
    full = lambda a: pl.BlockSpec(a.shape, lambda i: (0,) * a.ndim)
    st = pl.BlockSpec((ns, HB, DK_B, DV_B), lambda i: (layer * nblk + i, 0, 0, 0))
    kern = functools.partial(_mixer_sample_kernel, l_new=l_new)
    return pl.pallas_call(
        kern,
        grid=(nblk,),
        in_specs=[row(4 * W_B), row(LANES), row(2 * W_C), st] + [full(a) for a in prm],
        out_specs=[row(W_B + W_C), row(W_C), st],
        out_shape=[jax.ShapeDtypeStruct((t, W_B + W_C), BF16),
                   jax.ShapeDtypeStruct((t, W_C), F32),
                   jax.ShapeDtypeStruct(states.shape, F32)],
        input_output_aliases={3: 2},
        compiler_params=_cparams(("parallel",)),
        name="mixer_sample",
    )(g_in, br, c_in, states, *prm)


def _outproj_kernel(oa_ref, obc_ref, x_ref, wo_ref, g_ref, b_ref, rw_ref, rb_ref, *rest, alpha):
    h_ref, hp_ref, lg_ref = rest[-3:]
    y = jnp.dot(oa_ref[...], wo_ref[0:W_A, :], preferred_element_type=F32)
    y = y + jnp.dot(obc_ref[...], wo_ref[W_A:, :], preferred_element_type=F32)
    h = _ln_rows(alpha * x_ref[...] + y, g_ref[...], b_ref[...])
    h_ref[...] = h
    hp_ref[...] = _pack_bf16_pairs(h)
    h_hi = h.astype(BF16)
    h_lo = (h - h_hi.astype(F32)).astype(BF16)
    lg = jnp.dot(h_hi, rw_ref[0], preferred_element_type=F32)
    lg = lg + jnp.dot(h_lo, rw_ref[0], preferred_element_type=F32)
    lg = lg + jnp.dot(h_hi, rw_ref[1], preferred_element_type=F32)
    lg_ref[...] = lg + rb_ref[...]


def _outproj(o_a, o_bc, x, wo, g, b, rw, rb, alpha, row0, rows_all, logits_all):
    t = x.shape[0]
    tm = min(512, t)
    assert row0 % tm == 0
    off = row0 // tm
    row = lambda n: pl.BlockSpec((tm, n), lambda i: (i, 0))
    row_at = lambda n: pl.BlockSpec((tm, n), lambda i: (i + off, 0))
    full = lambda a: pl.BlockSpec(a.shape, lambda i: (0,) * a.ndim)
    anywhere = pl.BlockSpec(memory_space=pl.ANY)
    return pl.pallas_call(
        functools.partial(_outproj_kernel, alpha=alpha),
        grid=(t // tm,),
        in_specs=[row(W_A), row(W_B + W_C), row(D_MODEL)] + [full(a) for a in (wo, g, b, rw, rb)]
                 + [anywhere, anywhere],
        out_specs=[row(D_MODEL), row_at(D_PACK), row_at(LANES)],
        out_shape=[jax.ShapeDtypeStruct((t, D_MODEL), F32), jax.ShapeDtypeStruct(rows_all.shape, F32),
                   jax.ShapeDtypeStruct(logits_all.shape, F32)],
        input_output_aliases={8: 1, 9: 2},
        compiler_params=_cparams(("parallel",)),
        name="outproj",
    )(o_a, o_bc, x, wo, g, b, rw, rb, rows_all, logits_all)


def _moe_kernel(be_ref, rv_ref, x_ref, w1_ref, b1_ref, w2_ref, b2_ref, y_ref, w1b_sc, w2b_sc):
    i = pl.program_id(0)

    @pl.when((i == 0) | (be_ref[i] != be_ref[jnp.maximum(i - 1, 0)]))
    def _():
        w1b_sc[...] = w1_ref[...].astype(BF16)
        w2b_sc[...] = w2_ref[...].astype(BF16)

    @pl.when(rv_ref[i] == 0)
    def _():
        y_ref[...] = jnp.zeros(y_ref.shape, F32)

    @pl.when(rv_ref[i] > 0)
    def _():
        x = _unpack_bf16_pairs(x_ref[...])
        xb = jnp.where(_iota2(x.shape, 0) < rv_ref[i], x, 0.0).astype(BF16)
        acc = None
        for c in range(D_FF // MOE_FF_CHUNK):
            gs = slice(c * MOE_FF_CHUNK, (c + 1) * MOE_FF_CHUNK)
            us = slice(D_FF + c * MOE_FF_CHUNK, D_FF + (c + 1) * MOE_FF_CHUNK)
            g = jnp.dot(xb, w1b_sc[:, gs], preferred_element_type=F32) + b1_ref[:, gs]
            u = jnp.dot(xb, w1b_sc[:, us], preferred_element_type=F32) + b1_ref[:, us]
            g = jnp.minimum(g, SWIGLU_LIMIT)
            u = jnp.clip(u, -SWIGLU_LIMIT, SWIGLU_LIMIT)
            act = (u + 1.0) * g * (1.0 / (1.0 + jnp.exp(-SWIGLU_ALPHA * g)))
            part = jnp.dot(act.astype(BF16), w2b_sc[gs, :], preferred_element_type=F32)
            acc = part if acc is None else acc + part
        y_ref[...] = _pack_bf16_pairs(acc + b2_ref[...])


def _moe_experts(x_pad, block_e, rows_valid, w1, b1, w2, b2, layer, bm):
    nb = x_pad.shape[0] // bm
    return pl.pallas_call(
        _moe_kernel,
        grid_spec=pltpu.PrefetchScalarGridSpec(
            num_scalar_prefetch=2,
            grid=(nb,),
            in_specs=[pl.BlockSpec((bm, D_PACK), lambda i, be, rv: (i, 0)),
                      pl.BlockSpec((None, None, D_MODEL, 2 * D_FF), lambda i, be, rv: (layer, be[i], 0, 0)),
                      pl.BlockSpec((None, None, 1, 2 * D_FF), lambda i, be, rv: (layer, be[i], 0, 0)),
                      pl.BlockSpec((None, None, D_FF, D_MODEL), lambda i, be, rv: (layer, be[i], 0, 0)),
                      pl.BlockSpec((None, None, 1, D_MODEL), lambda i, be, rv: (layer, be[i], 0, 0))],
            out_specs=pl.BlockSpec((bm, D_PACK), lambda i, be, rv: (i, 0)),
            scratch_shapes=[pltpu.VMEM((D_MODEL, 2 * D_FF), BF16), pltpu.VMEM((D_FF, D_MODEL), BF16)],
        ),
        out_shape=jax.ShapeDtypeStruct((nb * bm, D_PACK), F32),
        compiler_params=_cparams(("arbitrary",)),
        name="moe_experts",
    )(block_e, rows_valid, x_pad, w1, b1, w2, b2)


ROUTE_E, ROUTE_RANK, ROUTE_GATE = 0, TOP_K, 2 * TOP_K


def _router_kernel(lg_ref, route_ref, route_t_ref, cnt_ref, cnt_sc):
    @pl.when(pl.program_id(0) == 0)
    def _():
        cnt_sc[...] = jnp.zeros(cnt_sc.shape, F32)

    work = lg_ref[...]
    tm = work.shape[0]
    lane = _iota2(work.shape, 1)
    lane_f = lane.astype(F32)
    sels, vals, ids = [], [], []
    for _ in range(TOP_K):
        mx = jnp.max(work, axis=1, keepdims=True)
        idx = jnp.min(jnp.where(work == mx, lane_f, float(LANES)), axis=1, keepdims=True)
        sel = lane_f == idx
        sels.append(sel)
        vals.append(mx)
        ids.append(idx)
        work = jnp.where(sel, -jnp.inf, work)
    ex = [jnp.exp(v - vals[0]) for v in vals]
    den = ex[0]
    for x in ex[1:]:
        den = den + x
    picked = jnp.zeros(work.shape, F32)
    for sel in sels:
        picked = jnp.where(sel, 1.0, picked)
    r, c = _iota2((tm, tm), 0), _iota2((tm, tm), 1)
    before = jnp.dot(jnp.where(c < r, 1.0, 0.0).astype(BF16), picked.astype(BF16), preferred_element_type=F32)
    before = before + cnt_sc[...]
    out = jnp.zeros(work.shape, F32)
    for k in range(TOP_K):
        rank = jnp.sum(jnp.where(sels[k], before, 0.0), axis=1, keepdims=True)
        out = jnp.where(lane == ROUTE_E + k, ids[k], out)
        out = jnp.where(lane == ROUTE_RANK + k, rank, out)
        out = jnp.where(lane == ROUTE_GATE + k, ex[k] / den, out)
    route_ref[...] = out
    route_t_ref[...] = out.T[0:ROUTE_GATE, :]
    cnt_sc[...] = cnt_sc[...] + jnp.sum(picked, axis=0, keepdims=True)
    cnt_ref[...] = cnt_sc[...]


def _router(logits):
    t = logits.shape[0]
    tm = min(512, t)
    return pl.pallas_call(
        _router_kernel,
        grid=(t // tm,),
        in_specs=[pl.BlockSpec((tm, LANES), lambda i: (i, 0))],
        out_specs=[pl.BlockSpec((tm, LANES), lambda i: (i, 0)), pl.BlockSpec((ROUTE_GATE, tm), lambda i: (0, i)),
                   pl.BlockSpec((1, LANES), lambda i: (0, 0))],
        out_shape=[jax.ShapeDtypeStruct((t, LANES), F32), jax.ShapeDtypeStruct((ROUTE_GATE, t), F32),
                   jax.ShapeDtypeStruct((1, LANES), F32)],
        scratch_shapes=[pltpu.VMEM((1, LANES), F32)],
        compiler_params=_cparams(("arbitrary",)),
        name="router",
    )(logits)


def _route(logits, bm):
    t = logits.shape[0]
    m = t * TOP_K
    route, route_t, cnt = _router(logits)
    e_t = route_t[ROUTE_E:ROUTE_E + TOP_K].astype(jnp.int32)
    rank_t = route_t[ROUTE_RANK:ROUTE_RANK + TOP_K].astype(jnp.int32)
    counts = cnt[0, :N_EXPERTS].astype(jnp.int32)
    padded = ((counts + bm - 1) // bm) * bm
    pad_end = jnp.cumsum(padded)
    pad_start = pad_end - padded
    experts = jnp.arange(N_EXPERTS)[:, None, None]
    dest = jnp.sum(jnp.where(e_t[None] == experts, pad_start[:, None, None], 0), axis=0) + rank_t
    nb = -(-m // bm) + N_EXPERTS
    block_e = jnp.sum((pad_end[None, :] <= (jnp.arange(nb) * bm)[:, None]).astype(jnp.int32), axis=1)
    block_e = jnp.minimum(block_e, N_EXPERTS - 1)
    n_used = (pad_end[-1] // bm).astype(jnp.int32).reshape(1)
    block_e = jnp.where(jnp.arange(nb) < n_used[0], block_e, block_e[jnp.maximum(n_used[0] - 1, 0)])
    rows_valid = jnp.clip((pad_start + counts)[block_e] - jnp.arange(nb) * bm, 0, bm).astype(jnp.int32)
    return route, dest, block_e.astype(jnp.int32), rows_valid


def _ln2_kernel(h_ref, gate_ref, *rest, alpha):
    y_refs, (g_ref, b_ref, o_ref) = rest[:TOP_K], rest[TOP_K:]
    gate = gate_ref[...]
    x = alpha * h_ref[...]
    for k in range(TOP_K):
        x = x + gate[:, ROUTE_GATE + k:ROUTE_GATE + k + 1] * _unpack_bf16_pairs(y_refs[k][...])
    o_ref[...] = _ln_rows(x, g_ref[...], b_ref[...])


def _ln2(h, gate, yg, row0, g, b, alpha):
    t = h.shape[0]
    t_all = yg.shape[0] // TOP_K
    tm = min(512, t)
    assert row0 % tm == 0 and t_all % tm == 0
    off = row0 // tm
    nt_all = t_all // tm
    y_specs = [pl.BlockSpec((tm, D_PACK), functools.partial(lambda i, k: (k * nt_all + off + i, 0), k=k))
               for k in range(TOP_K)]
    return pl.pallas_call(
        functools.partial(_ln2_kernel, alpha=alpha),
        grid=(t // tm,),
        in_specs=[pl.BlockSpec((tm, D_MODEL), lambda i: (i, 0)),
                  pl.BlockSpec((tm, LANES), lambda i: (i + off, 0))] + y_specs +
                 [pl.BlockSpec((1, D_MODEL), lambda i: (0, 0)),
                  pl.BlockSpec((1, D_MODEL), lambda i: (0, 0))],
        out_specs=pl.BlockSpec((tm, D_MODEL), lambda i: (i, 0)),
        out_shape=jax.ShapeDtypeStruct((t, D_MODEL), F32),
        compiler_params=_cparams(("parallel",)),
        name="ln2",
    )(h, gate, *([yg] * TOP_K), g, b)


SC_CORES, SC_SUBCORES = 2, 16
SC_CHUNK = 64


def _sc_gather(table, idx):
    b, d = idx.shape[0], table.shape[1]
    workers = SC_CORES * SC_SUBCORES
    per_w = b // workers
    assert per_w * workers == b and per_w % SC_CHUNK == 0
    mesh = plsc.VectorSubcoreMesh(core_axis_name="c", subcore_axis_name="s")

    n_chunks = per_w // SC_CHUNK

    @functools.partial(
        pl.kernel, mesh=mesh, out_type=jax.ShapeDtypeStruct((b, d), table.dtype),
        scratch_types=[pltpu.VMEM((per_w,), jnp.int32), pltpu.VMEM((2, SC_CHUNK, d), table.dtype),
                       pltpu.SemaphoreType.DMA((2,))],
        name="sc_gather")
    def gather(table_hbm, idx_hbm, out_hbm, idx_v, rows_v, sems):
        wid = lax.axis_index("s") * SC_CORES + lax.axis_index("c")
        pltpu.sync_copy(idx_hbm.at[pl.ds(wid * per_w, per_w)], idx_v)

        def fetch(i, slot):
            return pltpu.make_async_copy(table_hbm.at[idx_v.at[pl.ds(i * SC_CHUNK, SC_CHUNK)]],
                                         rows_v.at[slot], sems.at[slot])

        fetch(0, 0).start()

        @pl.loop(0, n_chunks)
        def _(i):
            slot = lax.rem(i, 2)

            @pl.when(i + 1 < n_chunks)
            def _():
                fetch(i + 1, 1 - slot).start()

            fetch(i, slot).wait()
            pltpu.sync_copy(rows_v.at[slot], out_hbm.at[pl.ds(wid * per_w + i * SC_CHUNK, SC_CHUNK)])

    return gather(table, idx)


SC_SCATTER_CHUNK = 48


def _sc_scatter_rows(rows, dest, n_out):
    t, d = rows.shape
    workers = SC_CORES * SC_SUBCORES
    per_w = t // workers
    assert per_w * workers == t and per_w % SC_SCATTER_CHUNK == 0
    mesh = plsc.VectorSubcoreMesh(core_axis_name="c", subcore_axis_name="s")

    @functools.partial(
        pl.kernel, mesh=mesh, out_type=jax.ShapeDtypeStruct((n_out, d), rows.dtype),
        scratch_types=[pltpu.VMEM((TOP_K, SC_SCATTER_CHUNK), jnp.int32),
                       pltpu.VMEM((SC_SCATTER_CHUNK, d), rows.dtype)],
        name="sc_scatter")
    def scatter(rows_hbm, dest_hbm, out_hbm, idx_v, rows_v):
        wid = lax.axis_index("s") * SC_CORES + lax.axis_index("c")

        @pl.loop(0, per_w // SC_SCATTER_CHUNK)
        def _(i):
            base = wid * per_w + i * SC_SCATTER_CHUNK
            pltpu.sync_copy(rows_hbm.at[pl.ds(base, SC_SCATTER_CHUNK)], rows_v)
            for k in range(TOP_K):
                pltpu.sync_copy(dest_hbm.at[pl.ds(k * t + base, SC_SCATTER_CHUNK)], idx_v.at[k])
            for k in range(TOP_K):
                pltpu.sync_copy(rows_v, out_hbm.at[idx_v.at[k]])

    return scatter(rows, dest.reshape(-1))


def _rope_tables(pos):
    half = ROT_DIM // 2
    inv_freq = ROPE_THETA ** (-jnp.arange(0, ROT_DIM, 2, dtype=F32) / ROT_DIM)
    ang = pos.astype(F32)[:, None] * inv_freq[None, :]
    cos, sin = jnp.cos(ang), jnp.sin(ang)
    m = np.arange(LANES) % DQK_A
    idx = m % half
    cos_l = jnp.where(m < ROT_DIM, cos[:, idx], 1.0)
    sa = jnp.where(m < half, -sin[:, idx], 0.0)
    sb = jnp.where((m >= half) & (m < ROT_DIM), sin[:, idx], 0.0)
    return cos_l, sa, sb


def _prep_w_in(w):
    r0 = COL_C
    r1 = r0 + GATE_RANK
    pad = jnp.zeros((w.shape[0], LANES - GATE_RANK), w.dtype)
    return jnp.concatenate([w[:, :r0], w[:, r1:], w[:, r0:r1], pad], axis=1).astype(BF16)


def _tile_lanes(v, reps):
    return jnp.tile(v.reshape(1, -1), (1, reps)).astype(F32)


def _blockdiag_states(s):
    n = s.shape[0]
    eye = jnp.eye(HB, dtype=s.dtype)
    return jnp.einsum('nhde,hg->nhdge', s, eye).reshape(n, HB * DK_B, HB * DV_B)


def _diag_states(sbd):
    n = sbd.shape[0]
    s = sbd.reshape(n, HB, DK_B, HB, DV_B)
    return jnp.stack([s[:, h, :, h, :] for h in range(HB)], axis=1)


def kernel(x_prompt, x_sample, cache_k, cache_v, page_table, state_gla, w_in, lam_q1, lam_k1, lam_q2, lam_k2, attn_norm_g, gla_w_gate, gla_b_gate, gla_norm_g, cmlp_ln_g, cmlp_ln_b, cmlp_ws, cmlp_bs, w_o, ln1_g, ln1_b, router_w, router_b, exp_w1, exp_b1, exp_w2, exp_b2, ln2_g, ln2_b):
    depth = w_in.shape[0]
    bp, s_len, _ = x_prompt.shape
    db, l_new, _ = x_sample.shape
    n_phys, page = cache_k.shape[1], cache_k.shape[2]
    past_len = page_table.shape[1] * page
    alpha = (2 * depth) ** 0.25
    tp, ts = bp * s_len, db * l_new
    bm = MOE_BLOCK

    tabs_p = _rope_tables(jnp.arange(s_len))
    tabs_s = _rope_tables(past_len + (jnp.arange(ts) % l_new))
    page_table = page_table.astype(jnp.int32)

    hp = x_prompt.reshape(tp, D_MODEL)
    hs = x_sample.reshape(ts, D_MODEL)
    outs = {k: [] for k in ("gp", "cs")}
    gs_all = state_gla.astype(F32).reshape(depth * db, HB, DK_B, DV_B)
    kp_all, vp_all = (jnp.zeros((depth * tp, HA, DV_A), F32) for _ in range(2))
    ks_all, vs_all = (jnp.zeros((depth * ts, HA, DV_A), F32) for _ in range(2))
    for l in range(depth):
        lam_init = 0.8 - 0.6 * math.exp(-0.3 * l)
        w = _prep_w_in(w_in[l])
        lamv = jnp.pad(jnp.stack([lam_q1[l], lam_k1[l], lam_q2[l], lam_k2[l]]).astype(F32),
                       ((0, 0), (0, LANES - DQK_A)))
        g_attn = attn_norm_g[l].reshape(1, DV_A).astype(F32)
        wg = jnp.pad(gla_w_gate[l], ((0, LANES - GATE_RANK), (0, 0))).astype(BF16)
        wo = w_o[l].astype(BF16)
        rw = jnp.pad(router_w[l].astype(F32), ((0, 0), (0, LANES - N_EXPERTS)))
        rw_hi = rw.astype(BF16)
        rw = jnp.stack([rw_hi, (rw - rw_hi.astype(F32)).astype(BF16)])
        rb = jnp.pad(router_b[l].astype(F32), (0, LANES - N_EXPERTS), constant_values=NEG_INF).reshape(1, LANES)
        ln1 = (ln1_g[l].reshape(1, D_MODEL), ln1_b[l].reshape(1, D_MODEL))

        def mixer_params(lc, n_rows):
            reps = n_rows // lc
            ws = jnp.tile(cmlp_ws[l][:, :lc, :lc], (1, reps, reps))
            bst = jnp.tile(jnp.repeat(cmlp_bs[l][:, :lc].T, DC, axis=1), (reps, 1))
            return (wg, gla_b_gate[l].reshape(1, W_B), _tile_lanes(gla_norm_g[l], HB),
                    _tile_lanes(cmlp_ln_g[l], HC), _tile_lanes(cmlp_ln_b[l], HC), ws, bst)

        q, kp_all, vp_all, kb, vb, g_in, c_in, br = _inproj(hp, w, tabs_p, l, kp_all, vp_all)
        o_a = _attn_prompt(q, kb, vb, lamv, g_attn, bp, lam_init)
        o_bc, st_p = _mixer_prompt(g_in, br, c_in, mixer_params(CMLP_CHUNK, CMLP_CHUNK), bp)
        rows_k, logits = jnp.zeros((tp + ts, D_PACK), F32), jnp.zeros((tp + ts, LANES), F32)
        hp1, rows_k, logits = _outproj(o_a, o_bc, hp, wo, *ln1, rw, rb, alpha, 0, rows_k, logits)
        outs["gp"].append(_diag_states(st_p))

        q, ks_all, vs_all, kb, vb, g_in, c_in, br = _inproj(hs, w, tabs_s, l, ks_all, vs_all)
        o_a = _attn_sample(q, kb, vb, cache_k, cache_v, l, page_table, lamv, g_attn, l_new, lam_init)
        rows_s = min(64, ts)
        lc = min(l_new, CMLP_CHUNK)
        o_bc, vn, gs_all = _mixer_sample(g_in, br, c_in, gs_all, l, mixer_params(lc, rows_s), l_new, rows_s)
        hs1, rows_k, logits = _outproj(o_a, o_bc, hs, wo, *ln1, rw, rb, alpha, tp, rows_k, logits)
        outs["cs"].append(vn.reshape(db, l_new, W_C))

        gate, dest, block_e, rows_valid = _route(logits, bm)
        x_pad = _sc_scatter_rows(rows_k, dest, block_e.shape[0] * bm)
        y_pad = _moe_experts(x_pad, block_e, rows_valid, exp_w1, exp_b1.reshape(depth, N_EXPERTS, 1, -1),
                             exp_w2, exp_b2.reshape(depth, N_EXPERTS, 1, -1), l, bm)
        yg = _sc_gather(y_pad, dest.reshape(-1))
        hp = _ln2(hp1, gate, yg, 0, ln2_g[l].reshape(1, -1), ln2_b[l].reshape(1, -1), alpha)
        hs = _ln2(hs1, gate, yg, tp, ln2_g[l].reshape(1, -1), ln2_b[l].reshape(1, -1), alpha)

    return (hp.reshape(bp, s_len, D_MODEL), hs.reshape(db, l_new, D_MODEL),
            kp_all.reshape(depth, bp, s_len, HA, DV_A), vp_all.reshape(depth, bp, s_len, HA, DV_A),
            jnp.stack(outs["gp"]),
            ks_all.reshape(depth, db, l_new, HA, DV_A), vs_all.reshape(depth, db, l_new, HA, DV_A),
            gs_all.reshape(depth, db, HB, DK_B, DV_B), jnp.stack(outs["cs"]))
```
